```python
import math
import jax, jax.numpy as jnp
from jax import lax
import numpy as np

D_MODEL = 1024
BATCH = 4
SEQ = 8192
DEPTH = 2

F32 = jnp.float32
GRID_W = 64
CTX_LEN = 256
EPS = 1e-6
POOL_WIDTH = 256
POOL_WINDOWS = (2, 4, 8, 16)
POOL_GROUPS = len(POOL_WINDOWS)
POOL_GROUP_DIM = POOL_WIDTH // POOL_GROUPS
HY_WIDTH = 256
HY_ORDER = 2
HY_IN = (HY_ORDER + 1) * HY_WIDTH
HY_BANDS = 8
HY_EMB = 1 + 2 * HY_BANDS
HY_FILTER_HIDDEN = 64
HY_MIN_DECAY = math.log(1e-2) / 1.5
HY_MAX_DECAY = math.log(1e-2) / 0.3
MLA_HEADS = 8
MLA_Q_RANK = 256
MLA_KV_RANK = 128
MLA_NOPE = 64
MLA_ROPE = 32
MLA_V = 64
MLA_QK = MLA_NOPE + MLA_ROPE
MLA_WIDTH = MLA_HEADS * MLA_V
ATTN_SCALE = MLA_QK ** -0.5
MIX_WIDTH = POOL_WIDTH + HY_WIDTH + MLA_WIDTH
OFF_HY = POOL_WIDTH
OFF_Q = OFF_HY + HY_IN
OFF_KV = OFF_Q + MLA_Q_RANK
OFF_KR = OFF_KV + MLA_KV_RANK
N_IN = OFF_KR + MLA_ROPE
ROPE_BASE = 10000.0
ROPE_AXIS_PAIRS = MLA_ROPE // 4
Q_BLOCK = 128
N_EXPERTS = 32
TOP_K = 4
D_EXPERT = 1024
SWIGLU_LIMIT = 7.0
SWIGLU_ALPHA = 1.702
EXPERT_BLOCK = 512

kernel_name = "hybrid_pool_hyena_mla_moe_dit"


def rms_norm(x, g):
    xf = x.astype(F32)
    y = xf * lax.rsqrt(jnp.mean(xf * xf, axis=-1, keepdims=True) + EPS)
    return (y * g.astype(F32)).astype(x.dtype)


def axial_rope(n_rows):
    row = jnp.repeat(jnp.arange(n_rows, dtype=F32), GRID_W)
    col = jnp.tile(jnp.arange(GRID_W, dtype=F32), n_rows)
    inv = ROPE_BASE ** (-jnp.arange(ROPE_AXIS_PAIRS, dtype=F32) / ROPE_AXIS_PAIRS)
    ang = jnp.concatenate([row[:, None] * inv, col[:, None] * inv], axis=-1)
    return jnp.cos(ang), jnp.sin(ang)


def apply_rope(x, cos, sin):
    half = MLA_ROPE // 2
    x1 = x[..., :half].astype(F32)
    x2 = x[..., half:].astype(F32)
    c = cos[None, :, None, :]
    s = sin[None, :, None, :]
    return jnp.concatenate([x1 * c - x2 * s, x1 * s + x2 * c], axis=-1).astype(x.dtype)


def pool_mixer(u, pool_w, pool_scale):
    B, L, _ = u.shape
    uf = u.astype(F32)
    cs = jnp.concatenate([jnp.zeros((B, 1, POOL_WIDTH), F32), jnp.cumsum(uf, axis=1)], axis=1)
    t = jnp.arange(L)
    diffs = []
    for gi, w in enumerate(POOL_WINDOWS):
        lo = jnp.clip(t - w // 2, 0, L)
        hi = jnp.clip(t + w // 2, 0, L)
        sl = slice(gi * POOL_GROUP_DIM, (gi + 1) * POOL_GROUP_DIM)
        csg = cs[:, :, sl]
        mean = (csg[:, hi] - csg[:, lo]) / (hi - lo).astype(F32)[None, :, None]
        diffs.append(mean - uf[:, :, sl])
    d = jnp.concatenate(diffs, axis=-1).astype(u.dtype).reshape(B, L, POOL_GROUPS, POOL_GROUP_DIM)
    y = jnp.einsum('blgc,gcd->blgd', d, pool_w).reshape(B, L, POOL_WIDTH)
    return y * pool_scale


def short_conv(u, w, b):
    L = u.shape[1]
    up = jnp.pad(u, ((0, 0), (1, 1), (0, 0)))
    return up[:, :L] * w[0] + up[:, 1:L + 1] * w[1] + up[:, 2:L + 2] * w[2] + b


def hyena_filter(L, w1, b1, w2, b2, w3, freq):
    t = jnp.linspace(0.0, 1.0, L, dtype=F32)[:, None]
    wpos = (2.0 * math.pi / L) * jnp.arange(L, dtype=F32)[:, None]
    f = jnp.linspace(1e-4, HY_BANDS - 1, HY_BANDS, dtype=F32)[None, :]
    z = jnp.concatenate([t, jnp.cos(f * wpos), -jnp.sin(f * wpos)], axis=-1)
    h = jnp.sin(freq * (z @ w1 + b1))
    h = jnp.sin(freq * (h @ w2 + b2))
    h = (h @ w3).astype(F32)
    deltas = jnp.abs(jnp.linspace(HY_MIN_DECAY, HY_MAX_DECAY, HY_WIDTH, dtype=F32))
    decay = jnp.exp(-t * deltas[None, :])
    h_fwd = h[:, :HY_WIDTH] * decay
    h_bwd = h[:, HY_WIDTH:] * decay
    return jnp.concatenate([h_fwd, jnp.zeros((1, HY_WIDTH), F32), h_bwd[1:][::-1]], axis=0)


def bidir_fftconv(u, k2, bias):
    L = u.shape[1]
    uf = u.astype(F32)
    U = jnp.fft.rfft(uf, n=2 * L, axis=1)
    K = jnp.fft.rfft(k2, n=2 * L, axis=0)
    y = jnp.fft.irfft(U * K[None], n=2 * L, axis=1)[:, :L]
    return (y + uf * bias.astype(F32)).astype(u.dtype)


def hyena_mixer(u, lp):
    L = u.shape[1]
    uc = short_conv(u, lp['hy_conv_w'], lp['hy_conv_b'])
    x0 = uc[..., :HY_WIDTH]
    x1 = uc[..., HY_WIDTH:2 * HY_WIDTH]
    v = uc[..., 2 * HY_WIDTH:]
    k2 = hyena_filter(L, lp['hy_f_w1'], lp['hy_f_b1'], lp['hy_f_w2'], lp['hy_f_b2'], lp['hy_f_w3'], lp['hy_freq'])
    return x0 * bidir_fftconv(x1 * v, k2, lp['hy_bias'])


def mla_queries(p, lp, rope):
    B, L, _ = p.shape
    cq = rms_norm(p[..., OFF_Q:OFF_KV], lp['mla_q_norm_g'])
    q = (cq @ lp['mla_w_uq']).reshape(B, L, MLA_HEADS, MLA_QK)
    q = rms_norm(q, lp['qk_norm_q'])
    if rope is not None:
        q = jnp.concatenate([q[..., :MLA_NOPE], apply_rope(q[..., MLA_NOPE:], rope[0], rope[1])], axis=-1)
    return q


def mla_keys_values(p, lp, rope):
    B, L, _ = p.shape
    ckv = rms_norm(p[..., OFF_KV:OFF_KR], lp['mla_kv_norm_g'])
    kv = (ckv @ lp['mla_w_ukv']).reshape(B, L, MLA_HEADS, MLA_NOPE + MLA_V)
    k_rope = jnp.broadcast_to(p[..., None, OFF_KR:N_IN], (B, L, MLA_HEADS, MLA_ROPE))
    k = rms_norm(jnp.concatenate([kv[..., :MLA_NOPE], k_rope], axis=-1), lp['qk_norm_k'])
    if rope is not None:
        k = jnp.concatenate([k[..., :MLA_NOPE], apply_rope(k[..., MLA_NOPE:], rope[0], rope[1])], axis=-1)
    return k, kv[..., MLA_NOPE:]


def softmax_attention(q, k, v):
    s = jnp.einsum('bqhd,bkhd->bhqk', q, k).astype(F32) * ATTN_SCALE
    pr = jax.nn.softmax(s, axis=-1).astype(v.dtype)
    return jnp.einsum('bhqk,bkhd->bqhd', pr, v)


def latent_attention(q, k_all, v_all):
    B, S, H, dk = q.shape
    nb = S // Q_BLOCK
    qb = q.reshape(B, nb, Q_BLOCK, H, dk).swapaxes(0, 1)
    o = lax.map(lambda qblk: softmax_attention(qblk, k_all, v_all), qb)
    return o.swapaxes(0, 1).reshape(B, S, MLA_WIDTH)


def expert_ffn(h, router_w, router_b, w1, b1, w2, b2):
    n_tok, d = h.shape
    logits = h.astype(F32) @ router_w.astype(F32) + router_b.astype(F32)
    top_v, top_i = lax.top_k(logits, TOP_K)
    gates = jax.nn.softmax(top_v, axis=-1)
    n_slots = n_tok * TOP_K
    flat_e = top_i.reshape(n_slots).astype(jnp.int32)
    flat_g = gates.reshape(n_slots)
    flat_tok = jnp.arange(n_slots, dtype=jnp.int32) // TOP_K
    order = jnp.argsort(flat_e)
    sorted_e = flat_e[order]
    counts = jnp.bincount(flat_e, length=N_EXPERTS).astype(jnp.int32)
    padded = (counts + EXPERT_BLOCK - 1) // EXPERT_BLOCK * EXPERT_BLOCK
    padded_end = jnp.cumsum(padded)
    padded_start = padded_end - padded
    start = jnp.cumsum(counts) - counts
    dest = padded_start[sorted_e] + jnp.arange(n_slots, dtype=jnp.int32) - start[sorted_e]
    n_blocks = -(-n_slots // EXPERT_BLOCK) + N_EXPERTS
    n_buf = n_blocks * EXPERT_BLOCK
    buf_tok = jnp.full((n_buf,), n_tok, jnp.int32).at[dest].set(flat_tok[order])
    buf_g = jnp.zeros((n_buf,), F32).at[dest].set(flat_g[order])
    block_start = jnp.arange(n_blocks, dtype=jnp.int32) * EXPERT_BLOCK
    block_e = jnp.minimum(jnp.searchsorted(padded_end, block_start, side='right'), N_EXPERTS - 1).astype(jnp.int32)
    h_pad = jnp.concatenate([h, jnp.zeros((1, d), h.dtype)], axis=0)

    def block_step(out, blk):
        tok, g, e = blk
        a = h_pad[tok] @ w1[e] + b1[e]
        a_glu = jnp.minimum(a[:, :D_EXPERT], SWIGLU_LIMIT)
        a_lin = jnp.clip(a[:, D_EXPERT:], -SWIGLU_LIMIT, SWIGLU_LIMIT)
        act = a_glu * jax.nn.sigmoid(SWIGLU_ALPHA * a_glu) * (a_lin + 1.0)
        y = act @ w2[e] + b2[e]
        return out.at[tok].add((g[:, None] * y).astype(out.dtype)), None

    out, _ = lax.scan(block_step, jnp.zeros((n_tok + 1, d), h.dtype),
                      (buf_tok.reshape(n_blocks, EXPERT_BLOCK), buf_g.reshape(n_blocks, EXPERT_BLOCK), block_e))
    return out[:n_tok]


def mixer_outputs(p, lp):
    pool_out = pool_mixer(p[..., :OFF_HY], lp['pool_w'], lp['pool_scale'])
    hy_out = hyena_mixer(p[..., OFF_HY:OFF_Q], lp)
    return pool_out, hy_out


def hybrid_layer(x, xc, c, c_ctx, lp, rope, last):
    B, S, D = x.shape
    n_ctx = xc.shape[1]
    mod = jax.nn.silu(c) @ lp['w_mod'] + lp['b_mod']
    mod_c = jax.nn.silu(c_ctx) @ lp['w_mod'] + lp['b_mod']
    sh1, sc1, g1, sh2, sc2, g2 = jnp.split(mod[:, None, :], 6, axis=-1)
    csh1, csc1, cg1, csh2, csc2, cg2 = jnp.split(mod_c, 6, axis=-1)

    h = rms_norm(x, lp['norm1_g']) * (1.0 + sc1) + sh1
    hc = rms_norm(xc, lp['norm1_g']) * (1.0 + csc1) + csh1
    p = h @ lp['w_in']
    pc = hc @ lp['w_in']
    kc, vc = mla_keys_values(pc, lp, None)
    kl, vl = mla_keys_values(p, lp, rope)
    ql = mla_queries(p, lp, rope)
    attn = latent_attention(ql, jnp.concatenate([kc, kl], axis=1), jnp.concatenate([vc, vl], axis=1))
    pool_l, hy_l = mixer_outputs(p, lp)
    x = x + g1 * (jnp.concatenate([pool_l, hy_l, attn], axis=-1) @ lp['w_out'])
    if not last:
        qc = mla_queries(pc, lp, None)
        attn_c = softmax_attention(qc, kc, vc).reshape(B, n_ctx, MLA_WIDTH)
        pool_c, hy_c = mixer_outputs(pc, lp)
        xc = xc + cg1 * (jnp.concatenate([pool_c, hy_c, attn_c], axis=-1) @ lp['w_out'])

    h2 = rms_norm(x, lp['norm2_g']) * (1.0 + sc2) + sh2
    moe_args = (lp['router_w'], lp['router_b'], lp['moe_w1'], lp['moe_b1'], lp['moe_w2'], lp['moe_b2'])
    if last:
        y = expert_ffn(h2.reshape(B * S, D), *moe_args)
        x = x + g2 * y.reshape(B, S, D)
    else:
        h2c = rms_norm(xc, lp['norm2_g']) * (1.0 + csc2) + csh2
        n_c = B * n_ctx
        y = expert_ffn(jnp.concatenate([h2c.reshape(n_c, D), h2.reshape(B * S, D)], axis=0), *moe_args)
        xc = xc + cg2 * y[:n_c].reshape(B, n_ctx, D)
        x = x + g2 * y[n_c:].reshape(B, S, D)
    return x, xc


def setup_inputs(seed: int = 0) -> dict:
    key = jax.random.key(seed)
    ks = iter(jax.random.split(key, 40))

    def nrm(shape, scale):
        return jax.random.normal(next(ks), shape, F32) * scale

    D = D_MODEL
    L = DEPTH
    return {
        'x': nrm((BATCH, SEQ, D), 1.0),
        'c': nrm((BATCH, D), 1.0),
        'ctx': nrm((BATCH, CTX_LEN, D), 1.0),
        'c_ctx': nrm((D,), 1.0),
        'norm1_g': 1.0 + nrm((L, D), 0.02),
        'norm2_g': 1.0 + nrm((L, D), 0.02),
        'w_mod': nrm((L, D, 6 * D), 0.5 * D ** -0.5),
        'b_mod': nrm((L, 6 * D), 0.02),
        'w_in': nrm((L, D, N_IN), D ** -0.5),
        'pool_w': nrm((L, POOL_GROUPS, POOL_GROUP_DIM, POOL_GROUP_DIM), POOL_GROUP_DIM ** -0.5),
        'pool_scale': 1.0 + nrm((L, POOL_WIDTH), 0.02),
        'hy_conv_w': nrm((L, 3, HY_IN), 3 ** -0.5),
        'hy_conv_b': nrm((L, HY_IN), 0.02),
        'hy_f_w1': nrm((L, HY_EMB, HY_FILTER_HIDDEN), HY_EMB ** -0.5),
        'hy_f_b1': nrm((L, HY_FILTER_HIDDEN), 0.02),
        'hy_f_w2': nrm((L, HY_FILTER_HIDDEN, HY_FILTER_HIDDEN), HY_FILTER_HIDDEN ** -0.5),
        'hy_f_b2': nrm((L, HY_FILTER_HIDDEN), 0.02),
        'hy_f_w3': nrm((L, HY_FILTER_HIDDEN, 2 * HY_WIDTH), 0.05 * HY_FILTER_HIDDEN ** -0.5),
        'hy_freq': 1.0 + nrm((L, HY_FILTER_HIDDEN), 0.02),
        'hy_bias': nrm((L, HY_WIDTH), 0.1),
        'mla_q_norm_g': 1.0 + nrm((L, MLA_Q_RANK), 0.02),
        'mla_w_uq': nrm((L, MLA_Q_RANK, MLA_HEADS * MLA_QK), MLA_Q_RANK ** -0.5),
        'mla_kv_norm_g': 1.0 + nrm((L, MLA_KV_RANK), 0.02),
        'mla_w_ukv': nrm((L, MLA_KV_RANK, MLA_HEADS * (MLA_NOPE + MLA_V)), MLA_KV_RANK ** -0.5),
        'qk_norm_q': 1.0 + nrm((L, MLA_QK), 0.02),
        'qk_norm_k': 1.0 + nrm((L, MLA_QK), 0.02),
        'w_out': nrm((L, MIX_WIDTH, D), MIX_WIDTH ** -0.5),
        'router_w': nrm((L, D, N_EXPERTS), D ** -0.5),
        'router_b': nrm((L, N_EXPERTS), 0.01),
        'moe_w1': nrm((L, N_EXPERTS, D, 2 * D_EXPERT), D ** -0.5),
        'moe_b1': nrm((L, N_EXPERTS, 2 * D_EXPERT), 0.02),
        'moe_w2': nrm((L, N_EXPERTS, D_EXPERT, D), D_EXPERT ** -0.5),
        'moe_b2': nrm((L, N_EXPERTS, D), 0.02),
    }


def reference(x, c, ctx, c_ctx, norm1_g, norm2_g, w_mod, b_mod, w_in, pool_w, pool_scale,
              hy_conv_w, hy_conv_b, hy_f_w1, hy_f_b1, hy_f_w2, hy_f_b2, hy_f_w3, hy_freq, hy_bias,
              mla_q_norm_g, mla_w_uq, mla_kv_norm_g, mla_w_ukv, qk_norm_q, qk_norm_k, w_out,
              router_w, router_b, moe_w1, moe_b1, moe_w2, moe_b2):
    n_rows = x.shape[1] // GRID_W
    rope = axial_rope(n_rows)
    xc = ctx
    for l in range(DEPTH):
        lp = {
            'norm1_g': norm1_g[l], 'norm2_g': norm2_g[l], 'w_mod': w_mod[l], 'b_mod': b_mod[l],
            'w_in': w_in[l], 'pool_w': pool_w[l], 'pool_scale': pool_scale[l],
            'hy_conv_w': hy_conv_w[l], 'hy_conv_b': hy_conv_b[l], 'hy_f_w1': hy_f_w1[l], 'hy_f_b1': hy_f_b1[l],
            'hy_f_w2': hy_f_w2[l], 'hy_f_b2': hy_f_b2[l], 'hy_f_w3': hy_f_w3[l], 'hy_freq': hy_freq[l],
            'hy_bias': hy_bias[l], 'mla_q_norm_g': mla_q_norm_g[l], 'mla_w_uq': mla_w_uq[l],
            'mla_kv_norm_g': mla_kv_norm_g[l], 'mla_w_ukv': mla_w_ukv[l], 'qk_norm_q': qk_norm_q[l],
            'qk_norm_k': qk_norm_k[l], 'w_out': w_out[l], 'router_w': router_w[l], 'router_b': router_b[l],
            'moe_w1': moe_w1[l], 'moe_b1': moe_b1[l], 'moe_w2': moe_w2[l], 'moe_b2': moe_b2[l],
        }
        x, xc = hybrid_layer(x, xc, c, c_ctx, lp, rope, l == DEPTH - 1)
    return x
```

```python
import functools
import math

import jax
import jax.numpy as jnp
import numpy as np
from jax import lax
from jax.experimental import pallas as pl
from jax.experimental.pallas import tpu as pltpu

F32 = jnp.float32
BF16 = jnp.bfloat16
HIGHEST = lax.Precision.HIGHEST

D_MODEL = 1024
GRID_W = 64
EPS = 1e-6
POOL_WIDTH = 256
POOL_GROUPS = 4
POOL_GROUP_DIM = POOL_WIDTH // POOL_GROUPS
HY_WIDTH = 256
HY_IN = 3 * HY_WIDTH
HY_BANDS = 8
HY_EMB = 1 + 2 * HY_BANDS
HY_HID = 64
HY_MIN_DECAY = math.log(1e-2) / 1.5
HY_MAX_DECAY = math.log(1e-2) / 0.3
HEADS = 8
Q_RANK = 256
KV_RANK = 128
NOPE = 64
ROPE = 32
V_DIM = 64
QK_DIM = NOPE + ROPE
MLA_WIDTH = HEADS * V_DIM
ATTN_SCALE = QK_DIM ** -0.5
OFF_HY = POOL_WIDTH
OFF_Q = OFF_HY + HY_IN
OFF_KV = OFF_Q + Q_RANK
OFF_KR = OFF_KV + KV_RANK
N_IN = OFF_KR + ROPE
N_IN_PAD = OFF_KR + 128
ROPE_BASE = 10000.0
N_EXPERTS = 32
TOP_K = 4
D_EXPERT = 1024
SWIGLU_LIMIT = 7.0
SWIGLU_ALPHA = 1.702

LANE = 128
HEAD_PAD = LANE
ROW_TILE = 512
EXPERT_TILE = 512
VMEM_LIMIT = 56 * 1024 * 1024
N_SEG = 8
CTX_SEG = 4


def _cparams(*sem):
    return pltpu.CompilerParams(dimension_semantics=sem, vmem_limit_bytes=VMEM_LIMIT)


def _rms(x, g, n):
    ss = jnp.sum(x * x, axis=-1, keepdims=True) * (1.0 / n)
    return x * lax.rsqrt(ss + EPS) * g


def _mod_kernel(c_ref, w_ref, b_ref, o_ref):
    c = c_ref[...]
    s = c * (1.0 / (1.0 + jnp.exp(-c)))
    o_ref[...] = jnp.dot(s, w_ref[...], precision=HIGHEST, preferred_element_type=F32) + b_ref[...]


def _modulation(cc, w, b):
    d, n = w.shape
    tn = 1536
    return pl.pallas_call(
        _mod_kernel,
        grid=(n // tn,),
        in_specs=[pl.BlockSpec((N_SEG, d), lambda j: (0, 0)),
                  pl.BlockSpec((d, tn), lambda j: (0, j)),
                  pl.BlockSpec((1, tn), lambda j: (0, j))],
        out_specs=pl.BlockSpec((N_SEG, tn), lambda j: (0, j)),
        out_shape=jax.ShapeDtypeStruct((N_SEG, n), F32),
        compiler_params=_cparams("arbitrary"),
        name="modulation",
    )(cc, w, b)


def _inproj_kernel(x_ref, gn_ref, sc_ref, sh_ref, win_ref, gq_ref, wq_ref, gkv_ref, wkv_ref,
                   gqq_ref, gqk_ref, vone_ref, rc_ref, ra_ref, rb_ref,
                   pph_ref, q_ref, k_ref, v_ref):
    x = x_ref[0]
    h = _rms(x, gn_ref[...], D_MODEL) * (1.0 + sc_ref[0]) + sh_ref[0]
    p = jnp.dot(h.astype(BF16), win_ref[...], preferred_element_type=F32)
    pph_ref[0] = p[:, :OFF_Q]
    cq = _rms(p[:, OFF_Q:OFF_KV], gq_ref[...], Q_RANK)
    qall = jnp.dot(cq.astype(BF16), wq_ref[...], preferred_element_type=F32)
    ckv = _rms(p[:, OFF_KV:OFF_KR], gkv_ref[...], KV_RANK)
    kin = jnp.concatenate([ckv, p[:, OFF_KR:N_IN_PAD]], axis=-1).astype(BF16)
    kvall = jnp.dot(kin, wkv_ref[...], preferred_element_type=F32)
    rc, ra, rb = rc_ref[...], ra_ref[...], rb_ref[...]

    def rope(xh):
        return xh * rc + pltpu.roll(xh, HEAD_PAD - ROPE // 2, axis=1) * ra + pltpu.roll(xh, ROPE // 2, axis=1) * rb

    for hd in range(HEADS):
        sl = slice(HEAD_PAD * hd, HEAD_PAD * (hd + 1))
        qh = rope(_rms(qall[:, sl], gqq_ref[...], QK_DIM)) * ATTN_SCALE
        q_ref[0, hd] = qh.astype(BF16)
        kh = rope(_rms(kvall[:, sl], gqk_ref[...], QK_DIM))
        k_ref[0, hd] = kh.astype(BF16)
        vs = slice(HEADS * HEAD_PAD + HEAD_PAD * hd, HEADS * HEAD_PAD + HEAD_PAD * (hd + 1))
        v_ref[0, hd] = (kvall[:, vs] + vone_ref[...]).astype(BF16)


def _inproj(x, seg_of_batch, wts, mod, rope_tabs):
    b, s, d = x.shape
    t = min(ROW_TILE, s)
    full = lambda shape: pl.BlockSpec(shape, lambda bi, i: (0,) * len(shape))
    seg = lambda: pl.BlockSpec((1, 1, d), lambda bi, i: (seg_of_batch(bi), 0, 0))
    tab = lambda: pl.BlockSpec((t, HEAD_PAD), lambda bi, i: (i, 0))
    hw = HEADS * HEAD_PAD
    head_out = lambda: pl.BlockSpec((1, HEADS, t, HEAD_PAD), lambda bi, i: (bi, 0, i, 0))
    return pl.pallas_call(
        _inproj_kernel,
        grid=(b, s // t),
        in_specs=[pl.BlockSpec((1, t, d), lambda bi, i: (bi, i, 0)),
                  full((1, d)), seg(), seg(),
                  full((d, N_IN_PAD)), full((1, Q_RANK)), full((Q_RANK, hw)),
                  full((1, KV_RANK)), full((2 * KV_RANK, 2 * hw)),
                  full((1, HEAD_PAD)), full((1, HEAD_PAD)), full((1, HEAD_PAD)),
                  tab(), tab(), tab()],
        out_specs=[pl.BlockSpec((1, t, OFF_Q), lambda bi, i: (bi, i, 0)),
                   head_out(), head_out(), head_out()],
        out_shape=[jax.ShapeDtypeStruct((b, s, OFF_Q), F32)]
        + [jax.ShapeDtypeStruct((b, HEADS, s, HEAD_PAD), BF16)] * 3,
        compiler_params=_cparams("parallel", "parallel"),
        name="inproj",
    )(x, wts["norm1_g"], mod["sc1"], mod["sh1"], wts["w_in"], wts["gq"], wts["wq"], wts["gkv"], wts["wkv"],
      wts["gqq"], wts["gqk"], wts["vone"], *rope_tabs)


def _seqmix_kernel(prev_ref, cur_ref, next_ref, wpool_ref, pscale_ref, cw_ref, cb_ref,
                   pool_ref, x0_ref, z_ref, *, seq_len):
    i = pl.program_id(1)
    n = pl.num_programs(1)
    t_rows = cur_ref.shape[1]
    halo = prev_ref.shape[1]
    r_rows = t_rows + 2 * halo
    prev = jnp.where(i > 0, prev_ref[0], 0.0)
    nxt = jnp.where(i < n - 1, next_ref[0], 0.0)
    ext = jnp.concatenate([prev, cur_ref[0], nxt], axis=0)

    def sh(a, dlt):
        return pltpu.roll(a, (-dlt) % r_rows, axis=0)

    u = ext[:, :POOL_WIDTH]
    a2 = u + sh(u, -1)
    a4 = sh(a2, -1) + sh(a2, 1)
    a8 = sh(a4, -2) + sh(a4, 2)
    a16 = sh(a8, -4) + sh(a8, 4)
    grp = lax.broadcasted_iota(jnp.int32, (1, POOL_WIDTH), 1) // POOL_GROUP_DIM
    win = jnp.where(grp == 0, a2, jnp.where(grp == 1, a4, jnp.where(grp == 2, a8, a16)))
    half = jnp.where(grp == 0, 1, jnp.where(grp == 1, 2, jnp.where(grp == 2, 4, 8)))
    pos = i * t_rows - halo + lax.broadcasted_iota(jnp.int32, (r_rows, 1), 0)
    cnt = jnp.minimum(pos + half, seq_len) - jnp.maximum(pos - half, 0)
    cnt = jnp.maximum(cnt, 1).astype(F32)
    dev = (win / cnt - u)[halo:halo + t_rows]
    pool = jnp.dot(dev.astype(BF16), wpool_ref[...], preferred_element_type=F32) * pscale_ref[...]
    pool_ref[0] = pool.astype(BF16)

    e = ext[:, OFF_HY:OFF_Q]
    cw = cw_ref[...]
    uc = sh(e, -1) * cw[0:1] + e * cw[1:2] + sh(e, 1) * cw[2:3] + cb_ref[...]
    uc = uc[halo:halo + t_rows]
    x0_ref[0] = uc[:, :HY_WIDTH]
    z_ref[0] = uc[:, HY_WIDTH:2 * HY_WIDTH] * uc[:, 2 * HY_WIDTH:]


def _seqmix(pph, wts):
    b, s, w = pph.shape
    t = min(ROW_TILE, s)
    halo = 8
    per = t // halo
    nh = s // halo
    full = lambda shape: pl.BlockSpec(shape, lambda bi, i: (0,) * len(shape))
    out = lambda: pl.BlockSpec((1, t, HY_WIDTH), lambda bi, i: (bi, i, 0))
    return pl.pallas_call(
        functools.partial(_seqmix_kernel, seq_len=s),
        grid=(b, s // t),
        in_specs=[pl.BlockSpec((1, halo, w), lambda bi, i: (bi, jnp.maximum(i * per - 1, 0), 0)),
                  pl.BlockSpec((1, t, w), lambda bi, i: (bi, i, 0)),
                  pl.BlockSpec((1, halo, w), lambda bi, i: (bi, jnp.minimum((i + 1) * per, nh - 1), 0)),
                  full((POOL_WIDTH, POOL_WIDTH)), full((1, POOL_WIDTH)), full((3, HY_IN)), full((1, HY_IN))],
        out_specs=[out(), out(), out()],
        out_shape=[jax.ShapeDtypeStruct((b, s, POOL_WIDTH), BF16),
                   jax.ShapeDtypeStruct((b, s, HY_WIDTH), F32),
                   jax.ShapeDtypeStruct((b, s, HY_WIDTH), F32)],
        compiler_params=_cparams("parallel", "parallel"),
        name="seqmix",
    )(pph, pph, pph, wts["wpool"], wts["pool_scale"], wts["conv_w"], wts["conv_b"])


def _leftmm_kernel(*refs, cmul, gate):
    m_ref, x_ref = refs[0], refs[1]
    nxt = 2
    x = x_ref[0]
    if cmul:
        ks = refs[nxt][...]
        nxt += 1
        kh = x.shape[0] // 2
        xr, xi, kr, ki = x[:kh], x[kh:], ks[:kh], ks[kh:]
        x = jnp.concatenate([xr * kr - xi * ki, xr * ki + xi * kr], axis=0)
    y = jnp.dot(m_ref[...], x, precision=HIGHEST, preferred_element_type=F32)
    if gate:
        z_ref, bias_ref, x0_ref = refs[nxt:nxt + 3]
        nxt += 3
        y = (y + z_ref[0] * bias_ref[...]) * x0_ref[0]
    o_ref = refs[nxt]
    o_ref[0] = y.astype(o_ref.dtype)


def _leftmm(m, x, spectrum=None, gate=None, out_dtype=F32):
    g, k, cols = x.shape
    mo = m.shape[0]
    tc = min(cols, 2048)
    args = [m, x]
    in_specs = [pl.BlockSpec((mo, k), lambda gi, j: (0, 0)), pl.BlockSpec((1, k, tc), lambda gi, j: (gi, 0, j))]
    if spectrum is not None:
        args.append(spectrum)
        in_specs.append(pl.BlockSpec((k, tc), lambda gi, j: (0, j)))
    if gate is not None:
        z, bias, x0 = gate
        args += [z, bias, x0]
        in_specs += [pl.BlockSpec((1, mo, tc), lambda gi, j: (gi, 0, j)),
                     pl.BlockSpec((1, tc), lambda gi, j: (0, j)),
                     pl.BlockSpec((1, mo, tc), lambda gi, j: (gi, 0, j))]
    return pl.pallas_call(
        functools.partial(_leftmm_kernel, cmul=spectrum is not None, gate=gate is not None),
        grid=(g, cols // tc),
        in_specs=in_specs,
        out_specs=pl.BlockSpec((1, mo, tc), lambda gi, j: (gi, 0, j)),
        out_shape=jax.ShapeDtypeStruct((g, mo, cols), out_dtype),
        compiler_params=_cparams("parallel", "parallel"),
        name="dft_leftmm",
    )(*args)


def _blockfft_kernel(*refs, n2, blocks, conv):
    if conv:
        a_ref, tw_ref, ks_ref, m2_ref, m2c_ref, o_ref = refs
    else:
        a_ref, tw_ref, m2_ref, o_ref = refs

    def body(j, carry):
        rows = pl.ds(pl.multiple_of(j * n2, n2), n2)
        ar, ai = a_ref[0, 0, rows, :], a_ref[0, 1, rows, :]
        tcs, tsn = tw_ref[0, rows, :], tw_ref[1, rows, :]
        tcs = jnp.concatenate([tcs] * (ar.shape[1] // tcs.shape[1]), axis=1)
        tsn = jnp.concatenate([tsn] * (ar.shape[1] // tsn.shape[1]), axis=1)
        bstk = jnp.concatenate([ar * tcs + ai * tsn, ai * tcs - ar * tsn], axis=0)
        xs = jnp.dot(m2_ref[...], bstk, precision=HIGHEST, preferred_element_type=F32)
        if not conv:
            o_ref[0, 0, rows, :] = xs[:n2]
            o_ref[0, 1, rows, :] = xs[n2:]
            return carry
        xr, xi = xs[:n2], xs[n2:]
        kr, ki = ks_ref[0, rows, :], ks_ref[1, rows, :]
        ystk = jnp.concatenate([xr * kr - xi * ki, xr * ki + xi * kr], axis=0)
        cs = jnp.dot(m2c_ref[...], ystk, precision=HIGHEST, preferred_element_type=F32)
        cr, ci = cs[:n2], cs[n2:]
        o_ref[0, 0, rows, :] = cr * tcs - ci * tsn
        o_ref[0, 1, rows, :] = cr * tsn + ci * tcs
        return carry

    lax.fori_loop(0, blocks, body, 0)


def _blockfft(a, tw, m2, spectrum=None, m2c=None):
    g, _, n, c = a.shape
    n2 = m2.shape[0] // 2
    blocks = min(8, n // n2)
    rows = blocks * n2
    conv = spectrum is not None
    args = [a, tw]
    in_specs = [pl.BlockSpec((1, 2, rows, c), lambda gi, j: (gi, 0, j, 0)),
                pl.BlockSpec((2, rows, LANE), lambda gi, j: (0, j, 0))]
    if conv:
        args.append(spectrum)
        in_specs.append(pl.BlockSpec((2, rows, c), lambda gi, j: (0, j, 0)))
    args.append(m2)
    in_specs.append(pl.BlockSpec(m2.shape, lambda gi, j: (0, 0)))
    if conv:
        args.append(m2c)
        in_specs.append(pl.BlockSpec(m2c.shape, lambda gi, j: (0, 0)))
    return pl.pallas_call(
        functools.partial(_blockfft_kernel, n2=n2, blocks=blocks, conv=conv),
        grid=(g, n // rows),
        in_specs=in_specs,
        out_specs=pl.BlockSpec((1, 2, rows, c), lambda gi, j: (gi, 0, j, 0)),
        out_shape=jax.ShapeDtypeStruct((g, 2, n, c), F32),
        compiler_params=_cparams("parallel", "parallel"),
        name="dft_block",
    )(*args)


def _filter_kernel(z_ref, w1_ref, b1_ref, w2_ref, b2_ref, w3_ref, fr_ref, dl_ref, o_ref, *, seq_len):
    i = pl.program_id(0)
    rows = z_ref.shape[0]
    z = z_ref[...]
    fr = fr_ref[...]
    h = jnp.sin(fr * (jnp.dot(z, w1_ref[...], precision=HIGHEST, preferred_element_type=F32) + b1_ref[...]))
    h = jnp.sin(fr * (jnp.dot(h, w2_ref[...], precision=HIGHEST, preferred_element_type=F32) + b2_ref[...]))
    hf = jnp.dot(h, w3_ref[...], precision=HIGHEST, preferred_element_type=F32)
    decay = jnp.exp(-z[:, 0:1] * dl_ref[...])
    j = i * rows + lax.broadcasted_iota(jnp.int32, (rows, 1), 0)
    taps = jnp.where(j < seq_len, hf[:, :HY_WIDTH], jnp.where(j > seq_len, hf[:, HY_WIDTH:], 0.0))
    o_ref[...] = taps * decay


def _hyena_filter(seq_len, wts):
    n = 2 * seq_len
    t = jnp.linspace(0.0, 1.0, seq_len, dtype=F32)[:, None]
    wpos = (2.0 * math.pi / seq_len) * jnp.arange(seq_len, dtype=F32)[:, None]
    f = jnp.linspace(1e-4, HY_BANDS - 1, HY_BANDS, dtype=F32)[None, :]
    feat = jnp.concatenate([t, jnp.cos(f * wpos), -jnp.sin(f * wpos)], axis=-1)
    j = jnp.arange(n)
    lag = jnp.clip(jnp.where(j < seq_len, j, n - j), 0, seq_len - 1)
    feat2 = jnp.pad(feat[lag], ((0, 0), (0, LANE - HY_EMB)))
    rows = min(n, 2048)
    full = lambda shape: pl.BlockSpec(shape, lambda i: (0,) * len(shape))
    return pl.pallas_call(
        functools.partial(_filter_kernel, seq_len=seq_len),
        grid=(n // rows,),
        in_specs=[pl.BlockSpec((rows, LANE), lambda i: (i, 0)),
                  full((LANE, LANE)), full((1, LANE)), full((LANE, LANE)), full((1, LANE)),
                  full((LANE, 2 * HY_WIDTH)), full((1, LANE)), full((1, HY_WIDTH))],
        out_specs=pl.BlockSpec((rows, HY_WIDTH), lambda i: (i, 0)),
        out_shape=jax.ShapeDtypeStruct((n, HY_WIDTH), F32),
        compiler_params=_cparams("parallel"),
        name="hyena_filter",
    )(feat2, wts["f_w1"], wts["f_b1"], wts["f_w2"], wts["f_b2"], wts["f_w3"], wts["f_freq"], wts["f_delta"])


def _cs(p, rows, cols):
    ang = 2.0 * np.pi * np.outer(np.arange(rows), np.arange(cols)) / p
    return np.cos(ang), np.sin(ang)


def _mat(blocks):
    return jnp.asarray(np.block(blocks), dtype=F32)


def _hyena_long(z, x0, bias, filt):
    b, s, c = z.shape
    n = 2 * s
    n1 = 1 << ((n.bit_length() - 1 + 1) // 2)
    n2 = n // n1
    g = b // 2
    cols = n2 * c
    c1, s1 = _cs(n1, n1, n1)
    h1 = n1 // 2
    m_fwd = _mat([[c1[:, :h1], s1[:, :h1]], [-s1[:, :h1], c1[:, :h1]]])
    m_filt = _mat([[c1], [-s1]])
    m_inv = _mat([[c1[:h1], -s1[:h1]], [s1[:h1], c1[:h1]]]) * (1.0 / n)
    c2, s2 = _cs(n2, n2, n2)
    m2 = _mat([[c2, s2], [-s2, c2]])
    m2c = _mat([[c2, -s2], [s2, c2]])
    k1 = jnp.arange(n1, dtype=jnp.int32)[:, None]
    nn2 = jnp.arange(n2, dtype=jnp.int32)[None, :]
    ang = (2.0 * math.pi / n) * (k1 * nn2).astype(F32).reshape(n)
    tw = jnp.broadcast_to(jnp.stack([jnp.cos(ang), jnp.sin(ang)])[:, :, None], (2, n, LANE))

    spec = _leftmm(m_filt, filt.reshape(1, n1, cols)).reshape(1, 2, n, c)
    spec = _blockfft(spec, tw, m2)[0]
    zv = z.reshape(g, n1, cols)
    a = _leftmm(m_fwd, zv).reshape(g, 2, n, c)
    dd = _blockfft(a, tw, m2, spectrum=spec, m2c=m2c).reshape(g, 2 * n1, cols)
    bias_t = jnp.tile(bias, (1, n2))
    y = _leftmm(m_inv, dd, gate=(zv, bias_t, x0.reshape(g, n1, cols)), out_dtype=BF16)
    return y.reshape(b, s, c)


def _hyena_short(z, x0, bias, filt):
    b, s, c = z.shape
    n = 2 * s
    g = b // 2
    cf, sf = _cs(n, n, n)
    m_fwd = _mat([[cf[:, :s], sf[:, :s]], [-sf[:, :s], cf[:, :s]]])
    m_filt = _mat([[cf], [-sf]])
    m_inv = _mat([[cf[:s], -sf[:s]], [sf[:s], cf[:s]]]) * (1.0 / n)
    spec = _leftmm(m_filt, filt.reshape(1, n, c))[0]
    zv = z.reshape(g, n, c)
    zf = _leftmm(m_fwd, zv)
    y = _leftmm(m_inv, zf, spectrum=spec, gate=(zv, bias, x0.reshape(g, n, c)), out_dtype=BF16)
    return y.reshape(b, s, c)


def _attn_kernel(*refs, tk, n_chunks, heads):
    if n_chunks:
        q_ref, kc_ref, vc_ref, kl_ref, vl_ref, o_ref = refs
    else:
        q_ref, kc_ref, vc_ref, o_ref = refs
    dn = (((1,), (1,)), ((), ()))
    outs = []
    for hh in range(heads):
        q = q_ref[0, hh]
        s = lax.dot_general(q, kc_ref[0, hh], dn, preferred_element_type=F32)
        m = jnp.max(s, axis=-1, keepdims=True)
        p = jnp.exp(s - m)
        acc = jnp.dot(p.astype(BF16), vc_ref[0, hh], preferred_element_type=F32)

        def body(j, carry, q=q, hh=hh):
            m_old, acc_old = carry
            rows = pl.ds(pl.multiple_of(j * tk, tk), tk)
            sj = lax.dot_general(q, kl_ref[0, hh, rows, :], dn, preferred_element_type=F32)
            m_new = jnp.maximum(m_old, jnp.max(sj, axis=-1, keepdims=True))
            pj = jnp.exp(sj - m_new)
            acc_new = jnp.exp(m_old - m_new) * acc_old + jnp.dot(pj.astype(BF16), vl_ref[0, hh, rows, :],
                                                                 preferred_element_type=F32)
            return m_new, acc_new

        if n_chunks:
            m, acc = lax.fori_loop(0, n_chunks, body, (m, acc))
        outs.append(acc[:, :V_DIM] / acc[:, V_DIM:V_DIM + 1])
    o_ref[0] = jnp.concatenate(outs, axis=-1).astype(o_ref.dtype)


def _attention(q, kc, vc, kl=None, vl=None):
    b, h, s, dp = q.shape
    nc = kc.shape[2]
    tq = min(ROW_TILE, s)
    heads = LANE // V_DIM
    tk = 512
    n_chunks = 0 if kl is None else kl.shape[2] // tk
    args = [q, kc, vc]
    in_specs = [pl.BlockSpec((1, heads, tq, dp), lambda bi, hi, i: (bi, hi, i, 0)),
                pl.BlockSpec((1, heads, nc, dp), lambda bi, hi, i: (bi, hi, 0, 0)),
                pl.BlockSpec((1, heads, nc, dp), lambda bi, hi, i: (bi, hi, 0, 0))]
    if n_chunks:
        sl = kl.shape[2]
        args += [kl, vl]
        in_specs += [pl.BlockSpec((1, heads, sl, dp), lambda bi, hi, i: (bi, hi, 0, 0))] * 2
    return pl.pallas_call(
        functools.partial(_attn_kernel, tk=tk, n_chunks=n_chunks, heads=heads),
        grid=(b, h // heads, s // tq),
        in_specs=in_specs,
        out_specs=pl.BlockSpec((1, tq, heads * V_DIM), lambda bi, hi, i: (bi, i, hi)),
        out_shape=jax.ShapeDtypeStruct((b, s, h * V_DIM), BF16),
        compiler_params=_cparams("parallel", "parallel", "parallel"),
        name="attention",
    )(*args)


def _outproj_kernel(x_ref, pool_ref, hy_ref, at_ref, wout_ref, g1_ref, gn_ref, sc_ref, sh_ref, rw_ref, rb_ref,
                    x1_ref, h2_ref, lg_ref):
    mix = jnp.concatenate([pool_ref[0], hy_ref[0], at_ref[0]], axis=-1)
    y = jnp.dot(mix, wout_ref[...], preferred_element_type=F32)
    x1 = x_ref[0] + g1_ref[0] * y
    x1_ref[0] = x1
    h2 = _rms(x1, gn_ref[...], D_MODEL) * (1.0 + sc_ref[0]) + sh_ref[0]
    h2_ref[0] = h2.astype(BF16)
    lg_ref[0] = jnp.dot(h2, rw_ref[...], precision=HIGHEST, preferred_element_type=F32) + rb_ref[...]


def _outproj(x, pool, hy, attn, seg_of_batch, wts, mod):
    b, s, d = x.shape
    t = min(ROW_TILE, s)
    full = lambda shape: pl.BlockSpec(shape, lambda bi, i: (0,) * len(shape))
    seg = lambda: pl.BlockSpec((1, 1, d), lambda bi, i: (seg_of_batch(bi), 0, 0))
    rowblk = lambda w: pl.BlockSpec((1, t, w), lambda bi, i: (bi, i, 0))
    return pl.pallas_call(
        _outproj_kernel,
        grid=(b, s // t),
        in_specs=[rowblk(d), rowblk(POOL_WIDTH), rowblk(HY_WIDTH), rowblk(MLA_WIDTH),
                  full((d, d)), seg(), full((1, d)), seg(), seg(), full((d, LANE)), full((1, LANE))],
        out_specs=[rowblk(d), rowblk(d), rowblk(LANE)],
        out_shape=[jax.ShapeDtypeStruct((b, s, d), F32), jax.ShapeDtypeStruct((b, s, d), BF16),
                   jax.ShapeDtypeStruct((b, s, LANE), F32)],
        compiler_params=_cparams("parallel", "parallel"),
        name="outproj",
    )(x, pool, hy, attn, wts["w_out"], mod["g1"], wts["norm2_g"], mod["sc2"], mod["sh2"],
      wts["router_w"], wts["router_b"])


def _moe_kernel(be_ref, nu_ref, x_ref, w1_ref, b1_ref, w2_ref, b2_ref, o_ref, w1b_ref, w2b_ref):
    i = pl.program_id(0)
    e = be_ref[i]
    e_prev = be_ref[jnp.maximum(i - 1, 0)]

    @pl.when((i == 0) | (e != e_prev))
    def _():
        w1b_ref[...] = w1_ref[0].astype(BF16)
        w2b_ref[...] = w2_ref[0].astype(BF16)

    @pl.when(i < nu_ref[0])
    def _():
        a = jnp.dot(x_ref[...], w1b_ref[...], preferred_element_type=F32) + b1_ref[0]
        glu = jnp.minimum(a[:, :D_EXPERT], SWIGLU_LIMIT)
        lin = jnp.clip(a[:, D_EXPERT:], -SWIGLU_LIMIT, SWIGLU_LIMIT)
        act = glu * (1.0 / (1.0 + jnp.exp(-SWIGLU_ALPHA * glu))) * (lin + 1.0)
        y = jnp.dot(act.astype(BF16), w2b_ref[...], preferred_element_type=F32) + b2_ref[0]
        o_ref[...] = y.astype(o_ref.dtype)

    @pl.when(i >= nu_ref[0])
    def _():
        o_ref[...] = jnp.zeros_like(o_ref)


def _experts(xs, block_e, n_used, w1, b1, w2, b2):
    n_buf, d = xs.shape
    tm = EXPERT_TILE
    n_blocks = n_buf // tm
    de2 = w1.shape[2]
    grid_spec = pltpu.PrefetchScalarGridSpec(
        num_scalar_prefetch=2,
        grid=(n_blocks,),
        in_specs=[pl.BlockSpec((tm, d), lambda i, be, nu: (i, 0)),
                  pl.BlockSpec((1, d, de2), lambda i, be, nu: (be[i], 0, 0)),
                  pl.BlockSpec((1, 1, de2), lambda i, be, nu: (be[i], 0, 0)),
                  pl.BlockSpec((1, D_EXPERT, d), lambda i, be, nu: (be[i], 0, 0)),
                  pl.BlockSpec((1, 1, d), lambda i, be, nu: (be[i], 0, 0))],
        out_specs=pl.BlockSpec((tm, d), lambda i, be, nu: (i, 0)),
        scratch_shapes=[pltpu.VMEM((d, de2), BF16), pltpu.VMEM((D_EXPERT, d), BF16)],
    )
    return pl.pallas_call(
        _moe_kernel,
        grid_spec=grid_spec,
        out_shape=jax.ShapeDtypeStruct((n_buf, d), BF16),
        compiler_params=_cparams("arbitrary"),
        name="experts",
    )(block_e, n_used, xs, w1, b1, w2, b2)


def _route(logits):
    n_tok = logits.shape[0]
    tm = EXPERT_TILE
    top_v, top_i = lax.top_k(logits, TOP_K)
    gates = jax.nn.softmax(top_v, axis=-1)
    n_slots = n_tok * TOP_K
    flat_e = top_i.reshape(n_slots).astype(jnp.int32)
    order = jnp.argsort(flat_e).astype(jnp.int32)
    sorted_e = flat_e[order]
    counts = jnp.bincount(flat_e, length=N_EXPERTS).astype(jnp.int32)
    padded = (counts + tm - 1) // tm * tm
    padded_end = jnp.cumsum(padded)
    padded_start = padded_end - padded
    start = jnp.cumsum(counts) - counts
    dest = padded_start[sorted_e] + jnp.arange(n_slots, dtype=jnp.int32) - start[sorted_e]
    n_blocks = n_slots // tm + N_EXPERTS
    buf_tok = jnp.zeros((n_blocks * tm,), jnp.int32).at[dest].set(order // TOP_K)
    slot_pos = jnp.zeros((n_slots,), jnp.int32).at[order].set(dest)
    block_start = jnp.arange(n_blocks, dtype=jnp.int32) * tm
    block_e = jnp.minimum(jnp.searchsorted(padded_end, block_start, side="right"), N_EXPERTS - 1).astype(jnp.int32)
    n_used = (padded_end[-1] // tm).astype(jnp.int32).reshape(1)
    return gates, buf_tok, slot_pos.reshape(n_tok, TOP_K), block_e, n_used


def _combine_kernel(x_ref, y_ref, gt_ref, g2_ref, o_ref):
    gt = gt_ref[...]
    acc = gt[:, 0:1] * y_ref[0].astype(F32)
    for j in range(1, TOP_K):
        acc = acc + gt[:, j:j + 1] * y_ref[j].astype(F32)
    o_ref[0] = x_ref[0] + g2_ref[0] * acc


def _combine(x, ysel, gates, tok_offset, seg_of_batch, mod):
    b, s, d = x.shape
    t = min(ROW_TILE, s)
    per = s // t
    off = tok_offset // t
    return pl.pallas_call(
        _combine_kernel,
        grid=(b, per),
        in_specs=[pl.BlockSpec((1, t, d), lambda bi, i: (bi, i, 0)),
                  pl.BlockSpec((TOP_K, t, d), lambda bi, i: (0, off + bi * per + i, 0)),
                  pl.BlockSpec((t, TOP_K), lambda bi, i: (off + bi * per + i, 0)),
                  pl.BlockSpec((1, 1, d), lambda bi, i: (seg_of_batch(bi), 0, 0))],
        out_specs=pl.BlockSpec((1, t, d), lambda bi, i: (bi, i, 0)),
        out_shape=jax.ShapeDtypeStruct((b, s, d), F32),
        compiler_params=_cparams("parallel", "parallel"),
        name="combine",
    )(x, ysel, gates, mod["g2"])


def _rope_tables(seq_len):
    n_rows = seq_len // GRID_W
    row = jnp.repeat(jnp.arange(n_rows, dtype=F32), GRID_W)
    col = jnp.tile(jnp.arange(GRID_W, dtype=F32), n_rows)
    pairs = ROPE // 4
    inv = ROPE_BASE ** (-jnp.arange(pairs, dtype=F32) / pairs)
    ang = jnp.concatenate([row[:, None] * inv, col[:, None] * inv], axis=-1)
    cos, sin = jnp.cos(ang), jnp.sin(ang)
    half = ROPE // 2
    one = jnp.ones((seq_len, NOPE), F32)
    zn = jnp.zeros((seq_len, NOPE), F32)
    zh = jnp.zeros((seq_len, half), F32)
    zp = jnp.zeros((seq_len, HEAD_PAD - QK_DIM), F32)
    return (jnp.concatenate([one, cos, cos, zp], axis=-1),
            jnp.concatenate([zn, -sin, zh, zp], axis=-1),
            jnp.concatenate([zn, zh, sin, zp], axis=-1))


def _identity_rope(seq_len):
    keep = jnp.concatenate([jnp.ones((seq_len, QK_DIM), F32), jnp.zeros((seq_len, HEAD_PAD - QK_DIM), F32)], axis=-1)
    zero = jnp.zeros((seq_len, HEAD_PAD), F32)
    return keep, zero, zero


def _pad_cols(w, n):
    return jnp.pad(w, ((0, 0), (0, n - w.shape[1])))


def _layer_weights(l, p):
    hw = HEADS * HEAD_PAD
    wq = jnp.pad(p["mla_w_uq"][l].reshape(Q_RANK, HEADS, QK_DIM), ((0, 0), (0, 0), (0, HEAD_PAD - QK_DIM)))
    wukv = p["mla_w_ukv"][l].reshape(KV_RANK, HEADS, NOPE + V_DIM)
    wk = jnp.pad(wukv[:, :, :NOPE], ((0, 0), (0, 0), (0, HEAD_PAD - NOPE))).reshape(KV_RANK, hw)
    wv = jnp.pad(wukv[:, :, NOPE:], ((0, 0), (0, 0), (0, HEAD_PAD - V_DIM))).reshape(KV_RANK, hw)
    place = jnp.pad(jnp.eye(ROPE, dtype=F32), ((0, KV_RANK - ROPE), (NOPE, HEAD_PAD - QK_DIM)))
    wk_rope = jnp.tile(place, (1, HEADS))
    wkv = jnp.concatenate([jnp.concatenate([wk, wv], axis=1),
                           jnp.concatenate([wk_rope, jnp.zeros((KV_RANK, hw), F32)], axis=1)], axis=0)
    wpool = jax.scipy.linalg.block_diag(*[p["pool_w"][l, g] for g in range(POOL_GROUPS)])
    vone = jnp.zeros((1, HEAD_PAD), F32).at[0, V_DIM].set(1.0)
    deltas = jnp.abs(jnp.linspace(HY_MIN_DECAY, HY_MAX_DECAY, HY_WIDTH, dtype=F32))[None, :]
    row = lambda v, n=None: (v if n is None else jnp.pad(v, (0, n - v.shape[0])))[None, :]
    return {
        "norm1_g": row(p["norm1_g"][l]), "norm2_g": row(p["norm2_g"][l]),
        "w_in": _pad_cols(p["w_in"][l], N_IN_PAD).astype(BF16),
        "gq": row(p["mla_q_norm_g"][l]), "wq": wq.reshape(Q_RANK, hw).astype(BF16),
        "gkv": row(p["mla_kv_norm_g"][l]), "wkv": wkv.astype(BF16),
        "gqq": row(p["qk_norm_q"][l], HEAD_PAD), "gqk": row(p["qk_norm_k"][l], HEAD_PAD), "vone": vone,
        "wpool": wpool.astype(BF16), "pool_scale": row(p["pool_scale"][l]),
        "conv_w": p["hy_conv_w"][l], "conv_b": row(p["hy_conv_b"][l]),
        "f_w1": jnp.pad(p["hy_f_w1"][l], ((0, LANE - HY_EMB), (0, LANE - HY_HID))),
        "f_b1": row(p["hy_f_b1"][l], LANE),
        "f_w2": jnp.pad(p["hy_f_w2"][l], ((0, LANE - HY_HID), (0, LANE - HY_HID))),
        "f_b2": row(p["hy_f_b2"][l], LANE),
        "f_w3": jnp.pad(p["hy_f_w3"][l], ((0, LANE - HY_HID), (0, 0))),
        "f_freq": row(p["hy_freq"][l], LANE), "f_delta": deltas,
        "hy_bias": row(p["hy_bias"][l]),
        "w_out": p["w_out"][l].astype(BF16),
        "router_w": _pad_cols(p["router_w"][l], LANE), "router_b": row(p["router_b"][l], LANE),
        "moe_w1": p["moe_w1"][l], "moe_b1": p["moe_b1"][l][:, None, :],
        "moe_w2": p["moe_w2"][l], "moe_b2": p["moe_b2"][l][:, None, :],
    }


def _layer(x, xc, cc, wts, w_mod, b_mod, rope_main, rope_ctx, last):
    b, s, d = x.shape
    n_ctx = xc.shape[1]
    modm = _modulation(cc, w_mod, b_mod)
    names = ("sh1", "sc1", "g1", "sh2", "sc2", "g2")
    mod = {nm: modm[:, i * d:(i + 1) * d].reshape(N_SEG, 1, d) for i, nm in enumerate(names)}
    main_seg = lambda bi: bi
    ctx_seg = lambda bi: CTX_SEG

    pph, q, kl, vl = _inproj(x, main_seg, wts, mod, rope_main)
    pph_c, qc, kc, vc = _inproj(xc, ctx_seg, wts, mod, rope_ctx)
    attn = _attention(q, kc, vc, kl, vl)
    pool, x0, z = _seqmix(pph, wts)
    hy = _hyena_long(z, x0, wts["hy_bias"], _hyena_filter(s, wts))
    x1, h2, logits = _outproj(x, pool, hy, attn, main_seg, wts, mod)
    h2 = h2.reshape(b * s, d)
    logits = logits.reshape(b * s, LANE)[:, :N_EXPERTS]
    n_c = 0
    if not last:
        attn_c = _attention(qc, kc, vc)
        pool_c, x0c, zc = _seqmix(pph_c, wts)
        hy_c = _hyena_short(zc, x0c, wts["hy_bias"], _hyena_filter(n_ctx, wts))
        xc1, h2c, logits_c = _outproj(xc, pool_c, hy_c, attn_c, ctx_seg, wts, mod)
        n_c = b * n_ctx
        h2 = jnp.concatenate([h2c.reshape(n_c, d), h2], axis=0)
        logits = jnp.concatenate([logits_c.reshape(n_c, LANE)[:, :N_EXPERTS], logits], axis=0)

    gates, buf_tok, slot_pos, block_e, n_used = _route(logits)
    ys = _experts(h2[buf_tok], block_e, n_used, wts["moe_w1"], wts["moe_b1"], wts["moe_w2"], wts["moe_b2"])
    ysel = ys[slot_pos.T]
    x2 = _combine(x1, ysel, gates, n_c, main_seg, mod)
    if not last:
        xc = _combine(xc1, ysel, gates, 0, ctx_seg, mod)
    return x2, xc


def kernel(x, c, ctx, c_ctx, norm1_g, norm2_g, w_mod, b_mod, w_in, pool_w, pool_scale, hy_conv_w, hy_conv_b,
           hy_f_w1, hy_f_b1, hy_f_w2, hy_f_b2, hy_f_w3, hy_freq, hy_bias, mla_q_norm_g, mla_w_uq, mla_kv_norm_g,
           mla_w_ukv, qk_norm_q, qk_norm_k, w_out, router_w, router_b, moe_w1, moe_b1, moe_w2, moe_b2):
    p = dict(norm1_g=norm1_g, norm2_g=norm2_g, w_in=w_in, pool_w=pool_w, pool_scale=pool_scale,
             hy_conv_w=hy_conv_w, hy_conv_b=hy_conv_b, hy_f_w1=hy_f_w1, hy_f_b1=hy_f_b1, hy_f_w2=hy_f_w2,
             hy_f_b2=hy_f_b2, hy_f_w3=hy_f_w3, hy_freq=hy_freq, hy_bias=hy_bias, mla_q_norm_g=mla_q_norm_g,
             mla_w_uq=mla_w_uq, mla_kv_norm_g=mla_kv_norm_g, mla_w_ukv=mla_w_ukv, qk_norm_q=qk_norm_q,
             qk_norm_k=qk_norm_k, w_out=w_out, router_w=router_w, router_b=router_b, moe_w1=moe_w1,
             moe_b1=moe_b1, moe_w2=moe_w2, moe_b2=moe_b2)
    b, s, d = x.shape
    depth = w_mod.shape[0]
    cc = jnp.zeros((N_SEG, d), F32).at[:b].set(c).at[CTX_SEG].set(c_ctx)
    rope_main = _rope_tables(s)
    rope_ctx = _identity_rope(ctx.shape[1])
    xc = ctx
    for l in range(depth):
        wts = _layer_weights(l, p)
        x, xc = _layer(x, xc, cc, wts, w_mod[l], b_mod[l][None, :], rope_main, rope_ctx, l == depth - 1)
    return x
```

```python
import functools
import math

import jax
import jax.numpy as jnp
import numpy as np
from jax import lax
from jax.experimental import pallas as pl
from jax.experimental.pallas import tpu as pltpu

F32 = jnp.float32
BF16 = jnp.bfloat16
HIGHEST = lax.Precision.HIGHEST

D_MODEL = 1024
GRID_W = 64
EPS = 1e-6
POOL_WIDTH = 256
POOL_GROUPS = 4
POOL_GROUP_DIM = POOL_WIDTH // POOL_GROUPS
HY_WIDTH = 256
HY_IN = 3 * HY_WIDTH
HY_BANDS = 8
HY_EMB = 1 + 2 * HY_BANDS
HY_HID = 64
HY_MIN_DECAY = math.log(1e-2) / 1.5
HY_MAX_DECAY = math.log(1e-2) / 0.3
HEADS = 8
Q_RANK = 256
KV_RANK = 128
NOPE = 64
ROPE = 32
V_DIM = 64
QK_DIM = NOPE + ROPE
MLA_WIDTH = HEADS * V_DIM
ATTN_SCALE = QK_DIM ** -0.5
LOG2E = math.log2(math.e)
OFF_HY = POOL_WIDTH
OFF_Q = OFF_HY + HY_IN
OFF_KV = OFF_Q + Q_RANK
OFF_KR = OFF_KV + KV_RANK
N_IN = OFF_KR + ROPE
N_IN_PAD = OFF_KR + 128
ROPE_BASE = 10000.0
N_EXPERTS = 32
TOP_K = 4
D_EXPERT = 1024
SWIGLU_LIMIT = 7.0
SWIGLU_ALPHA = 1.702

LANE = 128
HEAD_PAD = LANE
ROW_TILE = 512
EXPERT_TILE = 512
VMEM_LIMIT = 56 * 1024 * 1024
N_SEG = 8
CTX_SEG = 4


def _cparams(*sem):
    return pltpu.CompilerParams(dimension_semantics=sem, vmem_limit_bytes=VMEM_LIMIT)


def _rms(x, g, n):
    ss = jnp.sum(x * x, axis=-1, keepdims=True) * (1.0 / n)
    return x * lax.rsqrt(ss + EPS) * g


def _mod_kernel(c_ref, w_ref, b_ref, o_ref):
    c = c_ref[...]
    s = c * (1.0 / (1.0 + jnp.exp(-c)))
    o_ref[...] = jnp.dot(s, w_ref[0], precision=HIGHEST, preferred_element_type=F32) + b_ref[...]


def _modulation(cc, w, b, layer):
    _, d, n = w.shape
    tn = 1536
    return pl.pallas_call(
        _mod_kernel,
        grid=(n // tn,),
        in_specs=[pl.BlockSpec((N_SEG, d), lambda j: (0, 0)),
                  pl.BlockSpec((1, d, tn), lambda j: (layer, 0, j)),
                  pl.BlockSpec((1, tn), lambda j: (0, j))],
        out_specs=pl.BlockSpec((N_SEG, tn), lambda j: (0, j)),
        out_shape=jax.ShapeDtypeStruct((N_SEG, n), F32),
        compiler_params=_cparams("arbitrary"),
        name="modulation",
    )(cc, w, b)


def _inproj_kernel(x_ref, gn_ref, sc_ref, sh_ref, win_ref, gq_ref, wq_ref, gkv_ref, wkv_ref,
                   gqq_ref, gqk_ref, vone_ref, rc_ref, ra_ref, rb_ref,
                   pph_ref, q_ref, k_ref, v_ref):
    x = x_ref[0]
    h = _rms(x, gn_ref[...], D_MODEL) * (1.0 + sc_ref[0]) + sh_ref[0]
    p = jnp.dot(h.astype(BF16), win_ref[...], preferred_element_type=F32)
    pph_ref[0] = p[:, :OFF_Q]
    cq = _rms(p[:, OFF_Q:OFF_KV], gq_ref[...], Q_RANK)
    qall = jnp.dot(cq.astype(BF16), wq_ref[...], preferred_element_type=F32)
    ckv = _rms(p[:, OFF_KV:OFF_KR], gkv_ref[...], KV_RANK)
    kin = jnp.concatenate([ckv, p[:, OFF_KR:N_IN_PAD]], axis=-1).astype(BF16)
    kvall = jnp.dot(kin, wkv_ref[...], preferred_element_type=F32)
    rc, ra, rb = rc_ref[...], ra_ref[...], rb_ref[...]

    def rope(xh):
        return xh * rc + pltpu.roll(xh, HEAD_PAD - ROPE // 2, axis=1) * ra + pltpu.roll(xh, ROPE // 2, axis=1) * rb

    for hd in range(HEADS):
        sl = slice(HEAD_PAD * hd, HEAD_PAD * (hd + 1))
        qh = rope(_rms(qall[:, sl], gqq_ref[...], QK_DIM)) * (ATTN_SCALE * LOG2E)
        q_ref[0, hd] = qh.astype(BF16)
        kh = rope(_rms(kvall[:, sl], gqk_ref[...], QK_DIM))
        k_ref[0, hd] = kh.astype(BF16)
        vs = slice(HEADS * HEAD_PAD + HEAD_PAD * hd, HEADS * HEAD_PAD + HEAD_PAD * (hd + 1))
        v_ref[0, hd] = (kvall[:, vs] + vone_ref[...]).astype(BF16)


def _inproj(x, seg_of_batch, wts, mod, rope_tabs):
    b, s, d = x.shape
    t = min(ROW_TILE, s)
    full = lambda shape: pl.BlockSpec(shape, lambda bi, i: (0,) * len(shape))
    seg = lambda: pl.BlockSpec((1, 1, d), lambda bi, i: (seg_of_batch(bi), 0, 0))
    tab = lambda: pl.BlockSpec((t, HEAD_PAD), lambda bi, i: (i, 0))
    hw = HEADS * HEAD_PAD
    head_out = lambda: pl.BlockSpec((1, HEADS, t, HEAD_PAD), lambda bi, i: (bi, 0, i, 0))
    return pl.pallas_call(
        _inproj_kernel,
        grid=(b, s // t),
        in_specs=[pl.BlockSpec((1, t, d), lambda bi, i: (bi, i, 0)),
                  full((1, d)), seg(), seg(),
                  full((d, N_IN_PAD)), full((1, Q_RANK)), full((Q_RANK, hw)),
                  full((1, KV_RANK)), full((2 * KV_RANK, 2 * hw)),
                  full((1, HEAD_PAD)), full((1, HEAD_PAD)), full((1, HEAD_PAD)),
                  tab(), tab(), tab()],
        out_specs=[pl.BlockSpec((1, t, OFF_Q), lambda bi, i: (bi, i, 0)),
                   head_out(), head_out(), head_out()],
        out_shape=[jax.ShapeDtypeStruct((b, s, OFF_Q), F32)]
        + [jax.ShapeDtypeStruct((b, HEADS, s, HEAD_PAD), BF16)] * 3,
        compiler_params=_cparams("parallel", "parallel"),
        name="inproj",
    )(x, wts["norm1_g"], mod["sc1"], mod["sh1"], wts["w_in"], wts["gq"], wts["wq"], wts["gkv"], wts["wkv"],
      wts["gqq"], wts["gqk"], wts["vone"], *rope_tabs)


def _seqmix_kernel(prev_ref, cur_ref, next_ref, wpool_ref, pscale_ref, cw_ref, cb_ref,
                   pool_ref, x0_ref, z_ref, *, seq_len):
    i = pl.program_id(1)
    n = pl.num_programs(1)
    t_rows = cur_ref.shape[1]
    halo = prev_ref.shape[1]
    r_rows = t_rows + 2 * halo
    prev = jnp.where(i > 0, prev_ref[0], 0.0)
    nxt = jnp.where(i < n - 1, next_ref[0], 0.0)
    ext = jnp.concatenate([prev, cur_ref[0], nxt], axis=0)

    def sh(a, dlt):
        return pltpu.roll(a, (-dlt) % r_rows, axis=0)

    u = ext[:, :POOL_WIDTH]
    a2 = u + sh(u, -1)
    a4 = sh(a2, -1) + sh(a2, 1)
    a8 = sh(a4, -2) + sh(a4, 2)
    a16 = sh(a8, -4) + sh(a8, 4)
    grp = lax.broadcasted_iota(jnp.int32, (1, POOL_WIDTH), 1) // POOL_GROUP_DIM
    win = jnp.where(grp == 0, a2, jnp.where(grp == 1, a4, jnp.where(grp == 2, a8, a16)))
    half = jnp.where(grp == 0, 1, jnp.where(grp == 1, 2, jnp.where(grp == 2, 4, 8)))
    pos = i * t_rows - halo + lax.broadcasted_iota(jnp.int32, (r_rows, 1), 0)
    cnt = jnp.minimum(pos + half, seq_len) - jnp.maximum(pos - half, 0)
    cnt = jnp.maximum(cnt, 1).astype(F32)
    dev = (win / cnt - u)[halo:halo + t_rows]
    pool = jnp.dot(dev.astype(BF16), wpool_ref[...], preferred_element_type=F32) * pscale_ref[...]
    pool_ref[0] = pool.astype(BF16)

    e = ext[:, OFF_HY:OFF_Q]
    cw = cw_ref[...]
    uc = sh(e, -1) * cw[0:1] + e * cw[1:2] + sh(e, 1) * cw[2:3] + cb_ref[...]
    uc = uc[halo:halo + t_rows]
    x0_ref[0] = uc[:, :HY_WIDTH]
    z_ref[0] = uc[:, HY_WIDTH:2 * HY_WIDTH] * uc[:, 2 * HY_WIDTH:]


def _seqmix(pph, wts):
    b, s, w = pph.shape
    t = min(ROW_TILE, s)
    halo = 8
    per = t // halo
    nh = s // halo
    full = lambda shape: pl.BlockSpec(shape, lambda bi, i: (0,) * len(shape))
    out = lambda: pl.BlockSpec((1, t, HY_WIDTH), lambda bi, i: (bi, i, 0))
    return pl.pallas_call(
        functools.partial(_seqmix_kernel, seq_len=s),
        grid=(b, s // t),
        in_specs=[pl.BlockSpec((1, halo, w), lambda bi, i: (bi, jnp.maximum(i * per - 1, 0), 0)),
                  pl.BlockSpec((1, t, w), lambda bi, i: (bi, i, 0)),
                  pl.BlockSpec((1, halo, w), lambda bi, i: (bi, jnp.minimum((i + 1) * per, nh - 1), 0)),
                  full((POOL_WIDTH, POOL_WIDTH)), full((1, POOL_WIDTH)), full((3, HY_IN)), full((1, HY_IN))],
        out_specs=[out(), out(), out()],
        out_shape=[jax.ShapeDtypeStruct((b, s, POOL_WIDTH), BF16),
                   jax.ShapeDtypeStruct((b, s, HY_WIDTH), F32),
                   jax.ShapeDtypeStruct((b, s, HY_WIDTH), F32)],
        compiler_params=_cparams("parallel", "parallel"),
        name="seqmix",
    )(pph, pph, pph, wts["wpool"], wts["pool_scale"], wts["conv_w"], wts["conv_b"])


def _leftmm_kernel(*refs, cmul, gate):
    m_ref, x_ref = refs[0], refs[1]
    nxt = 2
    x = x_ref[0]
    if cmul:
        ks = refs[nxt][...]
        nxt += 1
        kh = x.shape[0] // 2
        xr, xi, kr, ki = x[:kh], x[kh:], ks[:kh], ks[kh:]
        x = jnp.concatenate([xr * kr - xi * ki, xr * ki + xi * kr], axis=0)
    y = jnp.dot(m_ref[...], x, precision=HIGHEST, preferred_element_type=F32)
    if gate:
        z_ref, bias_ref, x0_ref = refs[nxt:nxt + 3]
        nxt += 3
        y = (y + z_ref[0] * bias_ref[...]) * x0_ref[0]
    o_ref = refs[nxt]
    o_ref[0] = y.astype(o_ref.dtype)


def _leftmm(m, x, spectrum=None, gate=None, out_dtype=F32):
    g, k, cols = x.shape
    mo = m.shape[0]
    tc = min(cols, 2048)
    args = [m, x]
    in_specs = [pl.BlockSpec((mo, k), lambda gi, j: (0, 0)), pl.BlockSpec((1, k, tc), lambda gi, j: (gi, 0, j))]
    if spectrum is not None:
        args.append(spectrum)
        in_specs.append(pl.BlockSpec((k, tc), lambda gi, j: (0, j)))
    if gate is not None:
        z, bias, x0 = gate
        args += [z, bias, x0]
        in_specs += [pl.BlockSpec((1, mo, tc), lambda gi, j: (gi, 0, j)),
                     pl.BlockSpec((1, tc), lambda gi, j: (0, j)),
                     pl.BlockSpec((1, mo, tc), lambda gi, j: (gi, 0, j))]
    return pl.pallas_call(
        functools.partial(_leftmm_kernel, cmul=spectrum is not None, gate=gate is not None),
        grid=(g, cols // tc),
        in_specs=in_specs,
        out_specs=pl.BlockSpec((1, mo, tc), lambda gi, j: (gi, 0, j)),
        out_shape=jax.ShapeDtypeStruct((g, mo, cols), out_dtype),
        compiler_params=_cparams("parallel", "parallel"),
        name="dft_leftmm",
    )(*args)


def _blockfft_kernel(*refs, n2, blocks, conv):
    if conv:
        a_ref, tw_ref, ks_ref, m2_ref, m2c_ref, o_ref = refs
    else:
        a_ref, tw_ref, m2_ref, o_ref = refs

    def body(j, carry):
        rows = pl.ds(pl.multiple_of(j * n2, n2), n2)
        ar, ai = a_ref[0, 0, rows, :], a_ref[0, 1, rows, :]
        tcs, tsn = tw_ref[0, rows, :], tw_ref[1, rows, :]
        tcs = jnp.concatenate([tcs] * (ar.shape[1] // tcs.shape[1]), axis=1)
        tsn = jnp.concatenate([tsn] * (ar.shape[1] // tsn.shape[1]), axis=1)
        bstk = jnp.concatenate([ar * tcs + ai * tsn, ai * tcs - ar * tsn], axis=0)
        xs = jnp.dot(m2_ref[...], bstk, precision=HIGHEST, preferred_element_type=F32)
        if not conv:
            o_ref[0, 0, rows, :] = xs[:n2]
            o_ref[0, 1, rows, :] = xs[n2:]
            return carry
        xr, xi = xs[:n2], xs[n2:]
        kr, ki = ks_ref[0, rows, :], ks_ref[1, rows, :]
        ystk = jnp.concatenate([xr * kr - xi * ki, xr * ki + xi * kr], axis=0)
        cs = jnp.dot(m2c_ref[...], ystk, precision=HIGHEST, preferred_element_type=F32)
        cr, ci = cs[:n2], cs[n2:]
        o_ref[0, 0, rows, :] = cr * tcs - ci * tsn
        o_ref[0, 1, rows, :] = cr * tsn + ci * tcs
        return carry

    lax.fori_loop(0, blocks, body, 0)


def _blockfft(a, tw, m2, spectrum=None, m2c=None):
    g, _, n, c = a.shape
    n2 = m2.shape[0] // 2
    blocks = min(8, n // n2)
    rows = blocks * n2
    conv = spectrum is not None
    args = [a, tw]
    in_specs = [pl.BlockSpec((1, 2, rows, c), lambda gi, j: (gi, 0, j, 0)),
                pl.BlockSpec((2, rows, LANE), lambda gi, j: (0, j, 0))]
    if conv:
        args.append(spectrum)
        in_specs.append(pl.BlockSpec((2, rows, c), lambda gi, j: (0, j, 0)))
    args.append(m2)
    in_specs.append(pl.BlockSpec(m2.shape, lambda gi, j: (0, 0)))
    if conv:
        args.append(m2c)
        in_specs.append(pl.BlockSpec(m2c.shape, lambda gi, j: (0, 0)))
    return pl.pallas_call(
        functools.partial(_blockfft_kernel, n2=n2, blocks=blocks, conv=conv),
        grid=(g, n // rows),
        in_specs=in_specs,
        out_specs=pl.BlockSpec((1, 2, rows, c), lambda gi, j: (gi, 0, j, 0)),
        out_shape=jax.ShapeDtypeStruct((g, 2, n, c), F32),
        compiler_params=_cparams("parallel", "parallel"),
        name="dft_block",
    )(*args)


def _filter_kernel(z_ref, w1_ref, b1_ref, w2_ref, b2_ref, w3_ref, fr_ref, dl_ref, o_ref, *, seq_len):
    i = pl.program_id(0)
    rows = z_ref.shape[0]
    z = z_ref[...]
    fr = fr_ref[...]
    h = jnp.sin(fr * (jnp.dot(z, w1_ref[...], precision=HIGHEST, preferred_element_type=F32) + b1_ref[...]))
    h = jnp.sin(fr * (jnp.dot(h, w2_ref[...], precision=HIGHEST, preferred_element_type=F32) + b2_ref[...]))
    hf = jnp.dot(h, w3_ref[...], precision=HIGHEST, preferred_element_type=F32)
    decay = jnp.exp(-z[:, 0:1] * dl_ref[...])
    j = i * rows + lax.broadcasted_iota(jnp.int32, (rows, 1), 0)
    taps = jnp.where(j < seq_len, hf[:, :HY_WIDTH], jnp.where(j > seq_len, hf[:, HY_WIDTH:], 0.0))
    o_ref[...] = taps * decay


def _hyena_filter(seq_len, wts):
    n = 2 * seq_len
    t = jnp.linspace(0.0, 1.0, seq_len, dtype=F32)[:, None]
    wpos = (2.0 * math.pi / seq_len) * jnp.arange(seq_len, dtype=F32)[:, None]
    f = jnp.linspace(1e-4, HY_BANDS - 1, HY_BANDS, dtype=F32)[None, :]
    feat = jnp.concatenate([t, jnp.cos(f * wpos), -jnp.sin(f * wpos)], axis=-1)
    j = jnp.arange(n)
    lag = jnp.clip(jnp.where(j < seq_len, j, n - j), 0, seq_len - 1)
    feat2 = jnp.pad(feat[lag], ((0, 0), (0, LANE - HY_EMB)))
    rows = min(n, 2048)
    full = lambda shape: pl.BlockSpec(shape, lambda i: (0,) * len(shape))
    return pl.pallas_call(
        functools.partial(_filter_kernel, seq_len=seq_len),
        grid=(n // rows,),
        in_specs=[pl.BlockSpec((rows, LANE), lambda i: (i, 0)),
                  full((LANE, LANE)), full((1, LANE)), full((LANE, LANE)), full((1, LANE)),
                  full((LANE, 2 * HY_WIDTH)), full((1, LANE)), full((1, HY_WIDTH))],
        out_specs=pl.BlockSpec((rows, HY_WIDTH), lambda i: (i, 0)),
        out_shape=jax.ShapeDtypeStruct((n, HY_WIDTH), F32),
        compiler_params=_cparams("parallel"),
        name="hyena_filter",
    )(feat2, wts["f_w1"], wts["f_b1"], wts["f_w2"], wts["f_b2"], wts["f_w3"], wts["f_freq"], wts["f_delta"])


def _cs(p, rows, cols):
    ang = 2.0 * np.pi * np.outer(np.arange(rows), np.arange(cols)) / p
    return np.cos(ang), np.sin(ang)


def _mat(blocks):
    return jnp.asarray(np.block(blocks), dtype=F32)


def _hyena_long(z, x0, bias, filt):
    b, s, c = z.shape
    n = 2 * s
    n1 = 1 << ((n.bit_length() - 1 + 1) // 2)
    n2 = n // n1
    g = b // 2
    cols = n2 * c
    c1, s1 = _cs(n1, n1, n1)
    h1 = n1 // 2
    m_fwd = _mat([[c1[:, :h1], s1[:, :h1]], [-s1[:, :h1], c1[:, :h1]]])
    m_filt = _mat([[c1], [-s1]])
    m_inv = _mat([[c1[:h1], -s1[:h1]], [s1[:h1], c1[:h1]]]) * (1.0 / n)
    c2, s2 = _cs(n2, n2, n2)
    m2 = _mat([[c2, s2], [-s2, c2]])
    m2c = _mat([[c2, -s2], [s2, c2]])
    k1 = jnp.arange(n1, dtype=jnp.int32)[:, None]
    nn2 = jnp.arange(n2, dtype=jnp.int32)[None, :]
    ang = (2.0 * math.pi / n) * (k1 * nn2).astype(F32).reshape(n)
    tw = jnp.broadcast_to(jnp.stack([jnp.cos(ang), jnp.sin(ang)])[:, :, None], (2, n, LANE))

    spec = _leftmm(m_filt, filt.reshape(1, n1, cols)).reshape(1, 2, n, c)
    spec = _blockfft(spec, tw, m2)[0]
    zv = z.reshape(g, n1, cols)
    a = _leftmm(m_fwd, zv).reshape(g, 2, n, c)
    dd = _blockfft(a, tw, m2, spectrum=spec, m2c=m2c).reshape(g, 2 * n1, cols)
    bias_t = jnp.tile(bias, (1, n2))
    y = _leftmm(m_inv, dd, gate=(zv, bias_t, x0.reshape(g, n1, cols)), out_dtype=BF16)
    return y.reshape(b, s, c)


def _hyena_short(z, x0, bias, filt):
    b, s, c = z.shape
    n = 2 * s
    g = b // 2
    cf, sf = _cs(n, n, n)
    m_fwd = _mat([[cf[:, :s], sf[:, :s]], [-sf[:, :s], cf[:, :s]]])
    m_filt = _mat([[cf], [-sf]])
    m_inv = _mat([[cf[:s], -sf[:s]], [sf[:s], cf[:s]]]) * (1.0 / n)
    spec = _leftmm(m_filt, filt.reshape(1, n, c))[0]
    zv = z.reshape(g, n, c)
    zf = _leftmm(m_fwd, zv)
    y = _leftmm(m_inv, zf, spectrum=spec, gate=(zv, bias, x0.reshape(g, n, c)), out_dtype=BF16)
    return y.reshape(b, s, c)


def _attn_kernel(*refs, tk, n_chunks, heads, unroll):
    if n_chunks:
        q_ref, kc_ref, vc_ref, kl_ref, vl_ref, o_ref, s_ref = refs
    else:
        q_ref, kc_ref, vc_ref, o_ref = refs
    dn = (((1,), (1,)), ((), ()))
    qs = [q_ref[0, hh] for hh in range(heads)]

    def chunk_rows(j):
        return pl.ds(pl.multiple_of(j * tk, tk), tk)

    def scores(j, slot):
        for hh in range(heads):
            s_ref[slot, hh] = lax.dot_general(qs[hh], kl_ref[0, hh, chunk_rows(j), :], dn,
                                              preferred_element_type=F32)

    def absorb(j, slot, carry):
        out = []
        for hh in range(heads):
            m_old, acc_old = carry[hh]
            sj = s_ref[slot, hh]
            m_new = jnp.maximum(m_old, jnp.max(sj, axis=-1, keepdims=True))
            pj = jnp.exp2(sj - m_new)
            acc_new = jnp.exp2(m_old - m_new) * acc_old + jnp.dot(pj.astype(BF16), vl_ref[0, hh, chunk_rows(j), :],
                                                                  preferred_element_type=F32)
            out.append((m_new, acc_new))
        return tuple(out)

    if n_chunks:
        scores(0, 0)
    state = []
    for hh in range(heads):
        s = lax.dot_general(qs[hh], kc_ref[0, hh], dn, preferred_element_type=F32)
        m = jnp.max(s, axis=-1, keepdims=True)
        p = jnp.exp2(s - m)
        state.append((m, jnp.dot(p.astype(BF16), vc_ref[0, hh], preferred_element_type=F32)))

    def body(jj, carry):
        j = unroll * jj
        for u in range(unroll):
            nxt = j + u + 1
            if u == unroll - 1:
                nxt = jnp.minimum(nxt, n_chunks - 1)
            scores(nxt, (u + 1) % 2)
            carry = absorb(j + u, u % 2, carry)
        return carry

    if n_chunks:
        state = lax.fori_loop(0, n_chunks // unroll, body, tuple(state))
    outs = [acc[:, :V_DIM] / acc[:, V_DIM:V_DIM + 1] for _, acc in state]
    o_ref[0] = jnp.concatenate(outs, axis=-1).astype(o_ref.dtype)


def _attention(q, kc, vc, kl=None, vl=None):
    b, h, s, dp = q.shape
    nc = kc.shape[2]
    tq = min(ROW_TILE, s)
    heads = LANE // V_DIM
    tk = 512
    n_chunks = 0 if kl is None else kl.shape[2] // tk
    unroll = 8 if n_chunks % 8 == 0 else 2
    args = [q, kc, vc]
    in_specs = [pl.BlockSpec((1, heads, tq, dp), lambda bi, hi, i: (bi, hi, i, 0)),
                pl.BlockSpec((1, heads, nc, dp), lambda bi, hi, i: (bi, hi, 0, 0)),
                pl.BlockSpec((1, heads, nc, dp), lambda bi, hi, i: (bi, hi, 0, 0))]
    scratch = []
    if n_chunks:
        assert unroll % 2 == 0 and n_chunks % unroll == 0
        sl = kl.shape[2]
        args += [kl, vl]
        in_specs += [pl.BlockSpec((1, heads, sl, dp), lambda bi, hi, i: (bi, hi, 0, 0))] * 2
        scratch = [pltpu.VMEM((2, heads, tq, tk), F32)]
    return pl.pallas_call(
        functools.partial(_attn_kernel, tk=tk, n_chunks=n_chunks, heads=heads, unroll=unroll),
        grid=(b, h // heads, s // tq),
        in_specs=in_specs,
        out_specs=pl.BlockSpec((1, tq, heads * V_DIM), lambda bi, hi, i: (bi, i, hi)),
        out_shape=jax.ShapeDtypeStruct((b, s, h * V_DIM), BF16),
        scratch_shapes=scratch,
        compiler_params=_cparams("parallel", "parallel", "parallel"),
        name="attention",
    )(*args)


def _outproj_kernel(x_ref, pool_ref, hy_ref, at_ref, wout_ref, g1_ref, gn_ref, sc_ref, sh_ref, rw_ref, rb_ref,
                    x1_ref, h2_ref, lg_ref):
    mix = jnp.concatenate([pool_ref[0], hy_ref[0], at_ref[0]], axis=-1)
    y = jnp.dot(mix, wout_ref[...], preferred_element_type=F32)
    x1 = x_ref[0] + g1_ref[0] * y
    x1_ref[0] = x1
    h2 = _rms(x1, gn_ref[...], D_MODEL) * (1.0 + sc_ref[0]) + sh_ref[0]
    h2_ref[0] = h2.astype(BF16)
    lg_ref[0] = lax.dot_general(rw_ref[...], h2, (((1,), (1,)), ((), ())), precision=HIGHEST,
                                preferred_element_type=F32) + rb_ref[...]


def _outproj(x, pool, hy, attn, seg_of_batch, wts, mod):
    b, s, d = x.shape
    t = min(ROW_TILE, s)
    full = lambda shape: pl.BlockSpec(shape, lambda bi, i: (0,) * len(shape))
    seg = lambda: pl.BlockSpec((1, 1, d), lambda bi, i: (seg_of_batch(bi), 0, 0))
    rowblk = lambda w: pl.BlockSpec((1, t, w), lambda bi, i: (bi, i, 0))
    return pl.pallas_call(
        _outproj_kernel,
        grid=(b, s // t),
        in_specs=[rowblk(d), rowblk(POOL_WIDTH), rowblk(HY_WIDTH), rowblk(MLA_WIDTH),
                  full((d, d)), seg(), full((1, d)), seg(), seg(), full((N_EXPERTS, d)), full((N_EXPERTS, 1))],
        out_specs=[rowblk(d), rowblk(d), pl.BlockSpec((1, N_EXPERTS, t), lambda bi, i: (bi, 0, i))],
        out_shape=[jax.ShapeDtypeStruct((b, s, d), F32), jax.ShapeDtypeStruct((b, s, d), BF16),
                   jax.ShapeDtypeStruct((b, N_EXPERTS, s), F32)],
        compiler_params=_cparams("parallel", "parallel"),
        name="outproj",
    )(x, pool, hy, attn, wts["w_out"], mod["g1"], wts["norm2_g"], mod["sc2"], mod["sh2"],
      wts["router_w"], wts["router_b"])


def _moe_kernel(be_ref, nu_ref, x_ref, w1_ref, b1_ref, w2_ref, b2_ref, o_ref, w1b_ref, w2b_ref):
    i = pl.program_id(0)
    e = be_ref[i]
    e_prev = be_ref[jnp.maximum(i - 1, 0)]

    @pl.when((i == 0) | (e != e_prev))
    def _():
        w1b_ref[...] = w1_ref[0, 0].astype(BF16)
        w2b_ref[...] = w2_ref[0, 0].astype(BF16)

    @pl.when(i < nu_ref[0])
    def _():
        a = jnp.dot(x_ref[...], w1b_ref[...], preferred_element_type=F32) + b1_ref[0, 0]
        glu = jnp.minimum(a[:, :D_EXPERT], SWIGLU_LIMIT)
        lin = jnp.clip(a[:, D_EXPERT:], -SWIGLU_LIMIT, SWIGLU_LIMIT)
        act = glu * (1.0 / (1.0 + jnp.exp(-SWIGLU_ALPHA * glu))) * (lin + 1.0)
        y = jnp.dot(act.astype(BF16), w2b_ref[...], preferred_element_type=F32) + b2_ref[0, 0]
        o_ref[...] = y.astype(o_ref.dtype)

    @pl.when(i >= nu_ref[0])
    def _():
        o_ref[...] = jnp.zeros_like(o_ref)


def _experts(xs, block_e, n_used, w1, b1, w2, b2, layer):
    n_buf, d = xs.shape
    tm = EXPERT_TILE
    n_blocks = n_buf // tm
    de2 = w1.shape[3]
    grid_spec = pltpu.PrefetchScalarGridSpec(
        num_scalar_prefetch=2,
        grid=(n_blocks,),
        in_specs=[pl.BlockSpec((tm, d), lambda i, be, nu: (i, 0)),
                  pl.BlockSpec((1, 1, d, de2), lambda i, be, nu: (layer, be[i], 0, 0)),
                  pl.BlockSpec((1, 1, 1, de2), lambda i, be, nu: (layer, be[i], 0, 0)),
                  pl.BlockSpec((1, 1, D_EXPERT, d), lambda i, be, nu: (layer, be[i], 0, 0)),
                  pl.BlockSpec((1, 1, 1, d), lambda i, be, nu: (layer, be[i], 0, 0))],
        out_specs=pl.BlockSpec((tm, d), lambda i, be, nu: (i, 0)),
        scratch_shapes=[pltpu.VMEM((d, de2), BF16), pltpu.VMEM((D_EXPERT, d), BF16)],
    )
    return pl.pallas_call(
        _moe_kernel,
        grid_spec=grid_spec,
        out_shape=jax.ShapeDtypeStruct((n_buf, d), BF16),
        compiler_params=_cparams("arbitrary"),
        name="experts",
    )(block_e, n_used, xs, w1, b1, w2, b2)


def _router_kernel(lg_ref, idx_ref, gate_ref, rank_ref, cnt_ref, base_ref):
    i = pl.program_id(0)
    n_e, t = lg_ref.shape

    @pl.when(i == 0)
    def _():
        base_ref[...] = jnp.zeros_like(base_ref)

    work = lg_ref[...]
    eid = lax.broadcasted_iota(jnp.int32, (n_e, t), 0)
    vals, idxs, sels = [], [], []
    for _ in range(TOP_K):
        m = jnp.max(work, axis=0, keepdims=True)
        idx = jnp.min(jnp.where(work == m, eid, n_e), axis=0, keepdims=True)
        sel = eid == idx
        work = jnp.where(sel, -jnp.inf, work)
        vals.append(m)
        idxs.append(idx)
        sels.append(sel)
    cnt = jnp.where(sels[0], 1.0, 0.0)
    for sel in sels[1:]:
        cnt = cnt + jnp.where(sel, 1.0, 0.0)
    earlier = (lax.broadcasted_iota(jnp.int32, (t, t), 0) < lax.broadcasted_iota(jnp.int32, (t, t), 1))
    within = jnp.dot(cnt.astype(BF16), jnp.where(earlier, 1.0, 0.0).astype(BF16), preferred_element_type=F32)
    pref = base_ref[:, 0:1] + within
    ranks = [jnp.sum(jnp.where(sel, pref, 0.0), axis=0, keepdims=True) for sel in sels]
    new_base = base_ref[...] + jnp.sum(cnt, axis=1, keepdims=True)
    base_ref[...] = new_base
    cnt_ref[...] = new_base
    ex = [jnp.exp(v - vals[0]) for v in vals]
    den = ex[0]
    for e in ex[1:]:
        den = den + e
    idx_ref[...] = jnp.concatenate(idxs, axis=0)
    gate_ref[...] = jnp.concatenate([e / den for e in ex], axis=0)
    rank_ref[...] = jnp.concatenate(ranks, axis=0).astype(jnp.int32)


def _route(logits_t):
    n_e, n_tok = logits_t.shape
    tm = EXPERT_TILE
    t = ROW_TILE
    tokblk = lambda: pl.BlockSpec((TOP_K, t), lambda i: (0, i))
    idx_t, gates_t, rank_t, counts_f = pl.pallas_call(
        _router_kernel,
        grid=(n_tok // t,),
        in_specs=[pl.BlockSpec((n_e, t), lambda i: (0, i))],
        out_specs=[tokblk(), tokblk(), tokblk(), pl.BlockSpec((n_e, LANE), lambda i: (0, 0))],
        out_shape=[jax.ShapeDtypeStruct((TOP_K, n_tok), jnp.int32), jax.ShapeDtypeStruct((TOP_K, n_tok), F32),
                   jax.ShapeDtypeStruct((TOP_K, n_tok), jnp.int32), jax.ShapeDtypeStruct((n_e, LANE), F32)],
        scratch_shapes=[pltpu.VMEM((n_e, LANE), F32)],
        compiler_params=_cparams("arbitrary"),
        name="router",
    )(logits_t)
    n_slots = n_tok * TOP_K
    counts = counts_f[:, 0].astype(jnp.int32)
    padded = (counts + tm - 1) // tm * tm
    padded_end = jnp.cumsum(padded)
    padded_start = padded_end - padded
    start = jnp.cumsum(counts) - counts
    experts = jnp.arange(n_e, dtype=jnp.int32)
    dest_t = rank_t + jnp.sum(jnp.where(idx_t[None] == experts[:, None, None], padded_start[:, None, None], 0), axis=0)
    n_blocks = n_slots // tm + n_e
    block_start = jnp.arange(n_blocks, dtype=jnp.int32) * tm
    block_e = jnp.minimum(jnp.sum(padded_end[None, :] <= block_start[:, None], axis=1), n_e - 1).astype(jnp.int32)
    n_used = (padded_end[-1] // tm).astype(jnp.int32).reshape(1)
    order = jnp.argsort(idx_t.T.reshape(n_slots)).astype(jnp.int32)
    sel = (block_e[:, None] == experts[None, :])
    pick = lambda v: jnp.sum(jnp.where(sel, v[None, :], 0), axis=1)[:, None]
    row = block_start[:, None] + jnp.arange(tm, dtype=jnp.int32)[None, :]
    srt = row - pick(padded_start) + pick(start)
    valid = srt < pick(start) + pick(counts)
    buf_tok = jnp.where(valid, order[jnp.clip(srt, 0, n_slots - 1)] // TOP_K, 0).reshape(n_blocks * tm)
    return gates_t.T, buf_tok, dest_t, block_e, n_used


def _combine_kernel(x_ref, y_ref, gt_ref, g2_ref, o_ref):
    gt = gt_ref[...]
    acc = gt[:, 0:1] * y_ref[0].astype(F32)
    for j in range(1, TOP_K):
        acc = acc + gt[:, j:j + 1] * y_ref[j].astype(F32)
    o_ref[0] = x_ref[0] + g2_ref[0] * acc


def _combine(x, ysel, gates, tok_offset, seg_of_batch, mod):
    b, s, d = x.shape
    t = min(ROW_TILE, s)
    per = s // t
    off = tok_offset // t
    return pl.pallas_call(
        _combine_kernel,
        grid=(b, per),
        in_specs=[pl.BlockSpec((1, t, d), lambda bi, i: (bi, i, 0)),
                  pl.BlockSpec((TOP_K, t, d), lambda bi, i: (0, off + bi * per + i, 0)),
                  pl.BlockSpec((t, TOP_K), lambda bi, i: (off + bi * per + i, 0)),
                  pl.BlockSpec((1, 1, d), lambda bi, i: (seg_of_batch(bi), 0, 0))],
        out_specs=pl.BlockSpec((1, t, d), lambda bi, i: (bi, i, 0)),
        out_shape=jax.ShapeDtypeStruct((b, s, d), F32),
        compiler_params=_cparams("parallel", "parallel"),
        name="combine",
    )(x, ysel, gates, mod["g2"])


def _rope_tables(seq_len):
    n_rows = seq_len // GRID_W
    row = jnp.repeat(jnp.arange(n_rows, dtype=F32), GRID_W)
    col = jnp.tile(jnp.arange(GRID_W, dtype=F32), n_rows)
    pairs = ROPE // 4
    inv = ROPE_BASE ** (-jnp.arange(pairs, dtype=F32) / pairs)
    ang = jnp.concatenate([row[:, None] * inv, col[:, None] * inv], axis=-1)
    cos, sin = jnp.cos(ang), jnp.sin(ang)
    half = ROPE // 2
    one = jnp.ones((seq_len, NOPE), F32)
    zn = jnp.zeros((seq_len, NOPE), F32)
    zh = jnp.zeros((seq_len, half), F32)
    zp = jnp.zeros((seq_len, HEAD_PAD - QK_DIM), F32)
    return (jnp.concatenate([one, cos, cos, zp], axis=-1),
            jnp.concatenate([zn, -sin, zh, zp], axis=-1),
            jnp.concatenate([zn, zh, sin, zp], axis=-1))


def _identity_rope(seq_len):
    keep = jnp.concatenate([jnp.ones((seq_len, QK_DIM), F32), jnp.zeros((seq_len, HEAD_PAD - QK_DIM), F32)], axis=-1)
    zero = jnp.zeros((seq_len, HEAD_PAD), F32)
    return keep, zero, zero


def _pad_cols(w, n):
    return jnp.pad(w, ((0, 0), (0, n - w.shape[1])))


def _layer_weights(l, p):
    hw = HEADS * HEAD_PAD
    wq = jnp.pad(p["mla_w_uq"][l].reshape(Q_RANK, HEADS, QK_DIM), ((0, 0), (0, 0), (0, HEAD_PAD - QK_DIM)))
    wukv = p["mla_w_ukv"][l].reshape(KV_RANK, HEADS, NOPE + V_DIM)
    wk = jnp.pad(wukv[:, :, :NOPE], ((0, 0), (0, 0), (0, HEAD_PAD - NOPE))).reshape(KV_RANK, hw)
    wv = jnp.pad(wukv[:, :, NOPE:], ((0, 0), (0, 0), (0, HEAD_PAD - V_DIM))).reshape(KV_RANK, hw)
    place = jnp.pad(jnp.eye(ROPE, dtype=F32), ((0, KV_RANK - ROPE), (NOPE, HEAD_PAD - QK_DIM)))
    wk_rope = jnp.tile(place, (1, HEADS))
    wkv = jnp.concatenate([jnp.concatenate([wk, wv], axis=1),
                           jnp.concatenate([wk_rope, jnp.zeros((KV_RANK, hw), F32)], axis=1)], axis=0)
    wpool = jax.scipy.linalg.block_diag(*[p["pool_w"][l, g] for g in range(POOL_GROUPS)])
    vone = jnp.zeros((1, HEAD_PAD), F32).at[0, V_DIM].set(1.0)
    deltas = jnp.abs(jnp.linspace(HY_MIN_DECAY, HY_MAX_DECAY, HY_WIDTH, dtype=F32))[None, :]
    row = lambda v, n=None: (v if n is None else jnp.pad(v, (0, n - v.shape[0])))[None, :]
    return {
        "norm1_g": row(p["norm1_g"][l]), "norm2_g": row(p["norm2_g"][l]),
        "w_in": _pad_cols(p["w_in"][l], N_IN_PAD).astype(BF16),
        "gq": row(p["mla_q_norm_g"][l]), "wq": wq.reshape(Q_RANK, hw).astype(BF16),
        "gkv": row(p["mla_kv_norm_g"][l]), "wkv": wkv.astype(BF16),
        "gqq": row(p["qk_norm_q"][l], HEAD_PAD), "gqk": row(p["qk_norm_k"][l], HEAD_PAD), "vone": vone,
        "wpool": wpool.astype(BF16), "pool_scale": row(p["pool_scale"][l]),
        "conv_w": p["hy_conv_w"][l], "conv_b": row(p["hy_conv_b"][l]),
        "f_w1": jnp.pad(p["hy_f_w1"][l], ((0, LANE - HY_EMB), (0, LANE - HY_HID))),
        "f_b1": row(p["hy_f_b1"][l], LANE),
        "f_w2": jnp.pad(p["hy_f_w2"][l], ((0, LANE - HY_HID), (0, LANE - HY_HID))),
        "f_b2": row(p["hy_f_b2"][l], LANE),
        "f_w3": jnp.pad(p["hy_f_w3"][l], ((0, LANE - HY_HID), (0, 0))),
        "f_freq": row(p["hy_freq"][l], LANE), "f_delta": deltas,
        "hy_bias": row(p["hy_bias"][l]),
        "w_out": p["w_out"][l].astype(BF16),
        "router_w": p["router_w"][l].T, "router_b": p["router_b"][l][:, None],
    }


def _layer(l, x, xc, cc, wts, p, rope_main, rope_ctx, last):
    b, s, d = x.shape
    n_ctx = xc.shape[1]
    modm = _modulation(cc, p["w_mod"], p["b_mod"][l][None, :], l)
    names = ("sh1", "sc1", "g1", "sh2", "sc2", "g2")
    mod = {nm: modm[:, i * d:(i + 1) * d].reshape(N_SEG, 1, d) for i, nm in enumerate(names)}
    main_seg = lambda bi: bi
    ctx_seg = lambda bi: CTX_SEG

    pph, q, kl, vl = _inproj(x, main_seg, wts, mod, rope_main)
    pph_c, qc, kc, vc = _inproj(xc, ctx_seg, wts, mod, rope_ctx)
    attn = _attention(q, kc, vc, kl, vl)
    pool, x0, z = _seqmix(pph, wts)
    hy = _hyena_long(z, x0, wts["hy_bias"], _hyena_filter(s, wts))
    x1, h2, logits = _outproj(x, pool, hy, attn, main_seg, wts, mod)
    h2 = h2.reshape(b * s, d)
    logits = jnp.swapaxes(logits, 0, 1).reshape(N_EXPERTS, b * s)
    n_c = 0
    if not last:
        attn_c = _attention(qc, kc, vc)
        pool_c, x0c, zc = _seqmix(pph_c, wts)
        hy_c = _hyena_short(zc, x0c, wts["hy_bias"], _hyena_filter(n_ctx, wts))
        xc1, h2c, logits_c = _outproj(xc, pool_c, hy_c, attn_c, ctx_seg, wts, mod)
        n_c = b * n_ctx
        h2 = jnp.concatenate([h2c.reshape(n_c, d), h2], axis=0)
        logits = jnp.concatenate([jnp.swapaxes(logits_c, 0, 1).reshape(N_EXPERTS, n_c), logits], axis=1)

    gates, buf_tok, dest_t, block_e, n_used = _route(logits)
    de2 = p["moe_w1"].shape[-1]
    ys = _experts(h2[buf_tok], block_e, n_used, p["moe_w1"], p["moe_b1"].reshape(-1, N_EXPERTS, 1, de2),
                  p["moe_w2"], p["moe_b2"].reshape(-1, N_EXPERTS, 1, d), l)
    ysel = ys[dest_t]
    x2 = _combine(x1, ysel, gates, n_c, main_seg, mod)
    if not last:
        xc = _combine(xc1, ysel, gates, 0, ctx_seg, mod)
    return x2, xc


def kernel(x, c, ctx, c_ctx, norm1_g, norm2_g, w_mod, b_mod, w_in, pool_w, pool_scale, hy_conv_w, hy_conv_b,
           hy_f_w1, hy_f_b1, hy_f_w2, hy_f_b2, hy_f_w3, hy_freq, hy_bias, mla_q_norm_g, mla_w_uq, mla_kv_norm_g,
           mla_w_ukv, qk_norm_q, qk_norm_k, w_out, router_w, router_b, moe_w1, moe_b1, moe_w2, moe_b2):
    p = dict(norm1_g=norm1_g, norm2_g=norm2_g, w_mod=w_mod, b_mod=b_mod, w_in=w_in, pool_w=pool_w,
             pool_scale=pool_scale,
             hy_conv_w=hy_conv_w, hy_conv_b=hy_conv_b, hy_f_w1=hy_f_w1, hy_f_b1=hy_f_b1, hy_f_w2=hy_f_w2,
             hy_f_b2=hy_f_b2, hy_f_w3=hy_f_w3, hy_freq=hy_freq, hy_bias=hy_bias, mla_q_norm_g=mla_q_norm_g,
             mla_w_uq=mla_w_uq, mla_kv_norm_g=mla_kv_norm_g, mla_w_ukv=mla_w_ukv, qk_norm_q=qk_norm_q,
             qk_norm_k=qk_norm_k, w_out=w_out, router_w=router_w, router_b=router_b, moe_w1=moe_w1,
             moe_b1=moe_b1, moe_w2=moe_w2, moe_b2=moe_b2)
    b, s, d = x.shape
    depth = w_mod.shape[0]
    cc = jnp.zeros((N_SEG, d), F32).at[:b].set(c).at[CTX_SEG].set(c_ctx)
    rope_main = _rope_tables(s)
    rope_ctx = _identity_rope(ctx.shape[1])
    xc = ctx
    for l in range(depth):
        wts = _layer_weights(l, p)
        x, xc = _layer(l, x, xc, cc, wts, p, rope_main, rope_ctx, l == depth - 1)
    return x
```

```python
import functools
import math

import jax
import jax.numpy as jnp
import numpy as np
from jax import lax
from jax.experimental import pallas as pl
from jax.experimental.pallas import tpu as pltpu
from jax.experimental.pallas import tpu_sc as plsc

F32 = jnp.float32
BF16 = jnp.bfloat16
HIGHEST = lax.Precision.HIGHEST

D_MODEL = 1024
GRID_W = 64
EPS = 1e-6
POOL_WIDTH = 256
POOL_GROUPS = 4
POOL_GROUP_DIM = POOL_WIDTH // POOL_GROUPS
HY_WIDTH = 256
HY_IN = 3 * HY_WIDTH
HY_BANDS = 8
HY_EMB = 1 + 2 * HY_BANDS
HY_HID = 64
HY_MIN_DECAY = math.log(1e-2) / 1.5
HY_MAX_DECAY = math.log(1e-2) / 0.3
HEADS = 8
Q_RANK = 256
KV_RANK = 128
NOPE = 64
ROPE = 32
V_DIM = 64
QK_DIM = NOPE + ROPE
MLA_WIDTH = HEADS * V_DIM
ATTN_SCALE = QK_DIM ** -0.5
LOG2E = math.log2(math.e)
OFF_HY = POOL_WIDTH
OFF_Q = OFF_HY + HY_IN
OFF_KV = OFF_Q + Q_RANK
OFF_KR = OFF_KV + KV_RANK
N_IN = OFF_KR + ROPE
N_IN_PAD = OFF_KR + 128
ROPE_BASE = 10000.0
N_EXPERTS = 32
TOP_K = 4
D_EXPERT = 1024
SWIGLU_LIMIT = 7.0
SWIGLU_ALPHA = 1.702

LANE = 128
HEAD_PAD = LANE
ROW_TILE = 512
EXPERT_TILE = 512
VMEM_LIMIT = 56 * 1024 * 1024
N_SEG = 8
CTX_SEG = 4


def _cparams(*sem):
    return pltpu.CompilerParams(dimension_semantics=sem, vmem_limit_bytes=VMEM_LIMIT)


def _rms(x, g, n):
    ss = jnp.sum(x * x, axis=-1, keepdims=True) * (1.0 / n)
    return x * lax.rsqrt(ss + EPS) * g


def _pack_rows(x):
    half = x.shape[1] // 2
    bits = lax.bitcast_convert_type(x.astype(BF16).astype(F32), jnp.int32)
    return bits[:, :half] | lax.shift_right_logical(bits[:, half:], 16)


def _unpack_rows(w):
    hi = lax.bitcast_convert_type(w & jnp.int32(-65536), F32)
    lo = lax.bitcast_convert_type(lax.shift_left(w, 16), F32)
    return jnp.concatenate([hi, lo], axis=1).astype(BF16)


def _mod_kernel(c_ref, w_ref, b_ref, o_ref):
    c = c_ref[...]
    s = c * (1.0 / (1.0 + jnp.exp(-c)))
    o_ref[...] = jnp.dot(s, w_ref[0], precision=HIGHEST, preferred_element_type=F32) + b_ref[...]


def _modulation(cc, w, b, layer):
    _, d, n = w.shape
    tn = 1536
    return pl.pallas_call(
        _mod_kernel,
        grid=(n // tn,),
        in_specs=[pl.BlockSpec((N_SEG, d), lambda j: (0, 0)),
                  pl.BlockSpec((1, d, tn), lambda j: (layer, 0, j)),
                  pl.BlockSpec((1, tn), lambda j: (0, j))],
        out_specs=pl.BlockSpec((N_SEG, tn), lambda j: (0, j)),
        out_shape=jax.ShapeDtypeStruct((N_SEG, n), F32),
        compiler_params=_cparams("arbitrary"),
        name="modulation",
    )(cc, w, b)


def _inproj_kernel(x_ref, gn_ref, sc_ref, sh_ref, win_ref, gq_ref, wq_ref, gkv_ref, wkv_ref,
                   gqq_ref, gqk_ref, vone_ref, rc_ref, ra_ref, rb_ref,
                   pph_ref, q_ref, k_ref, v_ref):
    x = x_ref[0]
    h = _rms(x, gn_ref[...], D_MODEL) * (1.0 + sc_ref[0]) + sh_ref[0]
    p = jnp.dot(h.astype(BF16), win_ref[...], preferred_element_type=F32)
    pph_ref[0] = p[:, :OFF_Q]
    cq = _rms(p[:, OFF_Q:OFF_KV], gq_ref[...], Q_RANK)
    qall = jnp.dot(cq.astype(BF16), wq_ref[...], preferred_element_type=F32)
    ckv = _rms(p[:, OFF_KV:OFF_KR], gkv_ref[...], KV_RANK)
    kin = jnp.concatenate([ckv, p[:, OFF_KR:N_IN_PAD]], axis=-1).astype(BF16)
    kvall = jnp.dot(kin, wkv_ref[...], preferred_element_type=F32)
    rc, ra, rb = rc_ref[...], ra_ref[...], rb_ref[...]

    def rope(xh):
        return xh * rc + pltpu.roll(xh, HEAD_PAD - ROPE // 2, axis=1) * ra + pltpu.roll(xh, ROPE // 2, axis=1) * rb

    for hd in range(HEADS):
        sl = slice(HEAD_PAD * hd, HEAD_PAD * (hd + 1))
        qh = rope(_rms(qall[:, sl], gqq_ref[...], QK_DIM)) * (ATTN_SCALE * LOG2E)
        q_ref[0, hd] = qh.astype(BF16)
        kh = rope(_rms(kvall[:, sl], gqk_ref[...], QK_DIM))
        k_ref[0, hd] = kh.astype(BF16)
        vs = slice(HEADS * HEAD_PAD + HEAD_PAD * hd, HEADS * HEAD_PAD + HEAD_PAD * (hd + 1))
        v_ref[0, hd] = (kvall[:, vs] + vone_ref[...]).astype(BF16)


def _inproj(x, seg_of_batch, wts, mod, rope_tabs):
    b, s, d = x.shape
    t = min(ROW_TILE, s)
    full = lambda shape: pl.BlockSpec(shape, lambda bi, i: (0,) * len(shape))
    seg = lambda: pl.BlockSpec((1, 1, d), lambda bi, i: (seg_of_batch(bi), 0, 0))
    tab = lambda: pl.BlockSpec((t, HEAD_PAD), lambda bi, i: (i, 0))
    hw = HEADS * HEAD_PAD
    head_out = lambda: pl.BlockSpec((1, HEADS, t, HEAD_PAD), lambda bi, i: (bi, 0, i, 0))
    return pl.pallas_call(
        _inproj_kernel,
        grid=(b, s // t),
        in_specs=[pl.BlockSpec((1, t, d), lambda bi, i: (bi, i, 0)),
                  full((1, d)), seg(), seg(),
                  full((d, N_IN_PAD)), full((1, Q_RANK)), full((Q_RANK, hw)),
                  full((1, KV_RANK)), full((2 * KV_RANK, 2 * hw)),
                  full((1, HEAD_PAD)), full((1, HEAD_PAD)), full((1, HEAD_PAD)),
                  tab(), tab(), tab()],
        out_specs=[pl.BlockSpec((1, t, OFF_Q), lambda bi, i: (bi, i, 0)),
                   head_out(), head_out(), head_out()],
        out_shape=[jax.ShapeDtypeStruct((b, s, OFF_Q), F32)]
        + [jax.ShapeDtypeStruct((b, HEADS, s, HEAD_PAD), BF16)] * 3,
        compiler_params=_cparams("parallel", "parallel"),
        name="inproj",
    )(x, wts["norm1_g"], mod["sc1"], mod["sh1"], wts["w_in"], wts["gq"], wts["wq"], wts["gkv"], wts["wkv"],
      wts["gqq"], wts["gqk"], wts["vone"], *rope_tabs)


def _seqmix_kernel(prev_ref, cur_ref, next_ref, wpool_ref, pscale_ref, cw_ref, cb_ref,
                   pool_ref, x0_ref, z_ref, *, seq_len):
    i = pl.program_id(1)
    n = pl.num_programs(1)
    t_rows = cur_ref.shape[1]
    halo = prev_ref.shape[1]
    r_rows = t_rows + 2 * halo
    prev = jnp.where(i > 0, prev_ref[0], 0.0)
    nxt = jnp.where(i < n - 1, next_ref[0], 0.0)
    ext = jnp.concatenate([prev, cur_ref[0], nxt], axis=0)

    def sh(a, dlt):
        return pltpu.roll(a, (-dlt) % r_rows, axis=0)

    u = ext[:, :POOL_WIDTH]
    a2 = u + sh(u, -1)
    a4 = sh(a2, -1) + sh(a2, 1)
    a8 = sh(a4, -2) + sh(a4, 2)
    a16 = sh(a8, -4) + sh(a8, 4)
    grp = lax.broadcasted_iota(jnp.int32, (1, POOL_WIDTH), 1) // POOL_GROUP_DIM
    win = jnp.where(grp == 0, a2, jnp.where(grp == 1, a4, jnp.where(grp == 2, a8, a16)))
    half = jnp.where(grp == 0, 1, jnp.where(grp == 1, 2, jnp.where(grp == 2, 4, 8)))
    pos = i * t_rows - halo + lax.broadcasted_iota(jnp.int32, (r_rows, 1), 0)
    cnt = jnp.minimum(pos + half, seq_len) - jnp.maximum(pos - half, 0)
    cnt = jnp.maximum(cnt, 1).astype(F32)
    dev = (win / cnt - u)[halo:halo + t_rows]
    pool = jnp.dot(dev.astype(BF16), wpool_ref[...], preferred_element_type=F32) * pscale_ref[...]
    pool_ref[0] = pool.astype(BF16)

    e = ext[:, OFF_HY:OFF_Q]
    cw = cw_ref[...]
    uc = sh(e, -1) * cw[0:1] + e * cw[1:2] + sh(e, 1) * cw[2:3] + cb_ref[...]
    uc = uc[halo:halo + t_rows]
    x0_ref[0] = uc[:, :HY_WIDTH]
    z_ref[0] = uc[:, HY_WIDTH:2 * HY_WIDTH] * uc[:, 2 * HY_WIDTH:]


def _seqmix(pph, wts):
    b, s, w = pph.shape
    t = min(ROW_TILE, s)
    halo = 8
    per = t // halo
    nh = s // halo
    full = lambda shape: pl.BlockSpec(shape, lambda bi, i: (0,) * len(shape))
    out = lambda: pl.BlockSpec((1, t, HY_WIDTH), lambda bi, i: (bi, i, 0))
    return pl.pallas_call(
        functools.partial(_seqmix_kernel, seq_len=s),
        grid=(b, s // t),
        in_specs=[pl.BlockSpec((1, halo, w), lambda bi, i: (bi, jnp.maximum(i * per - 1, 0), 0)),
                  pl.BlockSpec((1, t, w), lambda bi, i: (bi, i, 0)),
                  pl.BlockSpec((1, halo, w), lambda bi, i: (bi, jnp.minimum((i + 1) * per, nh - 1), 0)),
                  full((POOL_WIDTH, POOL_WIDTH)), full((1, POOL_WIDTH)), full((3, HY_IN)), full((1, HY_IN))],
        out_specs=[out(), out(), out()],
        out_shape=[jax.ShapeDtypeStruct((b, s, POOL_WIDTH), BF16),
                   jax.ShapeDtypeStruct((b, s, HY_WIDTH), F32),
                   jax.ShapeDtypeStruct((b, s, HY_WIDTH), F32)],
        compiler_params=_cparams("parallel", "parallel"),
        name="seqmix",
    )(pph, pph, pph, wts["wpool"], wts["pool_scale"], wts["conv_w"], wts["conv_b"])


def _leftmm_kernel(*refs, cmul, gate):
    m_ref, x_ref = refs[0], refs[1]
    nxt = 2
    x = x_ref[0]
    if cmul:
        ks = refs[nxt][...]
        nxt += 1
        kh = x.shape[0] // 2
        xr, xi, kr, ki = x[:kh], x[kh:], ks[:kh], ks[kh:]
        x = jnp.concatenate([xr * kr - xi * ki, xr * ki + xi * kr], axis=0)
    y = jnp.dot(m_ref[...], x, precision=HIGHEST, preferred_element_type=F32)
    if gate:
        z_ref, bias_ref, x0_ref = refs[nxt:nxt + 3]
        nxt += 3
        y = (y + z_ref[0] * bias_ref[...]) * x0_ref[0]
    o_ref = refs[nxt]
    o_ref[0] = y.astype(o_ref.dtype)


def _leftmm(m, x, spectrum=None, gate=None, out_dtype=F32):
    g, k, cols = x.shape
    mo = m.shape[0]
    tc = min(cols, 2048)
    args = [m, x]
    in_specs = [pl.BlockSpec((mo, k), lambda gi, j: (0, 0)), pl.BlockSpec((1, k, tc), lambda gi, j: (gi, 0, j))]
    if spectrum is not None:
        args.append(spectrum)
        in_specs.append(pl.BlockSpec((k, tc), lambda gi, j: (0, j)))
    if gate is not None:
        z, bias, x0 = gate
        args += [z, bias, x0]
        in_specs += [pl.BlockSpec((1, mo, tc), lambda gi, j: (gi, 0, j)),
                     pl.BlockSpec((1, tc), lambda gi, j: (0, j)),
                     pl.BlockSpec((1, mo, tc), lambda gi, j: (gi, 0, j))]
    return pl.pallas_call(
        functools.partial(_leftmm_kernel, cmul=spectrum is not None, gate=gate is not None),
        grid=(g, cols // tc),
        in_specs=in_specs,
        out_specs=pl.BlockSpec((1, mo, tc), lambda gi, j: (gi, 0, j)),
        out_shape=jax.ShapeDtypeStruct((g, mo, cols), out_dtype),
        compiler_params=_cparams("parallel", "parallel"),
        name="dft_leftmm",
    )(*args)


def _blockfft_kernel(*refs, n2, blocks, conv):
    if conv:
        a_ref, tw_ref, ks_ref, m2_ref, m2c_ref, o_ref = refs
    else:
        a_ref, tw_ref, m2_ref, o_ref = refs

    def body(j, carry):
        rows = pl.ds(pl.multiple_of(j * n2, n2), n2)
        ar, ai = a_ref[0, 0, rows, :], a_ref[0, 1, rows, :]
        tcs, tsn = tw_ref[0, rows, :], tw_ref[1, rows, :]
        tcs = jnp.concatenate([tcs] * (ar.shape[1] // tcs.shape[1]), axis=1)
        tsn = jnp.concatenate([tsn] * (ar.shape[1] // tsn.shape[1]), axis=1)
        bstk = jnp.concatenate([ar * tcs + ai * tsn, ai * tcs - ar * tsn], axis=0)
        xs = jnp.dot(m2_ref[...], bstk, precision=HIGHEST, preferred_element_type=F32)
        if not conv:
            o_ref[0, 0, rows, :] = xs[:n2]
            o_ref[0, 1, rows, :] = xs[n2:]
            return carry
        xr, xi = xs[:n2], xs[n2:]
        kr, ki = ks_ref[0, rows, :], ks_ref[1, rows, :]
        ystk = jnp.concatenate([xr * kr - xi * ki, xr * ki + xi * kr], axis=0)
        cs = jnp.dot(m2c_ref[...], ystk, precision=HIGHEST, preferred_element_type=F32)
        cr, ci = cs[:n2], cs[n2:]
        o_ref[0, 0, rows, :] = cr * tcs - ci * tsn
        o_ref[0, 1, rows, :] = cr * tsn + ci * tcs
        return carry

    lax.fori_loop(0, blocks, body, 0)


def _blockfft(a, tw, m2, spectrum=None, m2c=None):
    g, _, n, c = a.shape
    n2 = m2.shape[0] // 2
    blocks = min(8, n // n2)
    rows = blocks * n2
    conv = spectrum is not None
    args = [a, tw]
    in_specs = [pl.BlockSpec((1, 2, rows, c), lambda gi, j: (gi, 0, j, 0)),
                pl.BlockSpec((2, rows, LANE), lambda gi, j: (0, j, 0))]
    if conv:
        args.append(spectrum)
        in_specs.append(pl.BlockSpec((2, rows, c), lambda gi, j: (0, j, 0)))
    args.append(m2)
    in_specs.append(pl.BlockSpec(m2.shape, lambda gi, j: (0, 0)))
    if conv:
        args.append(m2c)
        in_specs.append(pl.BlockSpec(m2c.shape, lambda gi, j: (0, 0)))
    return pl.pallas_call(
        functools.partial(_blockfft_kernel, n2=n2, blocks=blocks, conv=conv),
        grid=(g, n // rows),
        in_specs=in_specs,
        out_specs=pl.BlockSpec((1, 2, rows, c), lambda gi, j: (gi, 0, j, 0)),
        out_shape=jax.ShapeDtypeStruct((g, 2, n, c), F32),
        compiler_params=_cparams("parallel", "parallel"),
        name="dft_block",
    )(*args)


def _filter_kernel(z_ref, w1_ref, b1_ref, w2_ref, b2_ref, w3_ref, fr_ref, dl_ref, o_ref, *, seq_len):
    i = pl.program_id(0)
    rows = z_ref.shape[0]
    z = z_ref[...]
    fr = fr_ref[...]
    h = jnp.sin(fr * (jnp.dot(z, w1_ref[...], precision=HIGHEST, preferred_element_type=F32) + b1_ref[...]))
    h = jnp.sin(fr * (jnp.dot(h, w2_ref[...], precision=HIGHEST, preferred_element_type=F32) + b2_ref[...]))
    hf = jnp.dot(h, w3_ref[...], precision=HIGHEST, preferred_element_type=F32)
    decay = jnp.exp(-z[:, 0:1] * dl_ref[...])
    j = i * rows + lax.broadcasted_iota(jnp.int32, (rows, 1), 0)
    taps = jnp.where(j < seq_len, hf[:, :HY_WIDTH], jnp.where(j > seq_len, hf[:, HY_WIDTH:], 0.0))
    o_ref[...] = taps * decay


def _hyena_filter(seq_len, wts):
    n = 2 * seq_len
    t = jnp.linspace(0.0, 1.0, seq_len, dtype=F32)[:, None]
    wpos = (2.0 * math.pi / seq_len) * jnp.arange(seq_len, dtype=F32)[:, None]
    f = jnp.linspace(1e-4, HY_BANDS - 1, HY_BANDS, dtype=F32)[None, :]
    feat = jnp.concatenate([t, jnp.cos(f * wpos), -jnp.sin(f * wpos)], axis=-1)
    j = jnp.arange(n)
    lag = jnp.clip(jnp.where(j < seq_len, j, n - j), 0, seq_len - 1)
    feat2 = jnp.pad(feat[lag], ((0, 0), (0, LANE - HY_EMB)))
    rows = min(n, 2048)
    full = lambda shape: pl.BlockSpec(shape, lambda i: (0,) * len(shape))
    return pl.pallas_call(
        functools.partial(_filter_kernel, seq_len=seq_len),
        grid=(n // rows,),
        in_specs=[pl.BlockSpec((rows, LANE), lambda i: (i, 0)),
                  full((LANE, LANE)), full((1, LANE)), full((LANE, LANE)), full((1, LANE)),
                  full((LANE, 2 * HY_WIDTH)), full((1, LANE)), full((1, HY_WIDTH))],
        out_specs=pl.BlockSpec((rows, HY_WIDTH), lambda i: (i, 0)),
        out_shape=jax.ShapeDtypeStruct((n, HY_WIDTH), F32),
        compiler_params=_cparams("parallel"),
        name="hyena_filter",
    )(feat2, wts["f_w1"], wts["f_b1"], wts["f_w2"], wts["f_b2"], wts["f_w3"], wts["f_freq"], wts["f_delta"])


def _cs(p, rows, cols):
    ang = 2.0 * np.pi * np.outer(np.arange(rows), np.arange(cols)) / p
    return np.cos(ang), np.sin(ang)


def _mat(blocks):
    return jnp.asarray(np.block(blocks), dtype=F32)


def _hyena_long(z, x0, bias, filt):
    b, s, c = z.shape
    n = 2 * s
    n1 = 1 << ((n.bit_length() - 1 + 1) // 2)
    n2 = n // n1
    g = b // 2
    cols = n2 * c
    c1, s1 = _cs(n1, n1, n1)
    h1 = n1 // 2
    m_fwd = _mat([[c1[:, :h1], s1[:, :h1]], [-s1[:, :h1], c1[:, :h1]]])
    m_filt = _mat([[c1], [-s1]])
    m_inv = _mat([[c1[:h1], -s1[:h1]], [s1[:h1], c1[:h1]]]) * (1.0 / n)
    c2, s2 = _cs(n2, n2, n2)
    m2 = _mat([[c2, s2], [-s2, c2]])
    m2c = _mat([[c2, -s2], [s2, c2]])
    k1 = jnp.arange(n1, dtype=jnp.int32)[:, None]
    nn2 = jnp.arange(n2, dtype=jnp.int32)[None, :]
    ang = (2.0 * math.pi / n) * (k1 * nn2).astype(F32).reshape(n)
    tw = jnp.broadcast_to(jnp.stack([jnp.cos(ang), jnp.sin(ang)])[:, :, None], (2, n, LANE))

    spec = _leftmm(m_filt, filt.reshape(1, n1, cols)).reshape(1, 2, n, c)
    spec = _blockfft(spec, tw, m2)[0]
    zv = z.reshape(g, n1, cols)
    a = _leftmm(m_fwd, zv).reshape(g, 2, n, c)
    dd = _blockfft(a, tw, m2, spectrum=spec, m2c=m2c).reshape(g, 2 * n1, cols)
    bias_t = jnp.tile(bias, (1, n2))
    y = _leftmm(m_inv, dd, gate=(zv, bias_t, x0.reshape(g, n1, cols)), out_dtype=BF16)
    return y.reshape(b, s, c)


def _hyena_short(z, x0, bias, filt):
    b, s, c = z.shape
    n = 2 * s
    g = b // 2
    cf, sf = _cs(n, n, n)
    m_fwd = _mat([[cf[:, :s], sf[:, :s]], [-sf[:, :s], cf[:, :s]]])
    m_filt = _mat([[cf], [-sf]])
    m_inv = _mat([[cf[:s], -sf[:s]], [sf[:s], cf[:s]]]) * (1.0 / n)
    spec = _leftmm(m_filt, filt.reshape(1, n, c))[0]
    zv = z.reshape(g, n, c)
    zf = _leftmm(m_fwd, zv)
    y = _leftmm(m_inv, zf, spectrum=spec, gate=(zv, bias, x0.reshape(g, n, c)), out_dtype=BF16)
    return y.reshape(b, s, c)


def _attn_kernel(*refs, tk, n_chunks, heads, unroll):
    if n_chunks:
        q_ref, kc_ref, vc_ref, kl_ref, vl_ref, o_ref, s_ref = refs
    else:
        q_ref, kc_ref, vc_ref, o_ref = refs
    dn = (((1,), (1,)), ((), ()))
    qs = [q_ref[0, hh] for hh in range(heads)]

    def chunk_rows(j):
        return pl.ds(pl.multiple_of(j * tk, tk), tk)

    def scores(j, slot):
        for hh in range(heads):
            s_ref[slot, hh] = lax.dot_general(qs[hh], kl_ref[0, hh, chunk_rows(j), :], dn,
                                              preferred_element_type=F32)

    def absorb(j, slot, carry):
        out = []
        for hh in range(heads):
            m_old, acc_old = carry[hh]
            sj = s_ref[slot, hh]
            m_new = jnp.maximum(m_old, jnp.max(sj, axis=-1, keepdims=True))
            pj = jnp.exp2(sj - m_new)
            acc_new = jnp.exp2(m_old - m_new) * acc_old + jnp.dot(pj.astype(BF16), vl_ref[0, hh, chunk_rows(j), :],
                                                                  preferred_element_type=F32)
            out.append((m_new, acc_new))
        return tuple(out)

    if n_chunks:
        scores(0, 0)
    state = []
    for hh in range(heads):
        s = lax.dot_general(qs[hh], kc_ref[0, hh], dn, preferred_element_type=F32)
        m = jnp.max(s, axis=-1, keepdims=True)
        p = jnp.exp2(s - m)
        state.append((m, jnp.dot(p.astype(BF16), vc_ref[0, hh], preferred_element_type=F32)))

    def body(jj, carry):
        j = unroll * jj
        for u in range(unroll):
            nxt = j + u + 1
            if u == unroll - 1:
                nxt = jnp.minimum(nxt, n_chunks - 1)
            scores(nxt, (u + 1) % 2)
            carry = absorb(j + u, u % 2, carry)
        return carry

    if n_chunks:
        state = lax.fori_loop(0, n_chunks // unroll, body, tuple(state))
    outs = [acc[:, :V_DIM] / acc[:, V_DIM:V_DIM + 1] for _, acc in state]
    o_ref[0] = jnp.concatenate(outs, axis=-1).astype(o_ref.dtype)


def _attention(q, kc, vc, kl=None, vl=None):
    b, h, s, dp = q.shape
    nc = kc.shape[2]
    tq = min(ROW_TILE, s)
    heads = LANE // V_DIM
    tk = 512
    n_chunks = 0 if kl is None else kl.shape[2] // tk
    unroll = 8 if n_chunks % 8 == 0 else 2
    args = [q, kc, vc]
    in_specs = [pl.BlockSpec((1, heads, tq, dp), lambda bi, hi, i: (bi, hi, i, 0)),
                pl.BlockSpec((1, heads, nc, dp), lambda bi, hi, i: (bi, hi, 0, 0)),
                pl.BlockSpec((1, heads, nc, dp), lambda bi, hi, i: (bi, hi, 0, 0))]
    scratch = []
    if n_chunks:
        assert unroll % 2 == 0 and n_chunks % unroll == 0
        sl = kl.shape[2]
        args += [kl, vl]
        in_specs += [pl.BlockSpec((1, heads, sl, dp), lambda bi, hi, i: (bi, hi, 0, 0))] * 2
        scratch = [pltpu.VMEM((2, heads, tq, tk), F32)]
    return pl.pallas_call(
        functools.partial(_attn_kernel, tk=tk, n_chunks=n_chunks, heads=heads, unroll=unroll),
        grid=(b, h // heads, s // tq),
        in_specs=in_specs,
        out_specs=pl.BlockSpec((1, tq, heads * V_DIM), lambda bi, hi, i: (bi, i, hi)),
        out_shape=jax.ShapeDtypeStruct((b, s, h * V_DIM), BF16),
        scratch_shapes=scratch,
        compiler_params=_cparams("parallel", "parallel", "parallel"),
        name="attention",
    )(*args)


def _outproj_kernel(x_ref, pool_ref, hy_ref, at_ref, wout_ref, g1_ref, gn_ref, sc_ref, sh_ref, rw_ref, rb_ref,
                    x1_ref, h2_ref, lg_ref):
    mix = jnp.concatenate([pool_ref[0], hy_ref[0], at_ref[0]], axis=-1)
    y = jnp.dot(mix, wout_ref[...], preferred_element_type=F32)
    x1 = x_ref[0] + g1_ref[0] * y
    x1_ref[0] = x1
    h2 = _rms(x1, gn_ref[...], D_MODEL) * (1.0 + sc_ref[0]) + sh_ref[0]
    h2_ref[0] = _pack_rows(h2)
    lg_ref[0] = lax.dot_general(rw_ref[...], h2, (((1,), (1,)), ((), ())), precision=HIGHEST,
                                preferred_element_type=F32) + rb_ref[...]


def _outproj(x, pool, hy, attn, seg_of_batch, wts, mod):
    b, s, d = x.shape
    t = min(ROW_TILE, s)
    full = lambda shape: pl.BlockSpec(shape, lambda bi, i: (0,) * len(shape))
    seg = lambda: pl.BlockSpec((1, 1, d), lambda bi, i: (seg_of_batch(bi), 0, 0))
    rowblk = lambda w: pl.BlockSpec((1, t, w), lambda bi, i: (bi, i, 0))
    return pl.pallas_call(
        _outproj_kernel,
        grid=(b, s // t),
        in_specs=[rowblk(d), rowblk(POOL_WIDTH), rowblk(HY_WIDTH), rowblk(MLA_WIDTH),
                  full((d, d)), seg(), full((1, d)), seg(), seg(), full((N_EXPERTS, d)), full((N_EXPERTS, 1))],
        out_specs=[rowblk(d), rowblk(d // 2), pl.BlockSpec((1, N_EXPERTS, t), lambda bi, i: (bi, 0, i))],
        out_shape=[jax.ShapeDtypeStruct((b, s, d), F32), jax.ShapeDtypeStruct((b, s, d // 2), jnp.int32),
                   jax.ShapeDtypeStruct((b, N_EXPERTS, s), F32)],
        compiler_params=_cparams("parallel", "parallel"),
        name="outproj",
    )(x, pool, hy, attn, wts["w_out"], mod["g1"], wts["norm2_g"], mod["sc2"], mod["sh2"],
      wts["router_w"], wts["router_b"])


def _moe_kernel(be_ref, nu_ref, x_ref, w1_ref, b1_ref, w2_ref, b2_ref, o_ref, w1b_ref, w2b_ref):
    i = pl.program_id(0)
    e = be_ref[i]
    e_prev = be_ref[jnp.maximum(i - 1, 0)]

    @pl.when((i == 0) | (e != e_prev))
    def _():
        w1b_ref[...] = w1_ref[0, 0].astype(BF16)
        w2b_ref[...] = w2_ref[0, 0].astype(BF16)

    @pl.when(i < nu_ref[0])
    def _():
        a = jnp.dot(_unpack_rows(x_ref[...]), w1b_ref[...], preferred_element_type=F32) + b1_ref[0, 0]
        glu = jnp.minimum(a[:, :D_EXPERT], SWIGLU_LIMIT)
        lin = jnp.clip(a[:, D_EXPERT:], -SWIGLU_LIMIT, SWIGLU_LIMIT)
        act = glu * (1.0 / (1.0 + jnp.exp(-SWIGLU_ALPHA * glu))) * (lin + 1.0)
        y = jnp.dot(act.astype(BF16), w2b_ref[...], preferred_element_type=F32) + b2_ref[0, 0]
        o_ref[...] = _pack_rows(y)

    @pl.when(i >= nu_ref[0])
    def _():
        o_ref[...] = jnp.zeros_like(o_ref)


def _experts(xs, block_e, n_used, w1, b1, w2, b2, layer):
    n_buf, dh = xs.shape
    d = 2 * dh
    tm = EXPERT_TILE
    n_blocks = n_buf // tm
    de2 = w1.shape[3]
    grid_spec = pltpu.PrefetchScalarGridSpec(
        num_scalar_prefetch=2,
        grid=(n_blocks,),
        in_specs=[pl.BlockSpec((tm, dh), lambda i, be, nu: (i, 0)),
                  pl.BlockSpec((1, 1, d, de2), lambda i, be, nu: (layer, be[i], 0, 0)),
                  pl.BlockSpec((1, 1, 1, de2), lambda i, be, nu: (layer, be[i], 0, 0)),
                  pl.BlockSpec((1, 1, D_EXPERT, d), lambda i, be, nu: (layer, be[i], 0, 0)),
                  pl.BlockSpec((1, 1, 1, d), lambda i, be, nu: (layer, be[i], 0, 0))],
        out_specs=pl.BlockSpec((tm, dh), lambda i, be, nu: (i, 0)),
        scratch_shapes=[pltpu.VMEM((d, de2), BF16), pltpu.VMEM((D_EXPERT, d), BF16)],
    )
    return pl.pallas_call(
        _moe_kernel,
        grid_spec=grid_spec,
        out_shape=jax.ShapeDtypeStruct((n_buf, dh), jnp.int32),
        compiler_params=_cparams("arbitrary"),
        name="experts",
    )(block_e, n_used, xs, w1, b1, w2, b2)


def _router_kernel(lg_ref, idx_ref, gate_ref, rank_ref, cnt_ref, base_ref):
    i = pl.program_id(0)
    n_e, t = lg_ref.shape

    @pl.when(i == 0)
    def _():
        base_ref[...] = jnp.zeros_like(base_ref)

    work = lg_ref[...]
    eid = lax.broadcasted_iota(jnp.int32, (n_e, t), 0)
    vals, idxs, sels = [], [], []
    for _ in range(TOP_K):
        m = jnp.max(work, axis=0, keepdims=True)
        idx = jnp.min(jnp.where(work == m, eid, n_e), axis=0, keepdims=True)
        sel = eid == idx
        work = jnp.where(sel, -jnp.inf, work)
        vals.append(m)
        idxs.append(idx)
        sels.append(sel)
    cnt = jnp.where(sels[0], 1.0, 0.0)
    for sel in sels[1:]:
        cnt = cnt + jnp.where(sel, 1.0, 0.0)
    earlier = (lax.broadcasted_iota(jnp.int32, (t, t), 0) < lax.broadcasted_iota(jnp.int32, (t, t), 1))
    within = jnp.dot(cnt.astype(BF16), jnp.where(earlier, 1.0, 0.0).astype(BF16), preferred_element_type=F32)
    pref = base_ref[:, 0:1] + within
    ranks = [jnp.sum(jnp.where(sel, pref, 0.0), axis=0, keepdims=True) for sel in sels]
    new_base = base_ref[...] + jnp.sum(cnt, axis=1, keepdims=True)
    base_ref[...] = new_base
    cnt_ref[...] = new_base
    ex = [jnp.exp(v - vals[0]) for v in vals]
    den = ex[0]
    for e in ex[1:]:
        den = den + e
    idx_ref[...] = jnp.concatenate(idxs, axis=0)
    gate_ref[...] = jnp.concatenate([e / den for e in ex], axis=0)
    rank_ref[...] = jnp.concatenate(ranks, axis=0).astype(jnp.int32)


def _route(logits_t):
    n_e, n_tok = logits_t.shape
    tm = EXPERT_TILE
    t = ROW_TILE
    tokblk = lambda: pl.BlockSpec((TOP_K, t), lambda i: (0, i))
    idx_t, gates_t, rank_t, counts_f = pl.pallas_call(
        _router_kernel,
        grid=(n_tok // t,),
        in_specs=[pl.BlockSpec((n_e, t), lambda i: (0, i))],
        out_specs=[tokblk(), tokblk(), tokblk(), pl.BlockSpec((n_e, LANE), lambda i: (0, 0))],
        out_shape=[jax.ShapeDtypeStruct((TOP_K, n_tok), jnp.int32), jax.ShapeDtypeStruct((TOP_K, n_tok), F32),
                   jax.ShapeDtypeStruct((TOP_K, n_tok), jnp.int32), jax.ShapeDtypeStruct((n_e, LANE), F32)],
        scratch_shapes=[pltpu.VMEM((n_e, LANE), F32)],
        compiler_params=_cparams("arbitrary"),
        name="router",
    )(logits_t)
    n_slots = n_tok * TOP_K
    counts = counts_f[:, 0].astype(jnp.int32)
    padded = (counts + tm - 1) // tm * tm
    padded_end = jnp.cumsum(padded)
    padded_start = padded_end - padded
    start = jnp.cumsum(counts) - counts
    experts = jnp.arange(n_e, dtype=jnp.int32)
    dest_t = rank_t + jnp.sum(jnp.where(idx_t[None] == experts[:, None, None], padded_start[:, None, None], 0), axis=0)
    n_blocks = n_slots // tm + n_e
    block_start = jnp.arange(n_blocks, dtype=jnp.int32) * tm
    block_e = jnp.minimum(jnp.sum(padded_end[None, :] <= block_start[:, None], axis=1), n_e - 1).astype(jnp.int32)
    n_used = (padded_end[-1] // tm).astype(jnp.int32).reshape(1)
    order = jnp.argsort(idx_t.T.reshape(n_slots)).astype(jnp.int32)
    sel = (block_e[:, None] == experts[None, :])
    pick = lambda v: jnp.sum(jnp.where(sel, v[None, :], 0), axis=1)[:, None]
    row = block_start[:, None] + jnp.arange(tm, dtype=jnp.int32)[None, :]
    srt = row - pick(padded_start) + pick(start)
    valid = srt < pick(start) + pick(counts)
    buf_tok = jnp.where(valid, order[jnp.clip(srt, 0, n_slots - 1)] // TOP_K, 0).reshape(n_blocks * tm)
    return gates_t.T, buf_tok, dest_t, block_e, n_used


SC_WINDOW = 64


def _sc_gather(table, idx):
    _, d = table.shape
    b = idx.shape[0]
    info = plsc.get_sparse_core_info()
    nc, ns = info.num_cores, info.num_subcores
    nw = nc * ns
    w = SC_WINDOW
    per_w = b // nw
    n_win = per_w // w
    assert per_w * nw == b and n_win * w == per_w and n_win % 2 == 0
    mesh = plsc.VectorSubcoreMesh(core_axis_name="c", subcore_axis_name="s")

    def body(table_hbm, idx_hbm, out_hbm, idx_v, rows_v, gsem, osem):
        wid = lax.axis_index("s") * nc + lax.axis_index("c")
        base = wid * per_w
        pltpu.sync_copy(idx_hbm.at[wid], idx_v)

        def gather(g, slot):
            return pltpu.make_async_copy(table_hbm.at[idx_v.at[g]], rows_v.at[slot], gsem.at[slot])

        def put(g, slot):
            return pltpu.make_async_copy(rows_v.at[slot], out_hbm.at[pl.ds(base + g * w, w)], osem.at[slot])

        gather(0, 0).start()

        @pl.loop(0, n_win, step=2)
        def _(g0):
            for slot in range(2):
                g = g0 + slot
                gather(g, slot).wait()
                put(g, slot).start()

                @pl.when(g + 1 < n_win)
                def _():
                    @pl.when(g >= 1)
                    def _():
                        put(g - 1, 1 - slot).wait()

                    gather(g + 1, 1 - slot).start()

        put(n_win - 2, 0).wait()
        put(n_win - 1, 1).wait()

    return pl.kernel(
        body,
        out_type=jax.ShapeDtypeStruct((b, d), table.dtype),
        mesh=mesh,
        scratch_types=[pltpu.VMEM((n_win, w), jnp.int32), pltpu.VMEM((2, w, d), table.dtype),
                       pltpu.SemaphoreType.DMA((2,)), pltpu.SemaphoreType.DMA((2,))],
        name="sc_gather",
    )(table, idx.reshape(nw, n_win, w))


def _combine_kernel(x_ref, y_ref, gt_ref, g2_ref, o_ref):
    gt = gt_ref[...]
    acc = gt[:, 0:1] * _unpack_rows(y_ref[0]).astype(F32)
    for j in range(1, TOP_K):
        acc = acc + gt[:, j:j + 1] * _unpack_rows(y_ref[j]).astype(F32)
    o_ref[0] = x_ref[0] + g2_ref[0] * acc


def _combine(x, ysel, gates, tok_offset, seg_of_batch, mod):
    b, s, d = x.shape
    t = min(ROW_TILE, s)
    per = s // t
    off = tok_offset // t
    return pl.pallas_call(
        _combine_kernel,
        grid=(b, per),
        in_specs=[pl.BlockSpec((1, t, d), lambda bi, i: (bi, i, 0)),
                  pl.BlockSpec((TOP_K, t, d // 2), lambda bi, i: (0, off + bi * per + i, 0)),
                  pl.BlockSpec((t, TOP_K), lambda bi, i: (off + bi * per + i, 0)),
                  pl.BlockSpec((1, 1, d), lambda bi, i: (seg_of_batch(bi), 0, 0))],
        out_specs=pl.BlockSpec((1, t, d), lambda bi, i: (bi, i, 0)),
        out_shape=jax.ShapeDtypeStruct((b, s, d), F32),
        compiler_params=_cparams("parallel", "parallel"),
        name="combine",
    )(x, ysel, gates, mod["g2"])


def _rope_tables(seq_len):
    n_rows = seq_len // GRID_W
    row = jnp.repeat(jnp.arange(n_rows, dtype=F32), GRID_W)
    col = jnp.tile(jnp.arange(GRID_W, dtype=F32), n_rows)
    pairs = ROPE // 4
    inv = ROPE_BASE ** (-jnp.arange(pairs, dtype=F32) / pairs)
    ang = jnp.concatenate([row[:, None] * inv, col[:, None] * inv], axis=-1)
    cos, sin = jnp.cos(ang), jnp.sin(ang)
    half = ROPE // 2
    one = jnp.ones((seq_len, NOPE), F32)
    zn = jnp.zeros((seq_len, NOPE), F32)
    zh = jnp.zeros((seq_len, half), F32)
    zp = jnp.zeros((seq_len, HEAD_PAD - QK_DIM), F32)
    return (jnp.concatenate([one, cos, cos, zp], axis=-1),
            jnp.concatenate([zn, -sin, zh, zp], axis=-1),
            jnp.concatenate([zn, zh, sin, zp], axis=-1))


def _identity_rope(seq_len):
    keep = jnp.concatenate([jnp.ones((seq_len, QK_DIM), F32), jnp.zeros((seq_len, HEAD_PAD - QK_DIM), F32)], axis=-1)
    zero = jnp.zeros((seq_len, HEAD_PAD), F32)
    return keep, zero, zero


def _pad_cols(w, n):
    return jnp.pad(w, ((0, 0), (0, n - w.shape[1])))


def _layer_weights(l, p):
    hw = HEADS * HEAD_PAD
    wq = jnp.pad(p["mla_w_uq"][l].reshape(Q_RANK, HEADS, QK_DIM), ((0, 0), (0, 0), (0, HEAD_PAD - QK_DIM)))
    wukv = p["mla_w_ukv"][l].reshape(KV_RANK, HEADS, NOPE + V_DIM)
    wk = jnp.pad(wukv[:, :, :NOPE], ((0, 0), (0, 0), (0, HEAD_PAD - NOPE))).reshape(KV_RANK, hw)
    wv = jnp.pad(wukv[:, :, NOPE:], ((0, 0), (0, 0), (0, HEAD_PAD - V_DIM))).reshape(KV_RANK, hw)
    place = jnp.pad(jnp.eye(ROPE, dtype=F32), ((0, KV_RANK - ROPE), (NOPE, HEAD_PAD - QK_DIM)))
    wk_rope = jnp.tile(place, (1, HEADS))
    wkv = jnp.concatenate([jnp.concatenate([wk, wv], axis=1),
                           jnp.concatenate([wk_rope, jnp.zeros((KV_RANK, hw), F32)], axis=1)], axis=0)
    wpool = jax.scipy.linalg.block_diag(*[p["pool_w"][l, g] for g in range(POOL_GROUPS)])
    vone = jnp.zeros((1, HEAD_PAD), F32).at[0, V_DIM].set(1.0)
    deltas = jnp.abs(jnp.linspace(HY_MIN_DECAY, HY_MAX_DECAY, HY_WIDTH, dtype=F32))[None, :]
    row = lambda v, n=None: (v if n is None else jnp.pad(v, (0, n - v.shape[0])))[None, :]
    return {
        "norm1_g": row(p["norm1_g"][l]), "norm2_g": row(p["norm2_g"][l]),
        "w_in": _pad_cols(p["w_in"][l], N_IN_PAD).astype(BF16),
        "gq": row(p["mla_q_norm_g"][l]), "wq": wq.reshape(Q_RANK, hw).astype(BF16),
        "gkv": row(p["mla_kv_norm_g"][l]), "wkv": wkv.astype(BF16),
        "gqq": row(p["qk_norm_q"][l], HEAD_PAD), "gqk": row(p["qk_norm_k"][l], HEAD_PAD), "vone": vone,
        "wpool": wpool.astype(BF16), "pool_scale": row(p["pool_scale"][l]),
        "conv_w": p["hy_conv_w"][l], "conv_b": row(p["hy_conv_b"][l]),
        "f_w1": jnp.pad(p["hy_f_w1"][l], ((0, LANE - HY_EMB), (0, LANE - HY_HID))),
        "f_b1": row(p["hy_f_b1"][l], LANE),
        "f_w2": jnp.pad(p["hy_f_w2"][l], ((0, LANE - HY_HID), (0, LANE - HY_HID))),
        "f_b2": row(p["hy_f_b2"][l], LANE),
        "f_w3": jnp.pad(p["hy_f_w3"][l], ((0, LANE - HY_HID), (0, 0))),
        "f_freq": row(p["hy_freq"][l], LANE), "f_delta": deltas,
        "hy_bias": row(p["hy_bias"][l]),
        "w_out": p["w_out"][l].astype(BF16),
        "router_w": p["router_w"][l].T, "router_b": p["router_b"][l][:, None],
    }


def _layer(l, x, xc, cc, wts, p, rope_main, rope_ctx, last):
    b, s, d = x.shape
    n_ctx = xc.shape[1]
    modm = _modulation(cc, p["w_mod"], p["b_mod"][l][None, :], l)
    names = ("sh1", "sc1", "g1", "sh2", "sc2", "g2")
    mod = {nm: modm[:, i * d:(i + 1) * d].reshape(N_SEG, 1, d) for i, nm in enumerate(names)}
    main_seg = lambda bi: bi
    ctx_seg = lambda bi: CTX_SEG

    pph, q, kl, vl = _inproj(x, main_seg, wts, mod, rope_main)
    pph_c, qc, kc, vc = _inproj(xc, ctx_seg, wts, mod, rope_ctx)
    attn = _attention(q, kc, vc, kl, vl)
    pool, x0, z = _seqmix(pph, wts)
    hy = _hyena_long(z, x0, wts["hy_bias"], _hyena_filter(s, wts))
    x1, h2, logits = _outproj(x, pool, hy, attn, main_seg, wts, mod)
    h2 = h2.reshape(b * s, d // 2)
    logits = jnp.swapaxes(logits, 0, 1).reshape(N_EXPERTS, b * s)
    n_c = 0
    if not last:
        attn_c = _attention(qc, kc, vc)
        pool_c, x0c, zc = _seqmix(pph_c, wts)
        hy_c = _hyena_short(zc, x0c, wts["hy_bias"], _hyena_filter(n_ctx, wts))
        xc1, h2c, logits_c = _outproj(xc, pool_c, hy_c, attn_c, ctx_seg, wts, mod)
        n_c = b * n_ctx
        h2 = jnp.concatenate([h2c.reshape(n_c, d // 2), h2], axis=0)
        logits = jnp.concatenate([jnp.swapaxes(logits_c, 0, 1).reshape(N_EXPERTS, n_c), logits], axis=1)

    gates, buf_tok, dest_t, block_e, n_used = _route(logits)
    de2 = p["moe_w1"].shape[-1]
    ys = _experts(_sc_gather(h2, buf_tok), block_e, n_used, p["moe_w1"], p["moe_b1"].reshape(-1, N_EXPERTS, 1, de2),
                  p["moe_w2"], p["moe_b2"].reshape(-1, N_EXPERTS, 1, d), l)
    ysel = _sc_gather(ys, dest_t.reshape(-1)).reshape(TOP_K, -1, d // 2)
    x2 = _combine(x1, ysel, gates, n_c, main_seg, mod)
    if not last:
        xc = _combine(xc1, ysel, gates, 0, ctx_seg, mod)
    return x2, xc


def kernel(x, c, ctx, c_ctx, norm1_g, norm2_g, w_mod, b_mod, w_in, pool_w, pool_scale, hy_conv_w, hy_conv_b,
           hy_f_w1, hy_f_b1, hy_f_w2, hy_f_b2, hy_f_w3, hy_freq, hy_bias, mla_q_norm_g, mla_w_uq, mla_kv_norm_g,
           mla_w_ukv, qk_norm_q, qk_norm_k, w_out, router_w, router_b, moe_w1, moe_b1, moe_w2, moe_b2):
    p = dict(norm1_g=norm1_g, norm2_g=norm2_g, w_mod=w_mod, b_mod=b_mod, w_in=w_in, pool_w=pool_w,
             pool_scale=pool_scale,
             hy_conv_w=hy_conv_w, hy_conv_b=hy_conv_b, hy_f_w1=hy_f_w1, hy_f_b1=hy_f_b1, hy_f_w2=hy_f_w2,
             hy_f_b2=hy_f_b2, hy_f_w3=hy_f_w3, hy_freq=hy_freq, hy_bias=hy_bias, mla_q_norm_g=mla_q_norm_g,
             mla_w_uq=mla_w_uq, mla_kv_norm_g=mla_kv_norm_g, mla_w_ukv=mla_w_ukv, qk_norm_q=qk_norm_q,
             qk_norm_k=qk_norm_k, w_out=w_out, router_w=router_w, router_b=router_b, moe_w1=moe_w1,
             moe_b1=moe_b1, moe_w2=moe_w2, moe_b2=moe_b2)
    b, s, d = x.shape
    depth = w_mod.shape[0]
    cc = jnp.zeros((N_SEG, d), F32).at[:b].set(c).at[CTX_SEG].set(c_ctx)
    rope_main = _rope_tables(s)
    rope_ctx = _identity_rope(ctx.shape[1])
    xc = ctx
    for l in range(depth):
        wts = _layer_weights(l, p)
        x, xc = _layer(l, x, xc, cc, wts, p, rope_main, rope_ctx, l == depth - 1)
    return x
```

```python
import functools
import math

import jax
import jax.numpy as jnp
import numpy as np
from jax import lax
from jax.experimental import pallas as pl
from jax.experimental.pallas import tpu as pltpu
from jax.experimental.pallas import tpu_sc as plsc

F32 = jnp.float32
BF16 = jnp.bfloat16
HIGHEST = lax.Precision.HIGHEST

D_MODEL = 1024
GRID_W = 64
EPS = 1e-6
POOL_WIDTH = 256
POOL_GROUPS = 4
POOL_GROUP_DIM = POOL_WIDTH // POOL_GROUPS
HY_WIDTH = 256
HY_IN = 3 * HY_WIDTH
HY_BANDS = 8
HY_EMB = 1 + 2 * HY_BANDS
HY_HID = 64
HY_MIN_DECAY = math.log(1e-2) / 1.5
HY_MAX_DECAY = math.log(1e-2) / 0.3
HEADS = 8
Q_RANK = 256
KV_RANK = 128
NOPE = 64
ROPE = 32
V_DIM = 64
QK_DIM = NOPE + ROPE
MLA_WIDTH = HEADS * V_DIM
ATTN_SCALE = QK_DIM ** -0.5
LOG2E = math.log2(math.e)
OFF_HY = POOL_WIDTH
OFF_Q = OFF_HY + HY_IN
OFF_KV = OFF_Q + Q_RANK
OFF_KR = OFF_KV + KV_RANK
N_IN = OFF_KR + ROPE
N_IN_PAD = OFF_KR + 128
ROPE_BASE = 10000.0
N_EXPERTS = 32
TOP_K = 4
D_EXPERT = 1024
SWIGLU_LIMIT = 7.0
SWIGLU_ALPHA = 1.702

LANE = 128
HEAD_PAD = LANE
ROW_TILE = 512
EXPERT_TILE = 512
VMEM_LIMIT = 56 * 1024 * 1024
N_SEG = 8
CTX_SEG = 4


def _cparams(*sem):
    return pltpu.CompilerParams(dimension_semantics=sem, vmem_limit_bytes=VMEM_LIMIT)


def _rms(x, g, n):
    ss = jnp.sum(x * x, axis=-1, keepdims=True) * (1.0 / n)
    return x * lax.rsqrt(ss + EPS) * g


def _pack_rows(x):
    half = x.shape[1] // 2
    bits = lax.bitcast_convert_type(x.astype(BF16).astype(F32), jnp.int32)
    return bits[:, :half] | lax.shift_right_logical(bits[:, half:], 16)


def _unpack_rows(w):
    hi = lax.bitcast_convert_type(w & jnp.int32(-65536), F32)
    lo = lax.bitcast_convert_type(lax.shift_left(w, 16), F32)
    return jnp.concatenate([hi, lo], axis=1).astype(BF16)


def _mod_kernel(c_ref, w_ref, b_ref, o_ref):
    c = c_ref[...]
    s = c * (1.0 / (1.0 + jnp.exp(-c)))
    o_ref[...] = jnp.dot(s, w_ref[0], precision=HIGHEST, preferred_element_type=F32) + b_ref[...]


def _modulation(cc, w, b, layer):
    _, d, n = w.shape
    tn = 1536
    return pl.pallas_call(
        _mod_kernel,
        grid=(n // tn,),
        in_specs=[pl.BlockSpec((N_SEG, d), lambda j: (0, 0)),
                  pl.BlockSpec((1, d, tn), lambda j: (layer, 0, j)),
                  pl.BlockSpec((1, tn), lambda j: (0, j))],
        out_specs=pl.BlockSpec((N_SEG, tn), lambda j: (0, j)),
        out_shape=jax.ShapeDtypeStruct((N_SEG, n), F32),
        compiler_params=_cparams("arbitrary"),
        name="modulation",
    )(cc, w, b)


def _inproj_kernel(x_ref, gn_ref, sc_ref, sh_ref, win_ref, gq_ref, wq_ref, gkv_ref, wkv_ref,
                   gqq_ref, gqk_ref, vone_ref, rc_ref, ra_ref, rb_ref,
                   pph_ref, q_ref, k_ref, v_ref):
    x = x_ref[0]
    h = _rms(x, gn_ref[...], D_MODEL) * (1.0 + sc_ref[0]) + sh_ref[0]
    p = jnp.dot(h.astype(BF16), win_ref[...], preferred_element_type=F32)
    pph_ref[0] = p[:, :OFF_Q]
    cq = _rms(p[:, OFF_Q:OFF_KV], gq_ref[...], Q_RANK)
    qall = jnp.dot(cq.astype(BF16), wq_ref[...], preferred_element_type=F32)
    ckv = _rms(p[:, OFF_KV:OFF_KR], gkv_ref[...], KV_RANK)
    kin = jnp.concatenate([ckv, p[:, OFF_KR:N_IN_PAD]], axis=-1).astype(BF16)
    kvall = jnp.dot(kin, wkv_ref[...], preferred_element_type=F32)
    rc, ra, rb = rc_ref[...], ra_ref[...], rb_ref[...]

    def rope(xh):
        return xh * rc + pltpu.roll(xh, HEAD_PAD - ROPE // 2, axis=1) * ra + pltpu.roll(xh, ROPE // 2, axis=1) * rb

    for hd in range(HEADS):
        sl = slice(HEAD_PAD * hd, HEAD_PAD * (hd + 1))
        qh = rope(_rms(qall[:, sl], gqq_ref[...], QK_DIM)) * (ATTN_SCALE * LOG2E)
        q_ref[0, hd] = qh.astype(BF16)
        kh = rope(_rms(kvall[:, sl], gqk_ref[...], QK_DIM))
        k_ref[0, hd] = kh.astype(BF16)
        vs = slice(HEADS * HEAD_PAD + HEAD_PAD * hd, HEADS * HEAD_PAD + HEAD_PAD * (hd + 1))
        v_ref[0, hd] = (kvall[:, vs] + vone_ref[...]).astype(BF16)


def _inproj(x, seg_of_batch, wts, mod, rope_tabs):
    b, s, d = x.shape
    t = min(ROW_TILE, s)
    full = lambda shape: pl.BlockSpec(shape, lambda bi, i: (0,) * len(shape))
    seg = lambda: pl.BlockSpec((1, 1, d), lambda bi, i: (seg_of_batch(bi), 0, 0))
    tab = lambda: pl.BlockSpec((t, HEAD_PAD), lambda bi, i: (i, 0))
    hw = HEADS * HEAD_PAD
    head_out = lambda: pl.BlockSpec((1, HEADS, t, HEAD_PAD), lambda bi, i: (bi, 0, i, 0))
    return pl.pallas_call(
        _inproj_kernel,
        grid=(b, s // t),
        in_specs=[pl.BlockSpec((1, t, d), lambda bi, i: (bi, i, 0)),
                  full((1, d)), seg(), seg(),
                  full((d, N_IN_PAD)), full((1, Q_RANK)), full((Q_RANK, hw)),
                  full((1, KV_RANK)), full((2 * KV_RANK, 2 * hw)),
                  full((1, HEAD_PAD)), full((1, HEAD_PAD)), full((1, HEAD_PAD)),
                  tab(), tab(), tab()],
        out_specs=[pl.BlockSpec((1, t, OFF_Q), lambda bi, i: (bi, i, 0)),
                   head_out(), head_out(), head_out()],
        out_shape=[jax.ShapeDtypeStruct((b, s, OFF_Q), F32)]
        + [jax.ShapeDtypeStruct((b, HEADS, s, HEAD_PAD), BF16)] * 3,
        compiler_params=_cparams("parallel", "parallel"),
        name="inproj",
    )(x, wts["norm1_g"], mod["sc1"], mod["sh1"], wts["w_in"], wts["gq"], wts["wq"], wts["gkv"], wts["wkv"],
      wts["gqq"], wts["gqk"], wts["vone"], *rope_tabs)


def _seqmix_kernel(prev_ref, cur_ref, next_ref, wpool_ref, pscale_ref, cw_ref, cb_ref,
                   pool_ref, x0_ref, z_ref, *, seq_len):
    i = pl.program_id(1)
    n = pl.num_programs(1)
    t_rows = cur_ref.shape[1]
    halo = prev_ref.shape[1]
    r_rows = t_rows + 2 * halo
    prev = jnp.where(i > 0, prev_ref[0], 0.0)
    nxt = jnp.where(i < n - 1, next_ref[0], 0.0)
    ext = jnp.concatenate([prev, cur_ref[0], nxt], axis=0)

    def sh(a, dlt):
        return pltpu.roll(a, (-dlt) % r_rows, axis=0)

    u = ext[:, :POOL_WIDTH]
    a2 = u + sh(u, -1)
    a4 = sh(a2, -1) + sh(a2, 1)
    a8 = sh(a4, -2) + sh(a4, 2)
    a16 = sh(a8, -4) + sh(a8, 4)
    grp = lax.broadcasted_iota(jnp.int32, (1, POOL_WIDTH), 1) // POOL_GROUP_DIM
    win = jnp.where(grp == 0, a2, jnp.where(grp == 1, a4, jnp.where(grp == 2, a8, a16)))
    half = jnp.where(grp == 0, 1, jnp.where(grp == 1, 2, jnp.where(grp == 2, 4, 8)))
    pos = i * t_rows - halo + lax.broadcasted_iota(jnp.int32, (r_rows, 1), 0)
    cnt = jnp.minimum(pos + half, seq_len) - jnp.maximum(pos - half, 0)
    cnt = jnp.maximum(cnt, 1).astype(F32)
    dev = (win / cnt - u)[halo:halo + t_rows]
    pool = jnp.dot(dev.astype(BF16), wpool_ref[...], preferred_element_type=F32) * pscale_ref[...]
    pool_ref[0] = pool.astype(BF16)

    e = ext[:, OFF_HY:OFF_Q]
    cw = cw_ref[...]
    uc = sh(e, -1) * cw[0:1] + e * cw[1:2] + sh(e, 1) * cw[2:3] + cb_ref[...]
    uc = uc[halo:halo + t_rows]
    x0_ref[0] = uc[:, :HY_WIDTH]
    z_ref[0] = uc[:, HY_WIDTH:2 * HY_WIDTH] * uc[:, 2 * HY_WIDTH:]


def _seqmix(pph, wts):
    b, s, w = pph.shape
    t = min(ROW_TILE, s)
    halo = 8
    per = t // halo
    nh = s // halo
    full = lambda shape: pl.BlockSpec(shape, lambda bi, i: (0,) * len(shape))
    out = lambda: pl.BlockSpec((1, t, HY_WIDTH), lambda bi, i: (bi, i, 0))
    return pl.pallas_call(
        functools.partial(_seqmix_kernel, seq_len=s),
        grid=(b, s // t),
        in_specs=[pl.BlockSpec((1, halo, w), lambda bi, i: (bi, jnp.maximum(i * per - 1, 0), 0)),
                  pl.BlockSpec((1, t, w), lambda bi, i: (bi, i, 0)),
                  pl.BlockSpec((1, halo, w), lambda bi, i: (bi, jnp.minimum((i + 1) * per, nh - 1), 0)),
                  full((POOL_WIDTH, POOL_WIDTH)), full((1, POOL_WIDTH)), full((3, HY_IN)), full((1, HY_IN))],
        out_specs=[out(), out(), out()],
        out_shape=[jax.ShapeDtypeStruct((b, s, POOL_WIDTH), BF16),
                   jax.ShapeDtypeStruct((b, s, HY_WIDTH), F32),
                   jax.ShapeDtypeStruct((b, s, HY_WIDTH), F32)],
        compiler_params=_cparams("parallel", "parallel"),
        name="seqmix",
    )(pph, pph, pph, wts["wpool"], wts["pool_scale"], wts["conv_w"], wts["conv_b"])


def _dot3(m_ref, x):
    x_hi = x.astype(BF16)
    x_lo = (x - x_hi.astype(F32)).astype(BF16)
    m_hi, m_lo = m_ref[0], m_ref[1]
    return (jnp.dot(m_hi, x_hi, preferred_element_type=F32) + jnp.dot(m_hi, x_lo, preferred_element_type=F32)
            + jnp.dot(m_lo, x_hi, preferred_element_type=F32))


def _split_const(m):
    hi = m.astype(BF16)
    return jnp.stack([hi, (m - hi.astype(F32)).astype(BF16)])


def _leftmm_kernel(*refs, cmul, gate):
    m_ref, x_ref = refs[0], refs[1]
    nxt = 2
    x = x_ref[0]
    if cmul:
        ks = refs[nxt][...]
        nxt += 1
        kh = x.shape[0] // 2
        xr, xi, kr, ki = x[:kh], x[kh:], ks[:kh], ks[kh:]
        x = jnp.concatenate([xr * kr - xi * ki, xr * ki + xi * kr], axis=0)
    y = jnp.dot(m_ref[...], x, precision=HIGHEST, preferred_element_type=F32)
    if gate:
        z_ref, bias_ref, x0_ref = refs[nxt:nxt + 3]
        nxt += 3
        y = (y + z_ref[0] * bias_ref[...]) * x0_ref[0]
    o_ref = refs[nxt]
    o_ref[0] = y.astype(o_ref.dtype)


def _leftmm(m, x, spectrum=None, gate=None, out_dtype=F32):
    g, k, cols = x.shape
    mo = m.shape[0]
    tc = min(cols, 2048)
    args = [m, x]
    in_specs = [pl.BlockSpec((mo, k), lambda gi, j: (0, 0)), pl.BlockSpec((1, k, tc), lambda gi, j: (gi, 0, j))]
    if spectrum is not None:
        args.append(spectrum)
        in_specs.append(pl.BlockSpec((k, tc), lambda gi, j: (0, j)))
    if gate is not None:
        z, bias, x0 = gate
        args += [z, bias, x0]
        in_specs += [pl.BlockSpec((1, mo, tc), lambda gi, j: (gi, 0, j)),
                     pl.BlockSpec((1, tc), lambda gi, j: (0, j)),
                     pl.BlockSpec((1, mo, tc), lambda gi, j: (gi, 0, j))]
    return pl.pallas_call(
        functools.partial(_leftmm_kernel, cmul=spectrum is not None, gate=gate is not None),
        grid=(g, cols // tc),
        in_specs=in_specs,
        out_specs=pl.BlockSpec((1, mo, tc), lambda gi, j: (gi, 0, j)),
        out_shape=jax.ShapeDtypeStruct((g, mo, cols), out_dtype),
        compiler_params=_cparams("parallel", "parallel"),
        name="dft_leftmm",
    )(*args)


def _blockfft_kernel(*refs, n2, blocks, conv):
    if conv:
        a_ref, tw_ref, ks_ref, m2_ref, m2c_ref, o_ref = refs
    else:
        a_ref, tw_ref, m2_ref, o_ref = refs

    def body(j, carry):
        rows = pl.ds(pl.multiple_of(j * n2, n2), n2)
        ar, ai = a_ref[0, 0, rows, :], a_ref[0, 1, rows, :]
        tcs, tsn = tw_ref[0, rows, :], tw_ref[1, rows, :]
        tcs = jnp.concatenate([tcs] * (ar.shape[1] // tcs.shape[1]), axis=1)
        tsn = jnp.concatenate([tsn] * (ar.shape[1] // tsn.shape[1]), axis=1)
        bstk = jnp.concatenate([ar * tcs + ai * tsn, ai * tcs - ar * tsn], axis=0)
        xs = _dot3(m2_ref, bstk)
        if not conv:
            o_ref[0, 0, rows, :] = xs[:n2]
            o_ref[0, 1, rows, :] = xs[n2:]
            return carry
        xr, xi = xs[:n2], xs[n2:]
        kr, ki = ks_ref[0, rows, :], ks_ref[1, rows, :]
        ystk = jnp.concatenate([xr * kr - xi * ki, xr * ki + xi * kr], axis=0)
        cs = _dot3(m2c_ref, ystk)
        cr, ci = cs[:n2], cs[n2:]
        o_ref[0, 0, rows, :] = cr * tcs - ci * tsn
        o_ref[0, 1, rows, :] = cr * tsn + ci * tcs
        return carry

    lax.fori_loop(0, blocks, body, 0)


def _blockfft(a, tw, m2, spectrum=None, m2c=None):
    g, _, n, c = a.shape
    n2 = m2.shape[1] // 2
    blocks = min(8, n // n2)
    rows = blocks * n2
    conv = spectrum is not None
    args = [a, tw]
    in_specs = [pl.BlockSpec((1, 2, rows, c), lambda gi, j: (gi, 0, j, 0)),
                pl.BlockSpec((2, rows, LANE), lambda gi, j: (0, j, 0))]
    if conv:
        args.append(spectrum)
        in_specs.append(pl.BlockSpec((2, rows, c), lambda gi, j: (0, j, 0)))
    args.append(m2)
    in_specs.append(pl.BlockSpec(m2.shape, lambda gi, j: (0, 0, 0)))
    if conv:
        args.append(m2c)
        in_specs.append(pl.BlockSpec(m2c.shape, lambda gi, j: (0, 0, 0)))
    return pl.pallas_call(
        functools.partial(_blockfft_kernel, n2=n2, blocks=blocks, conv=conv),
        grid=(g, n // rows),
        in_specs=in_specs,
        out_specs=pl.BlockSpec((1, 2, rows, c), lambda gi, j: (gi, 0, j, 0)),
        out_shape=jax.ShapeDtypeStruct((g, 2, n, c), F32),
        compiler_params=_cparams("parallel", "parallel"),
        name="dft_block",
    )(*args)


def _filter_kernel(z_ref, w1_ref, b1_ref, w2_ref, b2_ref, w3_ref, fr_ref, dl_ref, o_ref, *, seq_len):
    i = pl.program_id(0)
    rows = z_ref.shape[0]
    z = z_ref[...]
    fr = fr_ref[...]
    h = jnp.sin(fr * (jnp.dot(z, w1_ref[...], precision=HIGHEST, preferred_element_type=F32) + b1_ref[...]))
    h = jnp.sin(fr * (jnp.dot(h, w2_ref[...], precision=HIGHEST, preferred_element_type=F32) + b2_ref[...]))
    hf = jnp.dot(h, w3_ref[...], precision=HIGHEST, preferred_element_type=F32)
    decay = jnp.exp(-z[:, 0:1] * dl_ref[...])
    j = i * rows + lax.broadcasted_iota(jnp.int32, (rows, 1), 0)
    taps = jnp.where(j < seq_len, hf[:, :HY_WIDTH], jnp.where(j > seq_len, hf[:, HY_WIDTH:], 0.0))
    o_ref[...] = taps * decay


def _hyena_filter(seq_len, wts):
    n = 2 * seq_len
    j = jnp.arange(n)
    lag = jnp.clip(jnp.where(j < seq_len, j, n - j), 0, seq_len - 1).astype(F32)[:, None]
    t = lag / (seq_len - 1)
    wpos = (2.0 * math.pi / seq_len) * lag
    f = jnp.linspace(1e-4, HY_BANDS - 1, HY_BANDS, dtype=F32)[None, :]
    feat = jnp.concatenate([t, jnp.cos(f * wpos), -jnp.sin(f * wpos)], axis=-1)
    feat2 = jnp.pad(feat, ((0, 0), (0, LANE - HY_EMB)))
    rows = min(n, 2048)
    full = lambda shape: pl.BlockSpec(shape, lambda i: (0,) * len(shape))
    return pl.pallas_call(
        functools.partial(_filter_kernel, seq_len=seq_len),
        grid=(n // rows,),
        in_specs=[pl.BlockSpec((rows, LANE), lambda i: (i, 0)),
                  full((LANE, LANE)), full((1, LANE)), full((LANE, LANE)), full((1, LANE)),
                  full((LANE, 2 * HY_WIDTH)), full((1, LANE)), full((1, HY_WIDTH))],
        out_specs=pl.BlockSpec((rows, HY_WIDTH), lambda i: (i, 0)),
        out_shape=jax.ShapeDtypeStruct((n, HY_WIDTH), F32),
        compiler_params=_cparams("parallel"),
        name="hyena_filter",
    )(feat2, wts["f_w1"], wts["f_b1"], wts["f_w2"], wts["f_b2"], wts["f_w3"], wts["f_freq"], wts["f_delta"])


def _cs(p, rows, cols):
    ang = 2.0 * np.pi * np.outer(np.arange(rows), np.arange(cols)) / p
    return np.cos(ang), np.sin(ang)


def _mat(blocks):
    return jnp.asarray(np.block(blocks), dtype=F32)


def _dftcols_kernel(*refs, gate):
    m_ref, x_ref = refs[0], refs[1]
    x = x_ref[0]
    k, nt, _ = x.shape
    xt = pltpu.einshape("knc->(nk)c", x)
    ys = [_dot3(m_ref, xt[n * k:(n + 1) * k]) for n in range(nt)]
    y = pltpu.einshape("(nk)c->knc", jnp.concatenate(ys, axis=0), n=nt)
    if gate:
        z_ref, bias_ref, x0_ref, o_ref = refs[2:]
        y = (y + z_ref[0] * bias_ref[...]) * x0_ref[0]
    else:
        o_ref = refs[2]
    o_ref[0] = y.astype(o_ref.dtype)


def _dftcols(m, x, gate=None, out_dtype=F32):
    g, k, n2, c = x.shape
    mo = m.shape[1]
    nt = 16 if n2 % 16 == 0 else 8
    blk = lambda rows: pl.BlockSpec((1, rows, nt, c), lambda gi, j: (gi, 0, j, 0))
    args = [m, x]
    in_specs = [pl.BlockSpec((2, mo, k), lambda gi, j: (0, 0, 0)), blk(k)]
    if gate is not None:
        z, bias, x0 = gate
        args += [z, bias, x0]
        in_specs += [blk(mo), pl.BlockSpec((1, c), lambda gi, j: (0, 0)), blk(mo)]
    return pl.pallas_call(
        functools.partial(_dftcols_kernel, gate=gate is not None),
        grid=(g, n2 // nt),
        in_specs=in_specs,
        out_specs=blk(mo),
        out_shape=jax.ShapeDtypeStruct((g, mo, n2, c), out_dtype),
        compiler_params=_cparams("parallel", "parallel"),
        name="dft_cols",
    )(*args)


def _hyena_long(z, x0, bias, filt):
    b, s, c = z.shape
    n = 2 * s
    n1 = 1 << ((n.bit_length() - 1 + 1) // 2)
    n2 = n // n1
    g = b // 2
    c1, s1 = _cs(n1, n1, n1)
    h1 = n1 // 2
    m_fwd = _split_const(_mat([[c1[:, :h1], s1[:, :h1]], [-s1[:, :h1], c1[:, :h1]]]))
    m_filt = _split_const(_mat([[c1], [-s1]]))
    m_inv = _split_const(_mat([[c1[:h1], -s1[:h1]], [s1[:h1], c1[:h1]]]) * (1.0 / n))
    c2, s2 = _cs(n2, n2, n2)
    m2 = _split_const(_mat([[c2, s2], [-s2, c2]]))
    m2c = _split_const(_mat([[c2, -s2], [s2, c2]]))
    k1 = jnp.arange(n1, dtype=jnp.int32)[:, None]
    nn2 = jnp.arange(n2, dtype=jnp.int32)[None, :]
    ang = (2.0 * math.pi / n) * (k1 * nn2).astype(F32).reshape(n)
    tw = jnp.broadcast_to(jnp.stack([jnp.cos(ang), jnp.sin(ang)])[:, :, None], (2, n, LANE))

    spec = _dftcols(m_filt, filt.reshape(1, n1, n2, c)).reshape(1, 2, n, c)
    spec = _blockfft(spec, tw, m2)[0]
    zv = z.reshape(g, n1, n2, c)
    a = _dftcols(m_fwd, zv).reshape(g, 2, n, c)
    dd = _blockfft(a, tw, m2, spectrum=spec, m2c=m2c).reshape(g, 2 * n1, n2, c)
    y = _dftcols(m_inv, dd, gate=(zv, bias, x0.reshape(g, n1, n2, c)), out_dtype=BF16)
    return y.reshape(b, s, c)


def _hyena_short(z, x0, bias, filt):
    b, s, c = z.shape
    n = 2 * s
    g = b // 2
    cf, sf = _cs(n, n, n)
    m_fwd = _mat([[cf[:, :s], sf[:, :s]], [-sf[:, :s], cf[:, :s]]])
    m_filt = _mat([[cf], [-sf]])
    m_inv = _mat([[cf[:s], -sf[:s]], [sf[:s], cf[:s]]]) * (1.0 / n)
    spec = _leftmm(m_filt, filt.reshape(1, n, c))[0]
    zv = z.reshape(g, n, c)
    zf = _leftmm(m_fwd, zv)
    y = _leftmm(m_inv, zf, spectrum=spec, gate=(zv, bias, x0.reshape(g, n, c)), out_dtype=BF16)
    return y.reshape(b, s, c)


def _attn_kernel(*refs, tk, n_chunks, heads, unroll):
    if n_chunks:
        q_ref, kc_ref, vc_ref, kl_ref, vl_ref, o_ref, s_ref = refs
    else:
        q_ref, kc_ref, vc_ref, o_ref = refs
    dn = (((1,), (1,)), ((), ()))
    qs = [q_ref[0, hh] for hh in range(heads)]

    def chunk_rows(j):
        return pl.ds(pl.multiple_of(j * tk, tk), tk)

    def scores(j, slot):
        for hh in range(heads):
            s_ref[slot, hh] = lax.dot_general(qs[hh], kl_ref[0, hh, chunk_rows(j), :], dn,
                                              preferred_element_type=F32)

    def absorb(j, slot, carry):
        out = []
        for hh in range(heads):
            m_old, acc_old = carry[hh]
            sj = s_ref[slot, hh]
            m_new = jnp.maximum(m_old, jnp.max(sj, axis=-1, keepdims=True))
            pj = jnp.exp2(sj - m_new)
            acc_new = jnp.exp2(m_old - m_new) * acc_old + jnp.dot(pj.astype(BF16), vl_ref[0, hh, chunk_rows(j), :],
                                                                  preferred_element_type=F32)
            out.append((m_new, acc_new))
        return tuple(out)

    if n_chunks:
        scores(0, 0)
    state = []
    for hh in range(heads):
        s = lax.dot_general(qs[hh], kc_ref[0, hh], dn, preferred_element_type=F32)
        m = jnp.max(s, axis=-1, keepdims=True)
        p = jnp.exp2(s - m)
        state.append((m, jnp.dot(p.astype(BF16), vc_ref[0, hh], preferred_element_type=F32)))

    def body(jj, carry):
        j = unroll * jj
        for u in range(unroll):
            nxt = j + u + 1
            if u == unroll - 1:
                nxt = jnp.minimum(nxt, n_chunks - 1)
            scores(nxt, (u + 1) % 2)
            carry = absorb(j + u, u % 2, carry)
        return carry

    if n_chunks:
        state = lax.fori_loop(0, n_chunks // unroll, body, tuple(state))
    outs = [acc[:, :V_DIM] / acc[:, V_DIM:V_DIM + 1] for _, acc in state]
    o_ref[0] = jnp.concatenate(outs, axis=-1).astype(o_ref.dtype)


def _attention(q, kc, vc, kl=None, vl=None):
    b, h, s, dp = q.shape
    nc = kc.shape[2]
    tq = min(ROW_TILE, s)
    heads = LANE // V_DIM
    tk = 512
    n_chunks = 0 if kl is None else kl.shape[2] // tk
    unroll = 8 if n_chunks % 8 == 0 else 2
    args = [q, kc, vc]
    in_specs = [pl.BlockSpec((1, heads, tq, dp), lambda bi, hi, i: (bi, hi, i, 0)),
                pl.BlockSpec((1, heads, nc, dp), lambda bi, hi, i: (bi, hi, 0, 0)),
                pl.BlockSpec((1, heads, nc, dp), lambda bi, hi, i: (bi, hi, 0, 0))]
    scratch = []
    if n_chunks:
        assert unroll % 2 == 0 and n_chunks % unroll == 0
        sl = kl.shape[2]
        args += [kl, vl]
        in_specs += [pl.BlockSpec((1, heads, sl, dp), lambda bi, hi, i: (bi, hi, 0, 0))] * 2
        scratch = [pltpu.VMEM((2, heads, tq, tk), F32)]
    return pl.pallas_call(
        functools.partial(_attn_kernel, tk=tk, n_chunks=n_chunks, heads=heads, unroll=unroll),
        grid=(b, h // heads, s // tq),
        in_specs=in_specs,
        out_specs=pl.BlockSpec((1, tq, heads * V_DIM), lambda bi, hi, i: (bi, i, hi)),
        out_shape=jax.ShapeDtypeStruct((b, s, h * V_DIM), BF16),
        scratch_shapes=scratch,
        compiler_params=_cparams("parallel", "parallel", "parallel"),
        name="attention",
    )(*args)


def _outproj_kernel(x_ref, pool_ref, hy_ref, at_ref, wout_ref, g1_ref, gn_ref, sc_ref, sh_ref, rw_ref, rb_ref,
                    x1_ref, h2_ref, lg_ref):
    mix = jnp.concatenate([pool_ref[0], hy_ref[0], at_ref[0]], axis=-1)
    y = jnp.dot(mix, wout_ref[...], preferred_element_type=F32)
    x1 = x_ref[0] + g1_ref[0] * y
    x1_ref[0] = x1
    h2 = _rms(x1, gn_ref[...], D_MODEL) * (1.0 + sc_ref[0]) + sh_ref[0]
    h2_ref[0] = _pack_rows(h2)
    lg_ref[0] = lax.dot_general(rw_ref[...], h2, (((1,), (1,)), ((), ())), precision=HIGHEST,
                                preferred_element_type=F32) + rb_ref[...]


def _outproj(x, pool, hy, attn, seg_of_batch, wts, mod):
    b, s, d = x.shape
    t = min(ROW_TILE, s)
    full = lambda shape: pl.BlockSpec(shape, lambda bi, i: (0,) * len(shape))
    seg = lambda: pl.BlockSpec((1, 1, d), lambda bi, i: (seg_of_batch(bi), 0, 0))
    rowblk = lambda w: pl.BlockSpec((1, t, w), lambda bi, i: (bi, i, 0))
    return pl.pallas_call(
        _outproj_kernel,
        grid=(b, s // t),
        in_specs=[rowblk(d), rowblk(POOL_WIDTH), rowblk(HY_WIDTH), rowblk(MLA_WIDTH),
                  full((d, d)), seg(), full((1, d)), seg(), seg(), full((N_EXPERTS, d)), full((N_EXPERTS, 1))],
        out_specs=[rowblk(d), rowblk(d // 2), pl.BlockSpec((1, N_EXPERTS, t), lambda bi, i: (bi, 0, i))],
        out_shape=[jax.ShapeDtypeStruct((b, s, d), F32), jax.ShapeDtypeStruct((b, s, d // 2), jnp.int32),
                   jax.ShapeDtypeStruct((b, N_EXPERTS, s), F32)],
        compiler_params=_cparams("parallel", "parallel"),
        name="outproj",
    )(x, pool, hy, attn, wts["w_out"], mod["g1"], wts["norm2_g"], mod["sc2"], mod["sh2"],
      wts["router_w"], wts["router_b"])


def _moe_kernel(be_ref, br_ref, x_ref, w1_ref, b1_ref, w2_ref, b2_ref, o_ref, w1b_ref, w2b_ref):
    i = pl.program_id(0)
    e = be_ref[i]
    e_prev = be_ref[jnp.maximum(i - 1, 0)]
    n_rows = br_ref[i]

    @pl.when((i == 0) | (e != e_prev))
    def _():
        w1b_ref[...] = w1_ref[0, 0].astype(BF16)
        w2b_ref[...] = w2_ref[0, 0].astype(BF16)

    @pl.when(n_rows > 0)
    def _():
        live = lax.broadcasted_iota(jnp.int32, (x_ref.shape[0], 1), 0) < n_rows
        x = _unpack_rows(jnp.where(live, x_ref[...], 0))
        a = jnp.dot(x, w1b_ref[...], preferred_element_type=F32) + b1_ref[0, 0]
        glu = jnp.minimum(a[:, :D_EXPERT], SWIGLU_LIMIT)
        lin = jnp.clip(a[:, D_EXPERT:], -SWIGLU_LIMIT, SWIGLU_LIMIT)
        act = glu * (1.0 / (1.0 + jnp.exp(-SWIGLU_ALPHA * glu))) * (lin + 1.0)
        y = jnp.dot(act.astype(BF16), w2b_ref[...], preferred_element_type=F32) + b2_ref[0, 0]
        o_ref[...] = _pack_rows(y)

    @pl.when(n_rows == 0)
    def _():
        o_ref[...] = jnp.zeros_like(o_ref)


def _experts(xs, block_e, block_rows, w1, b1, w2, b2, layer):
    n_buf, dh = xs.shape
    d = 2 * dh
    tm = EXPERT_TILE
    n_blocks = n_buf // tm
    de2 = w1.shape[3]
    grid_spec = pltpu.PrefetchScalarGridSpec(
        num_scalar_prefetch=2,
        grid=(n_blocks,),
        in_specs=[pl.BlockSpec((tm, dh), lambda i, be, nu: (i, 0)),
                  pl.BlockSpec((1, 1, d, de2), lambda i, be, nu: (layer, be[i], 0, 0)),
                  pl.BlockSpec((1, 1, 1, de2), lambda i, be, nu: (layer, be[i], 0, 0)),
                  pl.BlockSpec((1, 1, D_EXPERT, d), lambda i, be, nu: (layer, be[i], 0, 0)),
                  pl.BlockSpec((1, 1, 1, d), lambda i, be, nu: (layer, be[i], 0, 0))],
        out_specs=pl.BlockSpec((tm, dh), lambda i, be, nu: (i, 0)),
        scratch_shapes=[pltpu.VMEM((d, de2), BF16), pltpu.VMEM((D_EXPERT, d), BF16)],
    )
    return pl.pallas_call(
        _moe_kernel,
        grid_spec=grid_spec,
        out_shape=jax.ShapeDtypeStruct((n_buf, dh), jnp.int32),
        compiler_params=_cparams("arbitrary"),
        name="experts",
    )(block_e, block_rows, xs, w1, b1, w2, b2)


def _router_kernel(lg_ref, idx_ref, gate_ref, rank_ref, cnt_ref, base_ref):
    i = pl.program_id(0)
    n_e, t = lg_ref.shape

    @pl.when(i == 0)
    def _():
        base_ref[...] = jnp.zeros_like(base_ref)

    work = lg_ref[...]
    eid = lax.broadcasted_iota(jnp.int32, (n_e, t), 0)
    vals, idxs, sels = [], [], []
    for _ in range(TOP_K):
        m = jnp.max(work, axis=0, keepdims=True)
        idx = jnp.min(jnp.where(work == m, eid, n_e), axis=0, keepdims=True)
        sel = eid == idx
        work = jnp.where(sel, -jnp.inf, work)
        vals.append(m)
        idxs.append(idx)
        sels.append(sel)
    cnt = jnp.where(sels[0], 1.0, 0.0)
    for sel in sels[1:]:
        cnt = cnt + jnp.where(sel, 1.0, 0.0)
    earlier = (lax.broadcasted_iota(jnp.int32, (t, t), 0) < lax.broadcasted_iota(jnp.int32, (t, t), 1))
    within = jnp.dot(cnt.astype(BF16), jnp.where(earlier, 1.0, 0.0).astype(BF16), preferred_element_type=F32)
    pref = base_ref[:, 0:1] + within
    ranks = [jnp.sum(jnp.where(sel, pref, 0.0), axis=0, keepdims=True) for sel in sels]
    new_base = base_ref[...] + jnp.sum(cnt, axis=1, keepdims=True)
    base_ref[...] = new_base
    cnt_ref[...] = new_base
    ex = [jnp.exp(v - vals[0]) for v in vals]
    den = ex[0]
    for e in ex[1:]:
        den = den + e
    idx_ref[...] = jnp.concatenate(idxs, axis=0)
    gate_ref[...] = jnp.concatenate([e / den for e in ex], axis=0)
    rank_ref[...] = jnp.concatenate(ranks, axis=0).astype(jnp.int32)


def _route(logits_t):
    n_e, n_tok = logits_t.shape
    tm = EXPERT_TILE
    t = ROW_TILE
    tokblk = lambda: pl.BlockSpec((TOP_K, t), lambda i: (0, i))
    idx_t, gates_t, rank_t, counts_f = pl.pallas_call(
        _router_kernel,
        grid=(n_tok // t,),
        in_specs=[pl.BlockSpec((n_e, t), lambda i: (0, i))],
        out_specs=[tokblk(), tokblk(), tokblk(), pl.BlockSpec((n_e, LANE), lambda i: (0, 0))],
        out_shape=[jax.ShapeDtypeStruct((TOP_K, n_tok), jnp.int32), jax.ShapeDtypeStruct((TOP_K, n_tok), F32),
                   jax.ShapeDtypeStruct((TOP_K, n_tok), jnp.int32), jax.ShapeDtypeStruct((n_e, LANE), F32)],
        scratch_shapes=[pltpu.VMEM((n_e, LANE), F32)],
        compiler_params=_cparams("arbitrary"),
        name="router",
    )(logits_t)
    n_slots = n_tok * TOP_K
    counts = counts_f[:, 0].astype(jnp.int32)
    padded = (counts + tm - 1) // tm * tm
    padded_end = jnp.cumsum(padded)
    padded_start = padded_end - padded
    experts = jnp.arange(n_e, dtype=jnp.int32)
    dest_t = rank_t + jnp.sum(jnp.where(idx_t[None] == experts[:, None, None], padded_start[:, None, None], 0), axis=0)
    n_blocks = n_slots // tm + n_e
    block_start = jnp.arange(n_blocks, dtype=jnp.int32) * tm
    block_e = jnp.minimum(jnp.sum(padded_end[None, :] <= block_start[:, None], axis=1), n_e - 1).astype(jnp.int32)
    sel = (block_e[:, None] == experts[None, :])
    pick = lambda v: jnp.sum(jnp.where(sel, v[None, :], 0), axis=1)
    block_rows = jnp.clip(pick(padded_start) + pick(counts) - block_start, 0, tm).astype(jnp.int32)
    return gates_t.T, dest_t, block_e, block_rows


SC_WINDOW = 32
SC_SLOTS = 4


def _sc_gather(table, idx):
    _, d = table.shape
    b = idx.shape[0]
    info = plsc.get_sparse_core_info()
    nc, ns = info.num_cores, info.num_subcores
    nw = nc * ns
    w, nslot = SC_WINDOW, SC_SLOTS
    per_w = b // nw
    n_win = per_w // w
    assert per_w * nw == b and n_win * w == per_w and n_win % nslot == 0
    mesh = plsc.VectorSubcoreMesh(core_axis_name="c", subcore_axis_name="s")

    def body(table_hbm, idx_hbm, out_hbm, idx_v, rows_v, gsem, osem):
        wid = lax.axis_index("s") * nc + lax.axis_index("c")
        base = wid * per_w
        pltpu.sync_copy(idx_hbm.at[wid], idx_v)

        def gather(g, slot):
            return pltpu.make_async_copy(table_hbm.at[idx_v.at[g]], rows_v.at[slot], gsem.at[slot])

        def put(g, slot):
            return pltpu.make_async_copy(rows_v.at[slot], out_hbm.at[pl.ds(base + g * w, w)], osem.at[slot])

        for g in range(nslot - 1):
            gather(g, g).start()

        @pl.loop(0, n_win, step=nslot)
        def _(g0):
            for slot in range(nslot):
                g = g0 + slot
                gather(g, slot).wait()
                put(g, slot).start()
                free = (slot - 1) % nslot

                @pl.when(g + nslot - 1 < n_win)
                def _():
                    @pl.when(g >= 1)
                    def _():
                        put(g - 1, free).wait()

                    gather(g + nslot - 1, free).start()

        for g in range(n_win - nslot, n_win):
            put(g, g % nslot).wait()

    return pl.kernel(
        body,
        out_type=jax.ShapeDtypeStruct((b, d), table.dtype),
        mesh=mesh,
        scratch_types=[pltpu.VMEM((n_win, w), jnp.int32), pltpu.VMEM((nslot, w, d), table.dtype),
                       pltpu.SemaphoreType.DMA((nslot,)), pltpu.SemaphoreType.DMA((nslot,))],
        name="sc_gather",
    )(table, idx.reshape(nw, n_win, w))


SC_SCATTER_WINDOW = 16


def _sc_scatter(rows, dest_t, n_out):
    n_tok, d = rows.shape
    k = dest_t.shape[0]
    info = plsc.get_sparse_core_info()
    nc, ns = info.num_cores, info.num_subcores
    nw = nc * ns
    w = SC_SCATTER_WINDOW
    per_w = n_tok // nw
    n_win = per_w // w
    assert per_w * nw == n_tok and n_win * w == per_w and n_win % 2 == 0
    mesh = plsc.VectorSubcoreMesh(core_axis_name="c", subcore_axis_name="s")
    idx = dest_t.reshape(k, nw, n_win, w).transpose(1, 0, 2, 3)

    def body(rows_hbm, idx_hbm, out_hbm, idx_v, rows_v, lsem, ssem):
        wid = lax.axis_index("s") * nc + lax.axis_index("c")
        base = wid * per_w
        pltpu.sync_copy(idx_hbm.at[wid], idx_v)

        def load(g, slot):
            return pltpu.make_async_copy(rows_hbm.at[pl.ds(base + g * w, w)], rows_v.at[slot], lsem.at[slot])

        def scatter(g, slot, j):
            return pltpu.make_async_copy(rows_v.at[slot], out_hbm.at[idx_v.at[j, g]], ssem.at[slot])

        load(0, 0).start()

        @pl.loop(0, n_win, step=2)
        def _(g0):
            for slot in range(2):
                g = g0 + slot
                load(g, slot).wait()
                for j in range(k):
                    scatter(g, slot, j).start()

                @pl.when(g + 1 < n_win)
                def _():
                    @pl.when(g >= 1)
                    def _():
                        for j in range(k):
                            scatter(g - 1, 1 - slot, j).wait()

                    load(g + 1, 1 - slot).start()

        for g in (n_win - 2, n_win - 1):
            for j in range(k):
                scatter(g, g % 2, j).wait()

    return pl.kernel(
        body,
        out_type=jax.ShapeDtypeStruct((n_out, d), rows.dtype),
        mesh=mesh,
        scratch_types=[pltpu.VMEM((k, n_win, w), jnp.int32), pltpu.VMEM((2, w, d), rows.dtype),
                       pltpu.SemaphoreType.DMA((2,)), pltpu.SemaphoreType.DMA((2,))],
        name="sc_scatter",
    )(rows, idx)


def _combine_kernel(x_ref, y_ref, gt_ref, g2_ref, o_ref):
    gt = gt_ref[...]
    acc = gt[:, 0:1] * _unpack_rows(y_ref[0]).astype(F32)
    for j in range(1, TOP_K):
        acc = acc + gt[:, j:j + 1] * _unpack_rows(y_ref[j]).astype(F32)
    o_ref[0] = x_ref[0] + g2_ref[0] * acc


def _combine(x, ysel, gates, tok_offset, seg_of_batch, mod):
    b, s, d = x.shape
    t = min(ROW_TILE, s)
    per = s // t
    off = tok_offset // t
    return pl.pallas_call(
        _combine_kernel,
        grid=(b, per),
        in_specs=[pl.BlockSpec((1, t, d), lambda bi, i: (bi, i, 0)),
                  pl.BlockSpec((TOP_K, t, d // 2), lambda bi, i: (0, off + bi * per + i, 0)),
                  pl.BlockSpec((t, TOP_K), lambda bi, i: (off + bi * per + i, 0)),
                  pl.BlockSpec((1, 1, d), lambda bi, i: (seg_of_batch(bi), 0, 0))],
        out_specs=pl.BlockSpec((1, t, d), lambda bi, i: (bi, i, 0)),
        out_shape=jax.ShapeDtypeStruct((b, s, d), F32),
        compiler_params=_cparams("parallel", "parallel"),
        name="combine",
    )(x, ysel, gates, mod["g2"])


def _rope_tables(seq_len):
    n_rows = seq_len // GRID_W
    row = jnp.repeat(jnp.arange(n_rows, dtype=F32), GRID_W)
    col = jnp.tile(jnp.arange(GRID_W, dtype=F32), n_rows)
    pairs = ROPE // 4
    inv = ROPE_BASE ** (-jnp.arange(pairs, dtype=F32) / pairs)
    ang = jnp.concatenate([row[:, None] * inv, col[:, None] * inv], axis=-1)
    cos, sin = jnp.cos(ang), jnp.sin(ang)
    half = ROPE // 2
    one = jnp.ones((seq_len, NOPE), F32)
    zn = jnp.zeros((seq_len, NOPE), F32)
    zh = jnp.zeros((seq_len, half), F32)
    zp = jnp.zeros((seq_len, HEAD_PAD - QK_DIM), F32)
    return (jnp.concatenate([one, cos, cos, zp], axis=-1),
            jnp.concatenate([zn, -sin, zh, zp], axis=-1),
            jnp.concatenate([zn, zh, sin, zp], axis=-1))


def _identity_rope(seq_len):
    keep = jnp.concatenate([jnp.ones((seq_len, QK_DIM), F32), jnp.zeros((seq_len, HEAD_PAD - QK_DIM), F32)], axis=-1)
    zero = jnp.zeros((seq_len, HEAD_PAD), F32)
    return keep, zero, zero


def _pad_cols(w, n):
    return jnp.pad(w, ((0, 0), (0, n - w.shape[1])))


def _layer_weights(l, p):
    hw = HEADS * HEAD_PAD
    wq = jnp.pad(p["mla_w_uq"][l].reshape(Q_RANK, HEADS, QK_DIM), ((0, 0), (0, 0), (0, HEAD_PAD - QK_DIM)))
    wukv = p["mla_w_ukv"][l].reshape(KV_RANK, HEADS, NOPE + V_DIM)
    wk = jnp.pad(wukv[:, :, :NOPE], ((0, 0), (0, 0), (0, HEAD_PAD - NOPE))).reshape(KV_RANK, hw)
    wv = jnp.pad(wukv[:, :, NOPE:], ((0, 0), (0, 0), (0, HEAD_PAD - V_DIM))).reshape(KV_RANK, hw)
    place = jnp.pad(jnp.eye(ROPE, dtype=F32), ((0, KV_RANK - ROPE), (NOPE, HEAD_PAD - QK_DIM)))
    wk_rope = jnp.tile(place, (1, HEADS))
    wkv = jnp.concatenate([jnp.concatenate([wk, wv], axis=1),
                           jnp.concatenate([wk_rope, jnp.zeros((KV_RANK, hw), F32)], axis=1)], axis=0)
    wpool = jax.scipy.linalg.block_diag(*[p["pool_w"][l, g] for g in range(POOL_GROUPS)])
    vone = jnp.zeros((1, HEAD_PAD), F32).at[0, V_DIM].set(1.0)
    deltas = jnp.abs(jnp.linspace(HY_MIN_DECAY, HY_MAX_DECAY, HY_WIDTH, dtype=F32))[None, :]
    row = lambda v, n=None: (v if n is None else jnp.pad(v, (0, n - v.shape[0])))[None, :]
    return {
        "norm1_g": row(p["norm1_g"][l]), "norm2_g": row(p["norm2_g"][l]),
        "w_in": _pad_cols(p["w_in"][l], N_IN_PAD).astype(BF16),
        "gq": row(p["mla_q_norm_g"][l]), "wq": wq.reshape(Q_RANK, hw).astype(BF16),
        "gkv": row(p["mla_kv_norm_g"][l]), "wkv": wkv.astype(BF16),
        "gqq": row(p["qk_norm_q"][l], HEAD_PAD), "gqk": row(p["qk_norm_k"][l], HEAD_PAD), "vone": vone,
        "wpool": wpool.astype(BF16), "pool_scale": row(p["pool_scale"][l]),
        "conv_w": p["hy_conv_w"][l], "conv_b": row(p["hy_conv_b"][l]),
        "f_w1": jnp.pad(p["hy_f_w1"][l], ((0, LANE - HY_EMB), (0, LANE - HY_HID))),
        "f_b1": row(p["hy_f_b1"][l], LANE),
        "f_w2": jnp.pad(p["hy_f_w2"][l], ((0, LANE - HY_HID), (0, LANE - HY_HID))),
        "f_b2": row(p["hy_f_b2"][l], LANE),
        "f_w3": jnp.pad(p["hy_f_w3"][l], ((0, LANE - HY_HID), (0, 0))),
        "f_freq": row(p["hy_freq"][l], LANE), "f_delta": deltas,
        "hy_bias": row(p["hy_bias"][l]),
        "w_out": p["w_out"][l].astype(BF16),
        "router_w": p["router_w"][l].T, "router_b": p["router_b"][l][:, None],
    }


def _layer(l, x, xc, cc, wts, p, rope_main, rope_ctx, last):
    b, s, d = x.shape
    n_ctx = xc.shape[1]
    modm = _modulation(cc, p["w_mod"], p["b_mod"][l][None, :], l)
    names = ("sh1", "sc1", "g1", "sh2", "sc2", "g2")
    mod = {nm: modm[:, i * d:(i + 1) * d].reshape(N_SEG, 1, d) for i, nm in enumerate(names)}
    main_seg = lambda bi: bi
    ctx_seg = lambda bi: CTX_SEG

    pph, q, kl, vl = _inproj(x, main_seg, wts, mod, rope_main)
    pph_c, qc, kc, vc = _inproj(xc, ctx_seg, wts, mod, rope_ctx)
    attn = _attention(q, kc, vc, kl, vl)
    pool, x0, z = _seqmix(pph, wts)
    hy = _hyena_long(z, x0, wts["hy_bias"], _hyena_filter(s, wts))
    x1, h2, logits = _outproj(x, pool, hy, attn, main_seg, wts, mod)
    h2 = h2.reshape(b * s, d // 2)
    logits = jnp.swapaxes(logits, 0, 1).reshape(N_EXPERTS, b * s)
    n_c = 0
    if not last:
        attn_c = _attention(qc, kc, vc)
        pool_c, x0c, zc = _seqmix(pph_c, wts)
        hy_c = _hyena_short(zc, x0c, wts["hy_bias"], _hyena_filter(n_ctx, wts))
        xc1, h2c, logits_c = _outproj(xc, pool_c, hy_c, attn_c, ctx_seg, wts, mod)
        n_c = b * n_ctx
        h2 = jnp.concatenate([h2c.reshape(n_c, d // 2), h2], axis=0)
        logits = jnp.concatenate([jnp.swapaxes(logits_c, 0, 1).reshape(N_EXPERTS, n_c), logits], axis=1)

    gates, dest_t, block_e, block_rows = _route(logits)
    de2 = p["moe_w1"].shape[-1]
    xs = _sc_scatter(h2, dest_t, block_e.shape[0] * EXPERT_TILE)
    ys = _experts(xs, block_e, block_rows, p["moe_w1"], p["moe_b1"].reshape(-1, N_EXPERTS, 1, de2),
                  p["moe_w2"], p["moe_b2"].reshape(-1, N_EXPERTS, 1, d), l)
    ysel = _sc_gather(ys, dest_t.reshape(-1)).reshape(TOP_K, -1, d // 2)
    x2 = _combine(x1, ysel, gates, n_c, main_seg, mod)
    if not last:
        xc = _combine(xc1, ysel, gates, 0, ctx_seg, mod)
    return x2, xc


def kernel(x, c, ctx, c_ctx, norm1_g, norm2_g, w_mod, b_mod, w_in, pool_w, pool_scale, hy_conv_w, hy_conv_b,
           hy_f_w1, hy_f_b1, hy_f_w2, hy_f_b2, hy_f_w3, hy_freq, hy_bias, mla_q_norm_g, mla_w_uq, mla_kv_norm_g,
           mla_w_ukv, qk_norm_q, qk_norm_k, w_out, router_w, router_b, moe_w1, moe_b1, moe_w2, moe_b2):
    p = dict(norm1_g=norm1_g, norm2_g=norm2_g, w_mod=w_mod, b_mod=b_mod, w_in=w_in, pool_w=pool_w,
             pool_scale=pool_scale,
             hy_conv_w=hy_conv_w, hy_conv_b=hy_conv_b, hy_f_w1=hy_f_w1, hy_f_b1=hy_f_b1, hy_f_w2=hy_f_w2,
             hy_f_b2=hy_f_b2, hy_f_w3=hy_f_w3, hy_freq=hy_freq, hy_bias=hy_bias, mla_q_norm_g=mla_q_norm_g,
             mla_w_uq=mla_w_uq, mla_kv_norm_g=mla_kv_norm_g, mla_w_ukv=mla_w_ukv, qk_norm_q=qk_norm_q,
             qk_norm_k=qk_norm_k, w_out=w_out, router_w=router_w, router_b=router_b, moe_w1=moe_w1,
             moe_b1=moe_b1, moe_w2=moe_w2, moe_b2=moe_b2)
    b, s, d = x.shape
    depth = w_mod.shape[0]
    cc = jnp.zeros((N_SEG, d), F32).at[:b].set(c).at[CTX_SEG].set(c_ctx)
    rope_main = _rope_tables(s)
    rope_ctx = _identity_rope(ctx.shape[1])
    xc = ctx
    for l in range(depth):
        wts = _layer_weights(l, p)
        x, xc = _layer(l, x, xc, cc, wts, p, rope_main, rope_ctx, l == depth - 1)
    return x
```

```python
import functools
import math

import jax
import jax.numpy as jnp
import numpy as np
from jax import lax
from jax.experimental import pallas as pl
from jax.experimental.pallas import tpu as pltpu
from jax.experimental.pallas import tpu_sc as plsc

F32 = jnp.float32
BF16 = jnp.bfloat16
HIGHEST = lax.Precision.HIGHEST

D_MODEL = 1024
GRID_W = 64
EPS = 1e-6
POOL_WIDTH = 256
POOL_GROUPS = 4
POOL_GROUP_DIM = POOL_WIDTH // POOL_GROUPS
HY_WIDTH = 256
HY_IN = 3 * HY_WIDTH
HY_BANDS = 8
HY_EMB = 1 + 2 * HY_BANDS
HY_HID = 64
HY_MIN_DECAY = math.log(1e-2) / 1.5
HY_MAX_DECAY = math.log(1e-2) / 0.3
HEADS = 8
Q_RANK = 256
KV_RANK = 128
NOPE = 64
ROPE = 32
V_DIM = 64
QK_DIM = NOPE + ROPE
MLA_WIDTH = HEADS * V_DIM
ATTN_SCALE = QK_DIM ** -0.5
LOG2E = math.log2(math.e)
OFF_HY = POOL_WIDTH
OFF_Q = OFF_HY + HY_IN
OFF_KV = OFF_Q + Q_RANK
OFF_KR = OFF_KV + KV_RANK
N_IN = OFF_KR + ROPE
N_IN_PAD = OFF_KR + 128
ROPE_BASE = 10000.0
N_EXPERTS = 32
TOP_K = 4
D_EXPERT = 1024
SWIGLU_LIMIT = 7.0
SWIGLU_ALPHA = 1.702

LANE = 128
HEAD_PAD = LANE
ROW_TILE = 512
EXPERT_TILE = 512
VMEM_LIMIT = 56 * 1024 * 1024
N_SEG = 8
CTX_SEG = 4


def _cparams(*sem):
    return pltpu.CompilerParams(dimension_semantics=sem, vmem_limit_bytes=VMEM_LIMIT)


def _rms(x, g, n):
    ss = jnp.sum(x * x, axis=-1, keepdims=True) * (1.0 / n)
    return x * lax.rsqrt(ss + EPS) * g


def _pack_rows(x):
    half = x.shape[1] // 2
    bits = lax.bitcast_convert_type(x.astype(BF16).astype(F32), jnp.int32)
    return bits[:, :half] | lax.shift_right_logical(bits[:, half:], 16)


def _unpack_rows(w):
    hi = lax.bitcast_convert_type(w & jnp.int32(-65536), F32)
    lo = lax.bitcast_convert_type(lax.shift_left(w, 16), F32)
    return jnp.concatenate([hi, lo], axis=1).astype(BF16)


def _mod_kernel(c_ref, w_ref, b_ref, o_ref):
    c = c_ref[...]
    s = c * (1.0 / (1.0 + jnp.exp(-c)))
    o_ref[...] = jnp.dot(s, w_ref[0], precision=HIGHEST, preferred_element_type=F32) + b_ref[...]


def _modulation(cc, w, b, layer):
    _, d, n = w.shape
    tn = 1536
    return pl.pallas_call(
        _mod_kernel,
        grid=(n // tn,),
        in_specs=[pl.BlockSpec((N_SEG, d), lambda j: (0, 0)),
                  pl.BlockSpec((1, d, tn), lambda j: (layer, 0, j)),
                  pl.BlockSpec((1, tn), lambda j: (0, j))],
        out_specs=pl.BlockSpec((N_SEG, tn), lambda j: (0, j)),
        out_shape=jax.ShapeDtypeStruct((N_SEG, n), F32),
        compiler_params=_cparams("arbitrary"),
        name="modulation",
    )(cc, w, b)


def _inproj_kernel(x_ref, gn_ref, sc_ref, sh_ref, win_ref, gq_ref, wq_ref, gkv_ref, wk_ref, wvt_ref,
                   gqq_ref, gqqr_ref, gqk_ref, gqkr_ref, vone_ref, rc_ref, rs_ref,
                   pph_ref, q_ref, k_ref, vt_ref):
    x = x_ref[0]
    h = _rms(x, gn_ref[...], D_MODEL) * (1.0 + sc_ref[0]) + sh_ref[0]
    p = jnp.dot(h.astype(BF16), win_ref[...], preferred_element_type=F32)
    pph_ref[0] = p[:, :OFF_Q]
    cq = _rms(p[:, OFF_Q:OFF_KV], gq_ref[...], Q_RANK)
    qall = jnp.dot(cq.astype(BF16), wq_ref[...], preferred_element_type=F32)
    ckv_b = _rms(p[:, OFF_KV:OFF_KR], gkv_ref[...], KV_RANK).astype(BF16)
    kin = jnp.concatenate([ckv_b, p[:, OFF_KR:N_IN_PAD].astype(BF16)], axis=-1)
    kall = jnp.dot(kin, wk_ref[...], preferred_element_type=F32)
    vt_all = lax.dot_general(wvt_ref[...], ckv_b, (((1,), (1,)), ((), ())), preferred_element_type=F32)
    rc, rs = rc_ref[...], rs_ref[...]
    hw = HEADS * HEAD_PAD

    def norm_rope(x_all, hd, g_ref, gr_ref):
        xh = x_all[:, HEAD_PAD * hd:HEAD_PAD * (hd + 1)]
        xr = x_all[:, hw + HEAD_PAD * hd:hw + HEAD_PAD * (hd + 1)]
        r = lax.rsqrt(jnp.sum(xh * xh, axis=-1, keepdims=True) * (1.0 / QK_DIM) + EPS)
        return (xh * (g_ref[...] * rc) + xr * (gr_ref[...] * rs)) * r

    for hd in range(HEADS):
        q_ref[0, hd] = (norm_rope(qall, hd, gqq_ref, gqqr_ref) * (ATTN_SCALE * LOG2E)).astype(BF16)
        k_ref[0, hd] = norm_rope(kall, hd, gqk_ref, gqkr_ref).astype(BF16)
        vt_ref[0, hd] = (vt_all[HEAD_PAD * hd:HEAD_PAD * (hd + 1)] + vone_ref[...]).astype(BF16)


def _inproj(x, seg_of_batch, wts, mod, rope_tabs):
    b, s, d = x.shape
    t = min(ROW_TILE, s)
    full = lambda shape: pl.BlockSpec(shape, lambda bi, i: (0,) * len(shape))
    seg = lambda: pl.BlockSpec((1, 1, d), lambda bi, i: (seg_of_batch(bi), 0, 0))
    tab = lambda: pl.BlockSpec((t, HEAD_PAD), lambda bi, i: (i, 0))
    hw = HEADS * HEAD_PAD
    head_out = lambda: pl.BlockSpec((1, HEADS, t, HEAD_PAD), lambda bi, i: (bi, 0, i, 0))
    return pl.pallas_call(
        _inproj_kernel,
        grid=(b, s // t),
        in_specs=[pl.BlockSpec((1, t, d), lambda bi, i: (bi, i, 0)),
                  full((1, d)), seg(), seg(),
                  full((d, N_IN_PAD)), full((1, Q_RANK)), full((Q_RANK, 2 * hw)),
                  full((1, KV_RANK)), full((2 * KV_RANK, 2 * hw)), full((hw, KV_RANK)),
                  full((1, HEAD_PAD)), full((1, HEAD_PAD)), full((1, HEAD_PAD)), full((1, HEAD_PAD)),
                  full((HEAD_PAD, 1)), tab(), tab()],
        out_specs=[pl.BlockSpec((1, t, OFF_Q), lambda bi, i: (bi, i, 0)),
                   head_out(), head_out(),
                   pl.BlockSpec((1, HEADS, HEAD_PAD, t), lambda bi, i: (bi, 0, 0, i))],
        out_shape=[jax.ShapeDtypeStruct((b, s, OFF_Q), F32),
                   jax.ShapeDtypeStruct((b, HEADS, s, HEAD_PAD), BF16),
                   jax.ShapeDtypeStruct((b, HEADS, s, HEAD_PAD), BF16),
                   jax.ShapeDtypeStruct((b, HEADS, HEAD_PAD, s), BF16)],
        compiler_params=_cparams("parallel", "parallel"),
        name="inproj",
    )(x, wts["norm1_g"], mod["sc1"], mod["sh1"], wts["w_in"], wts["gq"], wts["wq"], wts["gkv"], wts["wk"],
      wts["wvt"], wts["gqq"], wts["gqq_rot"], wts["gqk"], wts["gqk_rot"], wts["vone"], *rope_tabs)


def _seqmix_kernel(prev_ref, cur_ref, next_ref, wpool_ref, pscale_ref, cw_ref, cb_ref,
                   pool_ref, x0_ref, z_ref, *, seq_len):
    i = pl.program_id(1)
    n = pl.num_programs(1)
    t_rows = cur_ref.shape[1]
    halo = prev_ref.shape[1]
    r_rows = t_rows + 2 * halo
    prev = jnp.where(i > 0, prev_ref[0], 0.0)
    nxt = jnp.where(i < n - 1, next_ref[0], 0.0)
    ext = jnp.concatenate([prev, cur_ref[0], nxt], axis=0)

    def sh(a, dlt):
        return pltpu.roll(a, (-dlt) % r_rows, axis=0)

    u = ext[:, :POOL_WIDTH]
    a2 = u + sh(u, -1)
    a4 = sh(a2, -1) + sh(a2, 1)
    a8 = sh(a4, -2) + sh(a4, 2)
    a16 = sh(a8, -4) + sh(a8, 4)
    grp = lax.broadcasted_iota(jnp.int32, (1, POOL_WIDTH), 1) // POOL_GROUP_DIM
    win = jnp.where(grp == 0, a2, jnp.where(grp == 1, a4, jnp.where(grp == 2, a8, a16)))
    half = jnp.where(grp == 0, 1, jnp.where(grp == 1, 2, jnp.where(grp == 2, 4, 8)))
    pos = i * t_rows - halo + lax.broadcasted_iota(jnp.int32, (r_rows, 1), 0)
    cnt = jnp.minimum(pos + half, seq_len) - jnp.maximum(pos - half, 0)
    cnt = jnp.maximum(cnt, 1).astype(F32)
    dev = (win / cnt - u)[halo:halo + t_rows]
    pool = jnp.dot(dev.astype(BF16), wpool_ref[...], preferred_element_type=F32) * pscale_ref[...]
    pool_ref[0] = pool.astype(BF16)

    e = ext[:, OFF_HY:OFF_Q]
    cw = cw_ref[...]
    uc = sh(e, -1) * cw[0:1] + e * cw[1:2] + sh(e, 1) * cw[2:3] + cb_ref[...]
    uc = uc[halo:halo + t_rows]
    x0_ref[0] = uc[:, :HY_WIDTH]
    z_ref[0] = uc[:, HY_WIDTH:2 * HY_WIDTH] * uc[:, 2 * HY_WIDTH:]


def _seqmix(pph, wts):
    b, s, w = pph.shape
    t = min(ROW_TILE, s)
    halo = 8
    per = t // halo
    nh = s // halo
    full = lambda shape: pl.BlockSpec(shape, lambda bi, i: (0,) * len(shape))
    out = lambda: pl.BlockSpec((1, t, HY_WIDTH), lambda bi, i: (bi, i, 0))
    return pl.pallas_call(
        functools.partial(_seqmix_kernel, seq_len=s),
        grid=(b, s // t),
        in_specs=[pl.BlockSpec((1, halo, w), lambda bi, i: (bi, jnp.maximum(i * per - 1, 0), 0)),
                  pl.BlockSpec((1, t, w), lambda bi, i: (bi, i, 0)),
                  pl.BlockSpec((1, halo, w), lambda bi, i: (bi, jnp.minimum((i + 1) * per, nh - 1), 0)),
                  full((POOL_WIDTH, POOL_WIDTH)), full((1, POOL_WIDTH)), full((3, HY_IN)), full((1, HY_IN))],
        out_specs=[out(), out(), out()],
        out_shape=[jax.ShapeDtypeStruct((b, s, POOL_WIDTH), BF16),
                   jax.ShapeDtypeStruct((b, s, HY_WIDTH), F32),
                   jax.ShapeDtypeStruct((b, s, HY_WIDTH), F32)],
        compiler_params=_cparams("parallel", "parallel"),
        name="seqmix",
    )(pph, pph, pph, wts["wpool"], wts["pool_scale"], wts["conv_w"], wts["conv_b"])


def _dot3(m_ref, x):
    x_hi = x.astype(BF16)
    x_lo = (x - x_hi.astype(F32)).astype(BF16)
    m_hi, m_lo = m_ref[0], m_ref[1]
    return (jnp.dot(m_hi, x_hi, preferred_element_type=F32) + jnp.dot(m_hi, x_lo, preferred_element_type=F32)
            + jnp.dot(m_lo, x_hi, preferred_element_type=F32))


def _split_const(m):
    hi = m.astype(BF16)
    return jnp.stack([hi, (m - hi.astype(F32)).astype(BF16)])


def _leftmm_kernel(*refs, cmul, gate):
    m_ref, x_ref = refs[0], refs[1]
    nxt = 2
    x = x_ref[0]
    if cmul:
        ks = refs[nxt][...]
        nxt += 1
        kh = x.shape[0] // 2
        xr, xi, kr, ki = x[:kh], x[kh:], ks[:kh], ks[kh:]
        x = jnp.concatenate([xr * kr - xi * ki, xr * ki + xi * kr], axis=0)
    y = jnp.dot(m_ref[...], x, precision=HIGHEST, preferred_element_type=F32)
    if gate:
        z_ref, bias_ref, x0_ref = refs[nxt:nxt + 3]
        nxt += 3
        y = (y + z_ref[0] * bias_ref[...]) * x0_ref[0]
    o_ref = refs[nxt]
    o_ref[0] = y.astype(o_ref.dtype)


def _leftmm(m, x, spectrum=None, gate=None, out_dtype=F32):
    g, k, cols = x.shape
    mo = m.shape[0]
    tc = min(cols, 2048)
    args = [m, x]
    in_specs = [pl.BlockSpec((mo, k), lambda gi, j: (0, 0)), pl.BlockSpec((1, k, tc), lambda gi, j: (gi, 0, j))]
    if spectrum is not None:
        args.append(spectrum)
        in_specs.append(pl.BlockSpec((k, tc), lambda gi, j: (0, j)))
    if gate is not None:
        z, bias, x0 = gate
        args += [z, bias, x0]
        in_specs += [pl.BlockSpec((1, mo, tc), lambda gi, j: (gi, 0, j)),
                     pl.BlockSpec((1, tc), lambda gi, j: (0, j)),
                     pl.BlockSpec((1, mo, tc), lambda gi, j: (gi, 0, j))]
    return pl.pallas_call(
        functools.partial(_leftmm_kernel, cmul=spectrum is not None, gate=gate is not None),
        grid=(g, cols // tc),
        in_specs=in_specs,
        out_specs=pl.BlockSpec((1, mo, tc), lambda gi, j: (gi, 0, j)),
        out_shape=jax.ShapeDtypeStruct((g, mo, cols), out_dtype),
        compiler_params=_cparams("parallel", "parallel"),
        name="dft_leftmm",
    )(*args)


def _blockfft_kernel(*refs, n2, blocks, conv):
    if conv:
        a_ref, tw_ref, ks_ref, m2_ref, m2c_ref, o_ref = refs
    else:
        a_ref, tw_ref, m2_ref, o_ref = refs

    def body(j, carry):
        rows = pl.ds(j * n2, n2)
        ar, ai = a_ref[0, 0, rows, :], a_ref[0, 1, rows, :]
        tcs, tsn = tw_ref[0, rows, :], tw_ref[1, rows, :]
        tcs = jnp.concatenate([tcs] * (ar.shape[1] // tcs.shape[1]), axis=1)
        tsn = jnp.concatenate([tsn] * (ar.shape[1] // tsn.shape[1]), axis=1)
        bstk = jnp.concatenate([ar * tcs + ai * tsn, ai * tcs - ar * tsn], axis=0)
        xs = _dot3(m2_ref, bstk)
        if not conv:
            o_ref[0, 0, rows, :] = xs[:n2]
            o_ref[0, 1, rows, :] = xs[n2:]
            return carry
        xr, xi = xs[:n2], xs[n2:]
        kr, ki = ks_ref[0, rows, :], ks_ref[1, rows, :]
        ystk = jnp.concatenate([xr * kr - xi * ki, xr * ki + xi * kr], axis=0)
        cs = _dot3(m2c_ref, ystk)
        cr, ci = cs[:n2], cs[n2:]
        o_ref[0, 0, rows, :] = cr * tcs - ci * tsn
        o_ref[0, 1, rows, :] = cr * tsn + ci * tcs
        return carry

    for j in range(blocks):
        body(j, 0)


def _blockfft(a, tw, m2, spectrum=None, m2c=None):
    g, _, n, c = a.shape
    n2 = m2.shape[1] // 2
    blocks = min(8, n // n2)
    rows = blocks * n2
    conv = spectrum is not None
    args = [a, tw]
    in_specs = [pl.BlockSpec((1, 2, rows, c), lambda gi, j: (gi, 0, j, 0)),
                pl.BlockSpec((2, rows, LANE), lambda gi, j: (0, j, 0))]
    if conv:
        args.append(spectrum)
        in_specs.append(pl.BlockSpec((2, rows, c), lambda gi, j: (0, j, 0)))
    args.append(m2)
    in_specs.append(pl.BlockSpec(m2.shape, lambda gi, j: (0, 0, 0)))
    if conv:
        args.append(m2c)
        in_specs.append(pl.BlockSpec(m2c.shape, lambda gi, j: (0, 0, 0)))
    return pl.pallas_call(
        functools.partial(_blockfft_kernel, n2=n2, blocks=blocks, conv=conv),
        grid=(g, n // rows),
        in_specs=in_specs,
        out_specs=pl.BlockSpec((1, 2, rows, c), lambda gi, j: (gi, 0, j, 0)),
        out_shape=jax.ShapeDtypeStruct((g, 2, n, c), F32),
        compiler_params=_cparams("parallel", "parallel"),
        name="dft_block",
    )(*args)


def _filter_kernel(z_ref, w1_ref, b1_ref, w2_ref, b2_ref, w3_ref, fr_ref, dl_ref, o_ref, *, seq_len):
    i = pl.program_id(0)
    rows = z_ref.shape[0]
    z = z_ref[...]
    fr = fr_ref[...]
    h = jnp.sin(fr * (jnp.dot(z, w1_ref[...], precision=HIGHEST, preferred_element_type=F32) + b1_ref[...]))
    h = jnp.sin(fr * (jnp.dot(h, w2_ref[...], precision=HIGHEST, preferred_element_type=F32) + b2_ref[...]))
    hf = jnp.dot(h, w3_ref[...], precision=HIGHEST, preferred_element_type=F32)
    decay = jnp.exp(-z[:, 0:1] * dl_ref[...])
    j = i * rows + lax.broadcasted_iota(jnp.int32, (rows, 1), 0)
    taps = jnp.where(j < seq_len, hf[:, :HY_WIDTH], jnp.where(j > seq_len, hf[:, HY_WIDTH:], 0.0))
    o_ref[...] = taps * decay


def _hyena_filter(seq_len, wts):
    n = 2 * seq_len
    j = jnp.arange(n)
    lag = jnp.clip(jnp.where(j < seq_len, j, n - j), 0, seq_len - 1).astype(F32)[:, None]
    t = lag / (seq_len - 1)
    wpos = (2.0 * math.pi / seq_len) * lag
    f = jnp.linspace(1e-4, HY_BANDS - 1, HY_BANDS, dtype=F32)[None, :]
    feat = jnp.concatenate([t, jnp.cos(f * wpos), -jnp.sin(f * wpos)], axis=-1)
    feat2 = jnp.pad(feat, ((0, 0), (0, LANE - HY_EMB)))
    rows = min(n, 2048)
    full = lambda shape: pl.BlockSpec(shape, lambda i: (0,) * len(shape))
    return pl.pallas_call(
        functools.partial(_filter_kernel, seq_len=seq_len),
        grid=(n // rows,),
        in_specs=[pl.BlockSpec((rows, LANE), lambda i: (i, 0)),
                  full((LANE, LANE)), full((1, LANE)), full((LANE, LANE)), full((1, LANE)),
                  full((LANE, 2 * HY_WIDTH)), full((1, LANE)), full((1, HY_WIDTH))],
        out_specs=pl.BlockSpec((rows, HY_WIDTH), lambda i: (i, 0)),
        out_shape=jax.ShapeDtypeStruct((n, HY_WIDTH), F32),
        compiler_params=_cparams("parallel"),
        name="hyena_filter",
    )(feat2, wts["f_w1"], wts["f_b1"], wts["f_w2"], wts["f_b2"], wts["f_w3"], wts["f_freq"], wts["f_delta"])


def _cs(p, rows, cols):
    ang = 2.0 * np.pi * np.outer(np.arange(rows), np.arange(cols)) / p
    return np.cos(ang), np.sin(ang)


def _mat(blocks):
    return jnp.asarray(np.block(blocks), dtype=F32)


def _dftcols_kernel(*refs, gate):
    m_ref, x_ref = refs[0], refs[1]
    x = x_ref[0]
    k, nt, _ = x.shape
    xt = pltpu.einshape("knc->(nk)c", x)
    ys = [_dot3(m_ref, xt[n * k:(n + 1) * k]) for n in range(nt)]
    y = pltpu.einshape("(nk)c->knc", jnp.concatenate(ys, axis=0), n=nt)
    if gate:
        z_ref, bias_ref, x0_ref, o_ref = refs[2:]
        y = (y + z_ref[0] * bias_ref[...]) * x0_ref[0]
    else:
        o_ref = refs[2]
    o_ref[0] = y.astype(o_ref.dtype)


def _dftcols(m, x, gate=None, out_dtype=F32):
    g, k, n2, c = x.shape
    mo = m.shape[1]
    nt = 16 if n2 % 16 == 0 else 8
    blk = lambda rows: pl.BlockSpec((1, rows, nt, c), lambda gi, j: (gi, 0, j, 0))
    args = [m, x]
    in_specs = [pl.BlockSpec((2, mo, k), lambda gi, j: (0, 0, 0)), blk(k)]
    if gate is not None:
        z, bias, x0 = gate
        args += [z, bias, x0]
        in_specs += [blk(mo), pl.BlockSpec((1, c), lambda gi, j: (0, 0)), blk(mo)]
    return pl.pallas_call(
        functools.partial(_dftcols_kernel, gate=gate is not None),
        grid=(g, n2 // nt),
        in_specs=in_specs,
        out_specs=blk(mo),
        out_shape=jax.ShapeDtypeStruct((g, mo, n2, c), out_dtype),
        compiler_params=_cparams("parallel", "parallel"),
        name="dft_cols",
    )(*args)


def _hyena_long(z, x0, bias, filt):
    b, s, c = z.shape
    n = 2 * s
    n1 = 1 << ((n.bit_length() - 1 + 1) // 2)
    n2 = n // n1
    g = b // 2
    c1, s1 = _cs(n1, n1, n1)
    h1 = n1 // 2
    m_fwd = _split_const(_mat([[c1[:, :h1], s1[:, :h1]], [-s1[:, :h1], c1[:, :h1]]]))
    m_filt = _split_const(_mat([[c1], [-s1]]))
    m_inv = _split_const(_mat([[c1[:h1], -s1[:h1]], [s1[:h1], c1[:h1]]]) * (1.0 / n))
    c2, s2 = _cs(n2, n2, n2)
    m2 = _split_const(_mat([[c2, s2], [-s2, c2]]))
    m2c = _split_const(_mat([[c2, -s2], [s2, c2]]))
    k1 = jnp.arange(n1, dtype=jnp.int32)[:, None]
    nn2 = jnp.arange(n2, dtype=jnp.int32)[None, :]
    ang = (2.0 * math.pi / n) * (k1 * nn2).astype(F32).reshape(n)
    tw = jnp.broadcast_to(jnp.stack([jnp.cos(ang), jnp.sin(ang)])[:, :, None], (2, n, LANE))

    spec = _dftcols(m_filt, filt.reshape(1, n1, n2, c)).reshape(1, 2, n, c)
    spec = _blockfft(spec, tw, m2)[0]
    zv = z.reshape(g, n1, n2, c)
    a = _dftcols(m_fwd, zv).reshape(g, 2, n, c)
    dd = _blockfft(a, tw, m2, spectrum=spec, m2c=m2c).reshape(g, 2 * n1, n2, c)
    y = _dftcols(m_inv, dd, gate=(zv, bias, x0.reshape(g, n1, n2, c)), out_dtype=BF16)
    return y.reshape(b, s, c)


def _hyena_short(z, x0, bias, filt):
    b, s, c = z.shape
    n = 2 * s
    g = b // 2
    cf, sf = _cs(n, n, n)
    m_fwd = _mat([[cf[:, :s], sf[:, :s]], [-sf[:, :s], cf[:, :s]]])
    m_filt = _mat([[cf], [-sf]])
    m_inv = _mat([[cf[:s], -sf[:s]], [sf[:s], cf[:s]]]) * (1.0 / n)
    spec = _leftmm(m_filt, filt.reshape(1, n, c))[0]
    zv = z.reshape(g, n, c)
    zf = _leftmm(m_fwd, zv)
    y = _leftmm(m_inv, zf, spectrum=spec, gate=(zv, bias, x0.reshape(g, n, c)), out_dtype=BF16)
    return y.reshape(b, s, c)


def _attn_kernel(*refs, tk, n_chunks, heads, unroll):
    if n_chunks:
        q_ref, kc_ref, vc_ref, kl_ref, vl_ref, o_ref, s_ref = refs
    else:
        q_ref, kc_ref, vc_ref, o_ref = refs
    dn = (((1,), (1,)), ((), ()))
    qs = [q_ref[0, hh] for hh in range(heads)]

    def chunk_rows(j):
        return pl.ds(pl.multiple_of(j * tk, tk), tk)

    def scores(j, slot):
        for hh in range(heads):
            s_ref[slot, hh] = lax.dot_general(kl_ref[0, hh, chunk_rows(j), :], qs[hh], dn,
                                              preferred_element_type=F32)

    def absorb(j, slot, carry):
        out = []
        for hh in range(heads):
            m_old, acc_old = carry[hh]
            sj = s_ref[slot, hh]
            m_new = jnp.maximum(m_old, jnp.max(sj, axis=0, keepdims=True))
            pj = jnp.exp2(sj - m_new)
            acc_new = jnp.exp2(m_old - m_new) * acc_old + jnp.dot(vl_ref[0, hh, :, chunk_rows(j)], pj.astype(BF16),
                                                                  preferred_element_type=F32)
            out.append((m_new, acc_new))
        return tuple(out)

    if n_chunks:
        scores(0, 0)
    state = []
    for hh in range(heads):
        s = lax.dot_general(kc_ref[0, hh], qs[hh], dn, preferred_element_type=F32)
        m = jnp.max(s, axis=0, keepdims=True)
        p = jnp.exp2(s - m)
        state.append((m, jnp.dot(vc_ref[0, hh], p.astype(BF16), preferred_element_type=F32)))

    def body(jj, carry):
        j = unroll * jj
        for u in range(unroll):
            nxt = j + u + 1
            if u == unroll - 1:
                nxt = jnp.minimum(nxt, n_chunks - 1)
            scores(nxt, (u + 1) % 2)
            carry = absorb(j + u, u % 2, carry)
        return carry

    if n_chunks:
        state = lax.fori_loop(0, n_chunks // unroll, body, tuple(state))
    outs = [acc[:V_DIM] / acc[V_DIM:V_DIM + 1] for _, acc in state]
    o_ref[0] = jnp.concatenate(outs, axis=0).T.astype(o_ref.dtype)


def _attention(q, kc, vct, kl=None, vlt=None):
    b, h, s, dp = q.shape
    nc = kc.shape[2]
    tq = min(ROW_TILE, s)
    heads = LANE // V_DIM
    tk = 512
    n_chunks = 0 if kl is None else kl.shape[2] // tk
    unroll = 8 if n_chunks % 8 == 0 else 2
    args = [q, kc, vct]
    in_specs = [pl.BlockSpec((1, heads, tq, dp), lambda bi, hi, i: (bi, hi, i, 0)),
                pl.BlockSpec((1, heads, nc, dp), lambda bi, hi, i: (bi, hi, 0, 0)),
                pl.BlockSpec((1, heads, dp, nc), lambda bi, hi, i: (bi, hi, 0, 0))]
    scratch = []
    if n_chunks:
        assert unroll % 2 == 0 and n_chunks % unroll == 0
        sl = kl.shape[2]
        args += [kl, vlt]
        in_specs += [pl.BlockSpec((1, heads, sl, dp), lambda bi, hi, i: (bi, hi, 0, 0)),
                     pl.BlockSpec((1, heads, dp, sl), lambda bi, hi, i: (bi, hi, 0, 0))]
        scratch = [pltpu.VMEM((2, heads, tk, tq), F32)]
    return pl.pallas_call(
        functools.partial(_attn_kernel, tk=tk, n_chunks=n_chunks, heads=heads, unroll=unroll),
        grid=(b, h // heads, s // tq),
        in_specs=in_specs,
        out_specs=pl.BlockSpec((1, tq, heads * V_DIM), lambda bi, hi, i: (bi, i, hi)),
        out_shape=jax.ShapeDtypeStruct((b, s, h * V_DIM), BF16),
        scratch_shapes=scratch,
        compiler_params=_cparams("parallel", "parallel", "parallel"),
        name="attention",
    )(*args)


def _outproj_kernel(x_ref, pool_ref, hy_ref, at_ref, wout_ref, g1_ref, gn_ref, sc_ref, sh_ref, rw_ref, rb_ref,
                    x1_ref, h2_ref, lg_ref):
    mix = jnp.concatenate([pool_ref[0], hy_ref[0], at_ref[0]], axis=-1)
    y = jnp.dot(mix, wout_ref[...], preferred_element_type=F32)
    x1 = x_ref[0] + g1_ref[0] * y
    x1_ref[0] = x1
    h2 = _rms(x1, gn_ref[...], D_MODEL) * (1.0 + sc_ref[0]) + sh_ref[0]
    h2_ref[0] = _pack_rows(h2)
    lg_ref[0] = lax.dot_general(rw_ref[...], h2, (((1,), (1,)), ((), ())), precision=HIGHEST,
                                preferred_element_type=F32) + rb_ref[...]


def _outproj(x, pool, hy, attn, seg_of_batch, wts, mod):
    b, s, d = x.shape
    t = min(ROW_TILE, s)
    full = lambda shape: pl.BlockSpec(shape, lambda bi, i: (0,) * len(shape))
    seg = lambda: pl.BlockSpec((1, 1, d), lambda bi, i: (seg_of_batch(bi), 0, 0))
    rowblk = lambda w: pl.BlockSpec((1, t, w), lambda bi, i: (bi, i, 0))
    return pl.pallas_call(
        _outproj_kernel,
        grid=(b, s // t),
        in_specs=[rowblk(d), rowblk(POOL_WIDTH), rowblk(HY_WIDTH), rowblk(MLA_WIDTH),
                  full((d, d)), seg(), full((1, d)), seg(), seg(), full((N_EXPERTS, d)), full((N_EXPERTS, 1))],
        out_specs=[rowblk(d), rowblk(d // 2), pl.BlockSpec((1, N_EXPERTS, t), lambda bi, i: (bi, 0, i))],
        out_shape=[jax.ShapeDtypeStruct((b, s, d), F32), jax.ShapeDtypeStruct((b, s, d // 2), jnp.int32),
                   jax.ShapeDtypeStruct((b, N_EXPERTS, s), F32)],
        compiler_params=_cparams("parallel", "parallel"),
        name="outproj",
    )(x, pool, hy, attn, wts["w_out"], mod["g1"], wts["norm2_g"], mod["sc2"], mod["sh2"],
      wts["router_w"], wts["router_b"])


def _moe_kernel(be_ref, br_ref, x_ref, w1_ref, b1_ref, w2_ref, b2_ref, o_ref, w1b_ref, w2b_ref):
    i = pl.program_id(0)
    e = be_ref[i]
    e_prev = be_ref[jnp.maximum(i - 1, 0)]
    n_rows = br_ref[i]

    @pl.when((i == 0) | (e != e_prev))
    def _():
        w1b_ref[...] = w1_ref[0, 0].astype(BF16)
        w2b_ref[...] = w2_ref[0, 0].astype(BF16)

    @pl.when(n_rows > 0)
    def _():
        live = lax.broadcasted_iota(jnp.int32, (x_ref.shape[0], 1), 0) < n_rows
        x = _unpack_rows(jnp.where(live, x_ref[...], 0))
        a = jnp.dot(x, w1b_ref[...], preferred_element_type=F32) + b1_ref[0, 0]
        glu = jnp.minimum(a[:, :D_EXPERT], SWIGLU_LIMIT)
        lin = jnp.clip(a[:, D_EXPERT:], -SWIGLU_LIMIT, SWIGLU_LIMIT)
        act = glu * (1.0 / (1.0 + jnp.exp(-SWIGLU_ALPHA * glu))) * (lin + 1.0)
        y = jnp.dot(act.astype(BF16), w2b_ref[...], preferred_element_type=F32) + b2_ref[0, 0]
        o_ref[...] = _pack_rows(y)

    @pl.when(n_rows == 0)
    def _():
        o_ref[...] = jnp.zeros_like(o_ref)


def _experts(xs, block_e, block_rows, w1, b1, w2, b2, layer):
    n_buf, dh = xs.shape
    d = 2 * dh
    tm = EXPERT_TILE
    n_blocks = n_buf // tm
    de2 = w1.shape[3]
    grid_spec = pltpu.PrefetchScalarGridSpec(
        num_scalar_prefetch=2,
        grid=(n_blocks,),
        in_specs=[pl.BlockSpec((tm, dh), lambda i, be, nu: (i, 0)),
                  pl.BlockSpec((1, 1, d, de2), lambda i, be, nu: (layer, be[i], 0, 0)),
                  pl.BlockSpec((1, 1, 1, de2), lambda i, be, nu: (layer, be[i], 0, 0)),
                  pl.BlockSpec((1, 1, D_EXPERT, d), lambda i, be, nu: (layer, be[i], 0, 0)),
                  pl.BlockSpec((1, 1, 1, d), lambda i, be, nu: (layer, be[i], 0, 0))],
        out_specs=pl.BlockSpec((tm, dh), lambda i, be, nu: (i, 0)),
        scratch_shapes=[pltpu.VMEM((d, de2), BF16), pltpu.VMEM((D_EXPERT, d), BF16)],
    )
    return pl.pallas_call(
        _moe_kernel,
        grid_spec=grid_spec,
        out_shape=jax.ShapeDtypeStruct((n_buf, dh), jnp.int32),
        compiler_params=_cparams("arbitrary"),
        name="experts",
    )(block_e, block_rows, xs, w1, b1, w2, b2)


def _router_kernel(lg_ref, idx_ref, gate_ref, rank_ref, cnt_ref, base_ref):
    i = pl.program_id(0)
    n_e, t = lg_ref.shape

    @pl.when(i == 0)
    def _():
        base_ref[...] = jnp.zeros_like(base_ref)

    work = lg_ref[...]
    eid = lax.broadcasted_iota(jnp.int32, (n_e, t), 0)
    vals, idxs, sels = [], [], []
    for _ in range(TOP_K):
        m = jnp.max(work, axis=0, keepdims=True)
        idx = jnp.min(jnp.where(work == m, eid, n_e), axis=0, keepdims=True)
        sel = eid == idx
        work = jnp.where(sel, -jnp.inf, work)
        vals.append(m)
        idxs.append(idx)
        sels.append(sel)
    cnt = jnp.where(sels[0], 1.0, 0.0)
    for sel in sels[1:]:
        cnt = cnt + jnp.where(sel, 1.0, 0.0)
    earlier = (lax.broadcasted_iota(jnp.int32, (t, t), 0) < lax.broadcasted_iota(jnp.int32, (t, t), 1))
    within = jnp.dot(cnt.astype(BF16), jnp.where(earlier, 1.0, 0.0).astype(BF16), preferred_element_type=F32)
    pref = base_ref[:, 0:1] + within
    ranks = [jnp.sum(jnp.where(sel, pref, 0.0), axis=0, keepdims=True) for sel in sels]
    new_base = base_ref[...] + jnp.sum(cnt, axis=1, keepdims=True)
    base_ref[...] = new_base
    cnt_ref[...] = new_base
    ex = [jnp.exp(v - vals[0]) for v in vals]
    den = ex[0]
    for e in ex[1:]:
        den = den + e
    idx_ref[...] = jnp.concatenate(idxs, axis=0)
    gate_ref[...] = jnp.concatenate([e / den for e in ex], axis=0)
    rank_ref[...] = jnp.concatenate(ranks, axis=0).astype(jnp.int32)


def _route(logits_t):
    n_e, n_tok = logits_t.shape
    tm = EXPERT_TILE
    t = ROW_TILE
    tokblk = lambda: pl.BlockSpec((TOP_K, t), lambda i: (0, i))
    idx_t, gates_t, rank_t, counts_f = pl.pallas_call(
        _router_kernel,
        grid=(n_tok // t,),
        in_specs=[pl.BlockSpec((n_e, t), lambda i: (0, i))],
        out_specs=[tokblk(), tokblk(), tokblk(), pl.BlockSpec((n_e, LANE), lambda i: (0, 0))],
        out_shape=[jax.ShapeDtypeStruct((TOP_K, n_tok), jnp.int32), jax.ShapeDtypeStruct((TOP_K, n_tok), F32),
                   jax.ShapeDtypeStruct((TOP_K, n_tok), jnp.int32), jax.ShapeDtypeStruct((n_e, LANE), F32)],
        scratch_shapes=[pltpu.VMEM((n_e, LANE), F32)],
        compiler_params=_cparams("arbitrary"),
        name="router",
    )(logits_t)
    n_slots = n_tok * TOP_K
    counts = counts_f[:, 0].astype(jnp.int32)
    padded = (counts + tm - 1) // tm * tm
    padded_end = jnp.cumsum(padded)
    padded_start = padded_end - padded
    experts = jnp.arange(n_e, dtype=jnp.int32)
    dest_t = rank_t + jnp.sum(jnp.where(idx_t[None] == experts[:, None, None], padded_start[:, None, None], 0), axis=0)
    n_blocks = n_slots // tm + n_e
    block_start = jnp.arange(n_blocks, dtype=jnp.int32) * tm
    block_e = jnp.minimum(jnp.sum(padded_end[None, :] <= block_start[:, None], axis=1), n_e - 1).astype(jnp.int32)
    sel = (block_e[:, None] == experts[None, :])
    pick = lambda v: jnp.sum(jnp.where(sel, v[None, :], 0), axis=1)
    block_rows = jnp.clip(pick(padded_start) + pick(counts) - block_start, 0, tm).astype(jnp.int32)
    return gates_t.T, dest_t, block_e, block_rows


SC_WINDOW = 32
SC_SLOTS = 4


def _sc_gather(table, idx):
    _, d = table.shape
    b = idx.shape[0]
    info = plsc.get_sparse_core_info()
    nc, ns = info.num_cores, info.num_subcores
    nw = nc * ns
    w, nslot = SC_WINDOW, SC_SLOTS
    per_w = b // nw
    n_win = per_w // w
    assert per_w * nw == b and n_win * w == per_w and n_win % nslot == 0
    mesh = plsc.VectorSubcoreMesh(core_axis_name="c", subcore_axis_name="s")

    def body(table_hbm, idx_hbm, out_hbm, idx_v, rows_v, gsem, osem):
        wid = lax.axis_index("s") * nc + lax.axis_index("c")
        base = wid * per_w
        pltpu.sync_copy(idx_hbm.at[wid], idx_v)

        def gather(g, slot):
            return pltpu.make_async_copy(table_hbm.at[idx_v.at[g]], rows_v.at[slot], gsem.at[slot])

        def put(g, slot):
            return pltpu.make_async_copy(rows_v.at[slot], out_hbm.at[pl.ds(base + g * w, w)], osem.at[slot])

        for g in range(nslot - 1):
            gather(g, g).start()

        @pl.loop(0, n_win, step=nslot)
        def _(g0):
            for slot in range(nslot):
                g = g0 + slot
                gather(g, slot).wait()
                put(g, slot).start()
                free = (slot - 1) % nslot

                @pl.when(g + nslot - 1 < n_win)
                def _():
                    @pl.when(g >= 1)
                    def _():
                        put(g - 1, free).wait()

                    gather(g + nslot - 1, free).start()

        for g in range(n_win - nslot, n_win):
            put(g, g % nslot).wait()

    return pl.kernel(
        body,
        out_type=jax.ShapeDtypeStruct((b, d), table.dtype),
        mesh=mesh,
        scratch_types=[pltpu.VMEM((n_win, w), jnp.int32), pltpu.VMEM((nslot, w, d), table.dtype),
                       pltpu.SemaphoreType.DMA((nslot,)), pltpu.SemaphoreType.DMA((nslot,))],
        name="sc_gather",
    )(table, idx.reshape(nw, n_win, w))


SC_SCATTER_WINDOW = 16


def _sc_scatter(rows, dest_t, n_out):
    n_tok, d = rows.shape
    k = dest_t.shape[0]
    info = plsc.get_sparse_core_info()
    nc, ns = info.num_cores, info.num_subcores
    nw = nc * ns
    w = SC_SCATTER_WINDOW
    per_w = n_tok // nw
    n_win = per_w // w
    assert per_w * nw == n_tok and n_win * w == per_w and n_win % 2 == 0
    mesh = plsc.VectorSubcoreMesh(core_axis_name="c", subcore_axis_name="s")
    idx = dest_t.reshape(k, nw, n_win, w).transpose(1, 0, 2, 3)

    def body(rows_hbm, idx_hbm, out_hbm, idx_v, rows_v, lsem, ssem):
        wid = lax.axis_index("s") * nc + lax.axis_index("c")
        base = wid * per_w
        pltpu.sync_copy(idx_hbm.at[wid], idx_v)

        def load(g, slot):
            return pltpu.make_async_copy(rows_hbm.at[pl.ds(base + g * w, w)], rows_v.at[slot], lsem.at[slot])

        def scatter(g, slot, j):
            return pltpu.make_async_copy(rows_v.at[slot], out_hbm.at[idx_v.at[j, g]], ssem.at[slot])

        load(0, 0).start()

        @pl.loop(0, n_win, step=2)
        def _(g0):
            for slot in range(2):
                g = g0 + slot
                load(g, slot).wait()
                for j in range(k):
                    scatter(g, slot, j).start()

                @pl.when(g + 1 < n_win)
                def _():
                    @pl.when(g >= 1)
                    def _():
                        for j in range(k):
                            scatter(g - 1, 1 - slot, j).wait()

                    load(g + 1, 1 - slot).start()

        for g in (n_win - 2, n_win - 1):
            for j in range(k):
                scatter(g, g % 2, j).wait()

    return pl.kernel(
        body,
        out_type=jax.ShapeDtypeStruct((n_out, d), rows.dtype),
        mesh=mesh,
        scratch_types=[pltpu.VMEM((k, n_win, w), jnp.int32), pltpu.VMEM((2, w, d), rows.dtype),
                       pltpu.SemaphoreType.DMA((2,)), pltpu.SemaphoreType.DMA((2,))],
        name="sc_scatter",
    )(rows, idx)


def _combine_kernel(x_ref, y_ref, gt_ref, g2_ref, o_ref):
    gt = gt_ref[...]
    acc = gt[:, 0:1] * _unpack_rows(y_ref[0]).astype(F32)
    for j in range(1, TOP_K):
        acc = acc + gt[:, j:j + 1] * _unpack_rows(y_ref[j]).astype(F32)
    o_ref[0] = x_ref[0] + g2_ref[0] * acc


def _combine(x, ysel, gates, tok_offset, seg_of_batch, mod):
    b, s, d = x.shape
    t = min(ROW_TILE, s)
    per = s // t
    off = tok_offset // t
    return pl.pallas_call(
        _combine_kernel,
        grid=(b, per),
        in_specs=[pl.BlockSpec((1, t, d), lambda bi, i: (bi, i, 0)),
                  pl.BlockSpec((TOP_K, t, d // 2), lambda bi, i: (0, off + bi * per + i, 0)),
                  pl.BlockSpec((t, TOP_K), lambda bi, i: (off + bi * per + i, 0)),
                  pl.BlockSpec((1, 1, d), lambda bi, i: (seg_of_batch(bi), 0, 0))],
        out_specs=pl.BlockSpec((1, t, d), lambda bi, i: (bi, i, 0)),
        out_shape=jax.ShapeDtypeStruct((b, s, d), F32),
        compiler_params=_cparams("parallel", "parallel"),
        name="combine",
    )(x, ysel, gates, mod["g2"])


def _rope_tables(seq_len):
    n_rows = seq_len // GRID_W
    row = jnp.repeat(jnp.arange(n_rows, dtype=F32), GRID_W)
    col = jnp.tile(jnp.arange(GRID_W, dtype=F32), n_rows)
    pairs = ROPE // 4
    inv = ROPE_BASE ** (-jnp.arange(pairs, dtype=F32) / pairs)
    ang = jnp.concatenate([row[:, None] * inv, col[:, None] * inv], axis=-1)
    cos, sin = jnp.cos(ang), jnp.sin(ang)
    one = jnp.ones((seq_len, NOPE), F32)
    zn = jnp.zeros((seq_len, NOPE), F32)
    zp = jnp.zeros((seq_len, HEAD_PAD - QK_DIM), F32)
    return jnp.concatenate([one, cos, cos, zp], axis=-1), jnp.concatenate([zn, sin, sin, zp], axis=-1)


def _identity_rope(seq_len):
    keep = jnp.concatenate([jnp.ones((seq_len, QK_DIM), F32), jnp.zeros((seq_len, HEAD_PAD - QK_DIM), F32)], axis=-1)
    return keep, jnp.zeros((seq_len, HEAD_PAD), F32)


def _rotate_half(w):
    half = ROPE // 2
    x1, x2 = w[..., NOPE:NOPE + half], w[..., NOPE + half:QK_DIM]
    return jnp.zeros_like(w).at[..., NOPE:NOPE + half].set(-x2).at[..., NOPE + half:QK_DIM].set(x1)


def _swap_rope_halves(g):
    half = ROPE // 2
    lo, hi = g[..., NOPE:NOPE + half], g[..., NOPE + half:QK_DIM]
    return g.at[..., NOPE:NOPE + half].set(hi).at[..., NOPE + half:QK_DIM].set(lo)


def _pad_cols(w, n):
    return jnp.pad(w, ((0, 0), (0, n - w.shape[1])))


def _layer_weights(l, p):
    hw = HEADS * HEAD_PAD
    wq = jnp.pad(p["mla_w_uq"][l].reshape(Q_RANK, HEADS, QK_DIM), ((0, 0), (0, 0), (0, HEAD_PAD - QK_DIM)))
    wukv = p["mla_w_ukv"][l].reshape(KV_RANK, HEADS, NOPE + V_DIM)
    wk = jnp.pad(wukv[:, :, :NOPE], ((0, 0), (0, 0), (0, HEAD_PAD - NOPE)))
    wv = jnp.pad(wukv[:, :, NOPE:], ((0, 0), (0, 0), (0, HEAD_PAD - V_DIM))).reshape(KV_RANK, hw)
    place = jnp.pad(jnp.eye(ROPE, dtype=F32), ((0, KV_RANK - ROPE), (NOPE, HEAD_PAD - QK_DIM)))
    wk = jnp.concatenate([wk, jnp.tile(place[:, None, :], (1, HEADS, 1))], axis=0)
    with_rot = lambda w: jnp.concatenate([w.reshape(-1, hw), _rotate_half(w).reshape(-1, hw)], axis=1).astype(BF16)
    wpool = jax.scipy.linalg.block_diag(*[p["pool_w"][l, g] for g in range(POOL_GROUPS)])
    vone = jnp.zeros((HEAD_PAD, 1), F32).at[V_DIM, 0].set(1.0)
    deltas = jnp.abs(jnp.linspace(HY_MIN_DECAY, HY_MAX_DECAY, HY_WIDTH, dtype=F32))[None, :]
    row = lambda v, n=None: (v if n is None else jnp.pad(v, (0, n - v.shape[0])))[None, :]
    return {
        "norm1_g": row(p["norm1_g"][l]), "norm2_g": row(p["norm2_g"][l]),
        "w_in": _pad_cols(p["w_in"][l], N_IN_PAD).astype(BF16),
        "gq": row(p["mla_q_norm_g"][l]), "wq": with_rot(wq),
        "gkv": row(p["mla_kv_norm_g"][l]), "wk": with_rot(wk), "wvt": wv.T.astype(BF16),
        "gqq": row(p["qk_norm_q"][l], HEAD_PAD), "gqq_rot": _swap_rope_halves(row(p["qk_norm_q"][l], HEAD_PAD)),
        "gqk": row(p["qk_norm_k"][l], HEAD_PAD), "gqk_rot": _swap_rope_halves(row(p["qk_norm_k"][l], HEAD_PAD)),
        "vone": vone,
        "wpool": wpool.astype(BF16), "pool_scale": row(p["pool_scale"][l]),
        "conv_w": p["hy_conv_w"][l], "conv_b": row(p["hy_conv_b"][l]),
        "f_w1": jnp.pad(p["hy_f_w1"][l], ((0, LANE - HY_EMB), (0, LANE - HY_HID))),
        "f_b1": row(p["hy_f_b1"][l], LANE),
        "f_w2": jnp.pad(p["hy_f_w2"][l], ((0, LANE - HY_HID), (0, LANE - HY_HID))),
        "f_b2": row(p["hy_f_b2"][l], LANE),
        "f_w3": jnp.pad(p["hy_f_w3"][l], ((0, LANE - HY_HID), (0, 0))),
        "f_freq": row(p["hy_freq"][l], LANE), "f_delta": deltas,
        "hy_bias": row(p["hy_bias"][l]),
        "w_out": p["w_out"][l].astype(BF16),
        "router_w": p["router_w"][l].T, "router_b": p["router_b"][l][:, None],
    }


def _layer(l, x, xc, cc, wts, p, rope_main, rope_ctx, last):
    b, s, d = x.shape
    n_ctx = xc.shape[1]
    modm = _modulation(cc, p["w_mod"], p["b_mod"][l][None, :], l)
    names = ("sh1", "sc1", "g1", "sh2", "sc2", "g2")
    mod = {nm: modm[:, i * d:(i + 1) * d].reshape(N_SEG, 1, d) for i, nm in enumerate(names)}
    main_seg = lambda bi: bi
    ctx_seg = lambda bi: CTX_SEG

    pph, q, kl, vl = _inproj(x, main_seg, wts, mod, rope_main)
    pph_c, qc, kc, vc = _inproj(xc, ctx_seg, wts, mod, rope_ctx)
    attn = _attention(q, kc, vc, kl, vl)
    pool, x0, z = _seqmix(pph, wts)
    hy = _hyena_long(z, x0, wts["hy_bias"], _hyena_filter(s, wts))
    x1, h2, logits = _outproj(x, pool, hy, attn, main_seg, wts, mod)
    h2 = h2.reshape(b * s, d // 2)
    logits = jnp.swapaxes(logits, 0, 1).reshape(N_EXPERTS, b * s)
    n_c = 0
    if not last:
        attn_c = _attention(qc, kc, vc)
        pool_c, x0c, zc = _seqmix(pph_c, wts)
        hy_c = _hyena_short(zc, x0c, wts["hy_bias"], _hyena_filter(n_ctx, wts))
        xc1, h2c, logits_c = _outproj(xc, pool_c, hy_c, attn_c, ctx_seg, wts, mod)
        n_c = b * n_ctx
        h2 = jnp.concatenate([h2c.reshape(n_c, d // 2), h2], axis=0)
        logits = jnp.concatenate([jnp.swapaxes(logits_c, 0, 1).reshape(N_EXPERTS, n_c), logits], axis=1)

    gates, dest_t, block_e, block_rows = _route(logits)
    de2 = p["moe_w1"].shape[-1]
    xs = _sc_scatter(h2, dest_t, block_e.shape[0] * EXPERT_TILE)
    ys = _experts(xs, block_e, block_rows, p["moe_w1"], p["moe_b1"].reshape(-1, N_EXPERTS, 1, de2),
                  p["moe_w2"], p["moe_b2"].reshape(-1, N_EXPERTS, 1, d), l)
    ysel = _sc_gather(ys, dest_t.reshape(-1)).reshape(TOP_K, -1, d // 2)
    x2 = _combine(x1, ysel, gates, n_c, main_seg, mod)
    if not last:
        xc = _combine(xc1, ysel, gates, 0, ctx_seg, mod)
    return x2, xc


def kernel(x, c, ctx, c_ctx, norm1_g, norm2_g, w_mod, b_mod, w_in, pool_w, pool_scale, hy_conv_w, hy_conv_b,
           hy_f_w1, hy_f_b1, hy_f_w2, hy_f_b2, hy_f_w3, hy_freq, hy_bias, mla_q_norm_g, mla_w_uq, mla_kv_norm_g,
           mla_w_ukv, qk_norm_q, qk_norm_k, w_out, router_w, router_b, moe_w1, moe_b1, moe_w2, moe_b2):
    p = dict(norm1_g=norm1_g, norm2_g=norm2_g, w_mod=w_mod, b_mod=b_mod, w_in=w_in, pool_w=pool_w,
             pool_scale=pool_scale,
             hy_conv_w=hy_conv_w, hy_conv_b=hy_conv_b, hy_f_w1=hy_f_w1, hy_f_b1=hy_f_b1, hy_f_w2=hy_f_w2,
             hy_f_b2=hy_f_b2, hy_f_w3=hy_f_w3, hy_freq=hy_freq, hy_bias=hy_bias, mla_q_norm_g=mla_q_norm_g,
             mla_w_uq=mla_w_uq, mla_kv_norm_g=mla_kv_norm_g, mla_w_ukv=mla_w_ukv, qk_norm_q=qk_norm_q,
             qk_norm_k=qk_norm_k, w_out=w_out, router_w=router_w, router_b=router_b, moe_w1=moe_w1,
             moe_b1=moe_b1, moe_w2=moe_w2, moe_b2=moe_b2)
    b, s, d = x.shape
    depth = w_mod.shape[0]
    cc = jnp.zeros((N_SEG, d), F32).at[:b].set(c).at[CTX_SEG].set(c_ctx)
    rope_main = _rope_tables(s)
    rope_ctx = _identity_rope(ctx.shape[1])
    xc = ctx
    for l in range(depth):
        wts = _layer_weights(l, p)
        x, xc = _layer(l, x, xc, cc, wts, p, rope_main, rope_ctx, l == depth - 1)
    return x
```

```python
import functools
import math

import jax
import jax.numpy as jnp
import numpy as np
from jax import lax
from jax.experimental import pallas as pl
from jax.experimental.pallas import tpu as pltpu
from jax.experimental.pallas import tpu_sc as plsc

F32 = jnp.float32
BF16 = jnp.bfloat16
HIGHEST = lax.Precision.HIGHEST

D_MODEL = 1024
GRID_W = 64
EPS = 1e-6
POOL_WIDTH = 256
POOL_GROUPS = 4
POOL_GROUP_DIM = POOL_WIDTH // POOL_GROUPS
HY_WIDTH = 256
HY_IN = 3 * HY_WIDTH
HY_BANDS = 8
HY_EMB = 1 + 2 * HY_BANDS
HY_HID = 64
HY_MIN_DECAY = math.log(1e-2) / 1.5
HY_MAX_DECAY = math.log(1e-2) / 0.3
HEADS = 8
Q_RANK = 256
KV_RANK = 128
NOPE = 64
ROPE = 32
V_DIM = 64
QK_DIM = NOPE + ROPE
MLA_WIDTH = HEADS * V_DIM
ATTN_SCALE = QK_DIM ** -0.5
LOG2E = math.log2(math.e)
OFF_HY = POOL_WIDTH
OFF_Q = OFF_HY + HY_IN
OFF_KV = OFF_Q + Q_RANK
OFF_KR = OFF_KV + KV_RANK
N_IN = OFF_KR + ROPE
N_IN_PAD = OFF_KR + 128
ROPE_BASE = 10000.0
N_EXPERTS = 32
TOP_K = 4
D_EXPERT = 1024
SWIGLU_LIMIT = 7.0
SWIGLU_ALPHA = 1.702

LANE = 128
HEAD_PAD = LANE
ROW_TILE = 512
EXPERT_TILE = 512
VMEM_LIMIT = 56 * 1024 * 1024
N_SEG = 8
CTX_SEG = 4


def _cparams(*sem):
    return pltpu.CompilerParams(dimension_semantics=sem, vmem_limit_bytes=VMEM_LIMIT)


def _rms(x, g, n):
    ss = jnp.sum(x * x, axis=-1, keepdims=True) * (1.0 / n)
    return x * lax.rsqrt(ss + EPS) * g


def _pack_rows(x):
    half = x.shape[1] // 2
    bits = lax.bitcast_convert_type(x.astype(BF16).astype(F32), jnp.int32)
    return bits[:, :half] | lax.shift_right_logical(bits[:, half:], 16)


def _unpack_rows(w):
    hi = lax.bitcast_convert_type(w & jnp.int32(-65536), F32)
    lo = lax.bitcast_convert_type(lax.shift_left(w, 16), F32)
    return jnp.concatenate([hi, lo], axis=1).astype(BF16)


def _mod_kernel(c_ref, w_ref, b_ref, o_ref):
    c = c_ref[...]
    s = c * (1.0 / (1.0 + jnp.exp(-c)))
    o_ref[...] = jnp.dot(s, w_ref[0], precision=HIGHEST, preferred_element_type=F32) + b_ref[...]


def _modulation(cc, w, b, layer):
    _, d, n = w.shape
    tn = 1536
    return pl.pallas_call(
        _mod_kernel,
        grid=(n // tn,),
        in_specs=[pl.BlockSpec((N_SEG, d), lambda j: (0, 0)),
                  pl.BlockSpec((1, d, tn), lambda j: (layer, 0, j)),
                  pl.BlockSpec((1, tn), lambda j: (0, j))],
        out_specs=pl.BlockSpec((N_SEG, tn), lambda j: (0, j)),
        out_shape=jax.ShapeDtypeStruct((N_SEG, n), F32),
        compiler_params=_cparams("arbitrary"),
        name="modulation",
    )(cc, w, b)


def _inproj_kernel(x_ref, gn_ref, sc_ref, sh_ref, win_ref, gq_ref, wq_ref, gkv_ref, wk_ref, wvt_ref,
                   gqq_ref, gqqr_ref, gqk_ref, gqkr_ref, vone_ref, rc_ref, rs_ref,
                   pph_ref, q_ref, k_ref, vt_ref):
    x = x_ref[0]
    h = _rms(x, gn_ref[...], D_MODEL) * (1.0 + sc_ref[0]) + sh_ref[0]
    p = jnp.dot(h.astype(BF16), win_ref[...], preferred_element_type=F32)
    pph_ref[0] = p[:, :OFF_Q]
    cq = _rms(p[:, OFF_Q:OFF_KV], gq_ref[...], Q_RANK)
    qall = jnp.dot(cq.astype(BF16), wq_ref[...], preferred_element_type=F32)
    ckv_b = _rms(p[:, OFF_KV:OFF_KR], gkv_ref[...], KV_RANK).astype(BF16)
    kin = jnp.concatenate([ckv_b, p[:, OFF_KR:N_IN_PAD].astype(BF16)], axis=-1)
    kall = jnp.dot(kin, wk_ref[...], preferred_element_type=F32)
    vt_all = lax.dot_general(wvt_ref[...], ckv_b, (((1,), (1,)), ((), ())), preferred_element_type=F32)
    rc, rs = rc_ref[...], rs_ref[...]
    hw = HEADS * HEAD_PAD

    def norm_rope(x_all, hd, g_ref, gr_ref):
        xh = x_all[:, HEAD_PAD * hd:HEAD_PAD * (hd + 1)]
        xr = x_all[:, hw + HEAD_PAD * hd:hw + HEAD_PAD * (hd + 1)]
        r = lax.rsqrt(jnp.sum(xh * xh, axis=-1, keepdims=True) * (1.0 / QK_DIM) + EPS)
        return (xh * (g_ref[...] * rc) + xr * (gr_ref[...] * rs)) * r

    for hd in range(HEADS):
        q_ref[0, hd] = (norm_rope(qall, hd, gqq_ref, gqqr_ref) * (ATTN_SCALE * LOG2E)).astype(BF16)
        k_ref[0, hd] = norm_rope(kall, hd, gqk_ref, gqkr_ref).astype(BF16)
        vt_ref[0, hd] = (vt_all[HEAD_PAD * hd:HEAD_PAD * (hd + 1)] + vone_ref[...]).astype(BF16)


def _inproj(x, seg_of_batch, wts, mod, rope_tabs):
    b, s, d = x.shape
    t = min(ROW_TILE, s)
    full = lambda shape: pl.BlockSpec(shape, lambda bi, i: (0,) * len(shape))
    seg = lambda: pl.BlockSpec((1, 1, d), lambda bi, i: (seg_of_batch(bi), 0, 0))
    tab = lambda: pl.BlockSpec((t, HEAD_PAD), lambda bi, i: (i, 0))
    hw = HEADS * HEAD_PAD
    head_out = lambda: pl.BlockSpec((1, HEADS, t, HEAD_PAD), lambda bi, i: (bi, 0, i, 0))
    return pl.pallas_call(
        _inproj_kernel,
        grid=(b, s // t),
        in_specs=[pl.BlockSpec((1, t, d), lambda bi, i: (bi, i, 0)),
                  full((1, d)), seg(), seg(),
                  full((d, N_IN_PAD)), full((1, Q_RANK)), full((Q_RANK, 2 * hw)),
                  full((1, KV_RANK)), full((2 * KV_RANK, 2 * hw)), full((hw, KV_RANK)),
                  full((1, HEAD_PAD)), full((1, HEAD_PAD)), full((1, HEAD_PAD)), full((1, HEAD_PAD)),
                  full((HEAD_PAD, 1)), tab(), tab()],
        out_specs=[pl.BlockSpec((1, t, OFF_Q), lambda bi, i: (bi, i, 0)),
                   head_out(), head_out(),
                   pl.BlockSpec((1, HEADS, HEAD_PAD, t), lambda bi, i: (bi, 0, 0, i))],
        out_shape=[jax.ShapeDtypeStruct((b, s, OFF_Q), F32),
                   jax.ShapeDtypeStruct((b, HEADS, s, HEAD_PAD), BF16),
                   jax.ShapeDtypeStruct((b, HEADS, s, HEAD_PAD), BF16),
                   jax.ShapeDtypeStruct((b, HEADS, HEAD_PAD, s), BF16)],
        compiler_params=_cparams("parallel", "parallel"),
        name="inproj",
    )(x, wts["norm1_g"], mod["sc1"], mod["sh1"], wts["w_in"], wts["gq"], wts["wq"], wts["gkv"], wts["wk"],
      wts["wvt"], wts["gqq"], wts["gqq_rot"], wts["gqk"], wts["gqk_rot"], wts["vone"], *rope_tabs)


def _seqmix_kernel(prev_ref, cur_ref, next_ref, wpool_ref, pscale_ref, cw_ref, cb_ref,
                   pool_ref, x0_ref, z_ref, *, seq_len):
    i = pl.program_id(1)
    n = pl.num_programs(1)
    t_rows = cur_ref.shape[1]
    halo = prev_ref.shape[1]
    r_rows = t_rows + 2 * halo
    prev = jnp.where(i > 0, prev_ref[0], 0.0)
    nxt = jnp.where(i < n - 1, next_ref[0], 0.0)
    ext = jnp.concatenate([prev, cur_ref[0], nxt], axis=0)

    def sh(a, dlt):
        return pltpu.roll(a, (-dlt) % r_rows, axis=0)

    u = ext[:, :POOL_WIDTH]
    a2 = u + sh(u, -1)
    a4 = sh(a2, -1) + sh(a2, 1)
    a8 = sh(a4, -2) + sh(a4, 2)
    a16 = sh(a8, -4) + sh(a8, 4)
    grp = lax.broadcasted_iota(jnp.int32, (1, POOL_WIDTH), 1) // POOL_GROUP_DIM
    win = jnp.where(grp == 0, a2, jnp.where(grp == 1, a4, jnp.where(grp == 2, a8, a16)))
    half = jnp.where(grp == 0, 1, jnp.where(grp == 1, 2, jnp.where(grp == 2, 4, 8)))
    pos = i * t_rows - halo + lax.broadcasted_iota(jnp.int32, (r_rows, 1), 0)
    cnt = jnp.minimum(pos + half, seq_len) - jnp.maximum(pos - half, 0)
    cnt = jnp.maximum(cnt, 1).astype(F32)
    dev = (win / cnt - u)[halo:halo + t_rows]
    pool = jnp.dot(dev.astype(BF16), wpool_ref[...], preferred_element_type=F32) * pscale_ref[...]
    pool_ref[0] = pool.astype(BF16)

    e = ext[:, OFF_HY:OFF_Q]
    cw = cw_ref[...]
    uc = sh(e, -1) * cw[0:1] + e * cw[1:2] + sh(e, 1) * cw[2:3] + cb_ref[...]
    uc = uc[halo:halo + t_rows]
    x0_ref[0] = uc[:, :HY_WIDTH]
    z_ref[0] = uc[:, HY_WIDTH:2 * HY_WIDTH] * uc[:, 2 * HY_WIDTH:]


def _seqmix(pph, wts):
    b, s, w = pph.shape
    t = min(ROW_TILE, s)
    halo = 8
    per = t // halo
    nh = s // halo
    full = lambda shape: pl.BlockSpec(shape, lambda bi, i: (0,) * len(shape))
    out = lambda: pl.BlockSpec((1, t, HY_WIDTH), lambda bi, i: (bi, i, 0))
    return pl.pallas_call(
        functools.partial(_seqmix_kernel, seq_len=s),
        grid=(b, s // t),
        in_specs=[pl.BlockSpec((1, halo, w), lambda bi, i: (bi, jnp.maximum(i * per - 1, 0), 0)),
                  pl.BlockSpec((1, t, w), lambda bi, i: (bi, i, 0)),
                  pl.BlockSpec((1, halo, w), lambda bi, i: (bi, jnp.minimum((i + 1) * per, nh - 1), 0)),
                  full((POOL_WIDTH, POOL_WIDTH)), full((1, POOL_WIDTH)), full((3, HY_IN)), full((1, HY_IN))],
        out_specs=[out(), out(), out()],
        out_shape=[jax.ShapeDtypeStruct((b, s, POOL_WIDTH), BF16),
                   jax.ShapeDtypeStruct((b, s, HY_WIDTH), F32),
                   jax.ShapeDtypeStruct((b, s, HY_WIDTH), F32)],
        compiler_params=_cparams("parallel", "parallel"),
        name="seqmix",
    )(pph, pph, pph, wts["wpool"], wts["pool_scale"], wts["conv_w"], wts["conv_b"])


def _dot3(m_ref, x):
    x_hi = x.astype(BF16)
    x_lo = (x - x_hi.astype(F32)).astype(BF16)
    m_hi, m_lo = m_ref[0], m_ref[1]
    return (jnp.dot(m_hi, x_hi, preferred_element_type=F32) + jnp.dot(m_hi, x_lo, preferred_element_type=F32)
            + jnp.dot(m_lo, x_hi, preferred_element_type=F32))


def _split_const(m):
    hi = m.astype(BF16)
    return jnp.stack([hi, (m - hi.astype(F32)).astype(BF16)])


def _leftmm_kernel(*refs, cmul, gate):
    m_ref, x_ref = refs[0], refs[1]
    nxt = 2
    x = x_ref[0]
    if cmul:
        ks = refs[nxt][...]
        nxt += 1
        kh = x.shape[0] // 2
        xr, xi, kr, ki = x[:kh], x[kh:], ks[:kh], ks[kh:]
        x = jnp.concatenate([xr * kr - xi * ki, xr * ki + xi * kr], axis=0)
    y = jnp.dot(m_ref[...], x, precision=HIGHEST, preferred_element_type=F32)
    if gate:
        z_ref, bias_ref, x0_ref = refs[nxt:nxt + 3]
        nxt += 3
        y = (y + z_ref[0] * bias_ref[...]) * x0_ref[0]
    o_ref = refs[nxt]
    o_ref[0] = y.astype(o_ref.dtype)


def _leftmm(m, x, spectrum=None, gate=None, out_dtype=F32):
    g, k, cols = x.shape
    mo = m.shape[0]
    tc = min(cols, 2048)
    args = [m, x]
    in_specs = [pl.BlockSpec((mo, k), lambda gi, j: (0, 0)), pl.BlockSpec((1, k, tc), lambda gi, j: (gi, 0, j))]
    if spectrum is not None:
        args.append(spectrum)
        in_specs.append(pl.BlockSpec((k, tc), lambda gi, j: (0, j)))
    if gate is not None:
        z, bias, x0 = gate
        args += [z, bias, x0]
        in_specs += [pl.BlockSpec((1, mo, tc), lambda gi, j: (gi, 0, j)),
                     pl.BlockSpec((1, tc), lambda gi, j: (0, j)),
                     pl.BlockSpec((1, mo, tc), lambda gi, j: (gi, 0, j))]
    return pl.pallas_call(
        functools.partial(_leftmm_kernel, cmul=spectrum is not None, gate=gate is not None),
        grid=(g, cols // tc),
        in_specs=in_specs,
        out_specs=pl.BlockSpec((1, mo, tc), lambda gi, j: (gi, 0, j)),
        out_shape=jax.ShapeDtypeStruct((g, mo, cols), out_dtype),
        compiler_params=_cparams("parallel", "parallel"),
        name="dft_leftmm",
    )(*args)


def _blockfft_kernel(*refs, n2, blocks, conv):
    if conv:
        a_ref, tw_ref, ks_ref, m2_ref, m2c_ref, o_ref = refs
    else:
        a_ref, tw_ref, m2_ref, o_ref = refs

    def body(j, carry):
        rows = pl.ds(j * n2, n2)
        ar, ai = a_ref[0, 0, rows, :], a_ref[0, 1, rows, :]
        tcs, tsn = tw_ref[0, rows, :], tw_ref[1, rows, :]
        tcs = jnp.concatenate([tcs] * (ar.shape[1] // tcs.shape[1]), axis=1)
        tsn = jnp.concatenate([tsn] * (ar.shape[1] // tsn.shape[1]), axis=1)
        bstk = jnp.concatenate([ar * tcs + ai * tsn, ai * tcs - ar * tsn], axis=0)
        xs = _dot3(m2_ref, bstk)
        if not conv:
            o_ref[0, 0, rows, :] = xs[:n2]
            o_ref[0, 1, rows, :] = xs[n2:]
            return carry
        xr, xi = xs[:n2], xs[n2:]
        kr, ki = ks_ref[0, rows, :], ks_ref[1, rows, :]
        ystk = jnp.concatenate([xr * kr - xi * ki, xr * ki + xi * kr], axis=0)
        cs = _dot3(m2c_ref, ystk)
        cr, ci = cs[:n2], cs[n2:]
        o_ref[0, 0, rows, :] = cr * tcs - ci * tsn
        o_ref[0, 1, rows, :] = cr * tsn + ci * tcs
        return carry

    for j in range(blocks):
        body(j, 0)


def _blockfft(a, tw, m2, spectrum=None, m2c=None):
    g, _, n, c = a.shape
    n2 = m2.shape[1] // 2
    blocks = min(8, n // n2)
    rows = blocks * n2
    conv = spectrum is not None
    args = [a, tw]
    in_specs = [pl.BlockSpec((1, 2, rows, c), lambda gi, j: (gi, 0, j, 0)),
                pl.BlockSpec((2, rows, LANE), lambda gi, j: (0, j, 0))]
    if conv:
        args.append(spectrum)
        in_specs.append(pl.BlockSpec((2, rows, c), lambda gi, j: (0, j, 0)))
    args.append(m2)
    in_specs.append(pl.BlockSpec(m2.shape, lambda gi, j: (0, 0, 0)))
    if conv:
        args.append(m2c)
        in_specs.append(pl.BlockSpec(m2c.shape, lambda gi, j: (0, 0, 0)))
    return pl.pallas_call(
        functools.partial(_blockfft_kernel, n2=n2, blocks=blocks, conv=conv),
        grid=(g, n // rows),
        in_specs=in_specs,
        out_specs=pl.BlockSpec((1, 2, rows, c), lambda gi, j: (gi, 0, j, 0)),
        out_shape=jax.ShapeDtypeStruct((g, 2, n, c), F32),
        compiler_params=_cparams("parallel", "parallel"),
        name="dft_block",
    )(*args)


def _filter_kernel(za_ref, zb_ref, w1a_ref, w1b_ref, b1_ref, w2_ref, b2_ref, w3a_ref, w3b_ref, fr_ref, dl_ref,
                   o_ref):
    i = pl.program_id(0)
    za, zb = za_ref[...], zb_ref[...]
    fr = fr_ref[...]
    dot = functools.partial(jnp.dot, precision=HIGHEST, preferred_element_type=F32)
    h = jnp.sin(fr * (dot(za, w1a_ref[...]) + dot(zb, w1b_ref[...]) + b1_ref[...]))
    h = jnp.sin(fr * (dot(h, w2_ref[...]) + b2_ref[...]))
    o_ref[0] = dot(h, w3a_ref[...]) * jnp.exp(-za[:, 0:1] * dl_ref[...])
    first = (i == 0) & (lax.broadcasted_iota(jnp.int32, (zb.shape[0], 1), 0) == 0)
    o_ref[1] = jnp.where(first, 0.0, dot(h, w3b_ref[...]) * jnp.exp(-zb[:, 0:1] * dl_ref[...]))


def _hyena_filter(seq_len, wts):
    n = 2 * seq_len
    j = jnp.arange(n)
    lag = jnp.clip(jnp.where(j < seq_len, j, n - j), 0, seq_len - 1).astype(F32)[:, None]
    t = lag / (seq_len - 1)
    wpos = (2.0 * math.pi / seq_len) * lag
    f = jnp.linspace(1e-4, HY_BANDS - 1, HY_BANDS, dtype=F32)[None, :]
    feat = jnp.concatenate([t, jnp.cos(f * wpos), -jnp.sin(f * wpos)], axis=-1)
    feat2 = jnp.pad(feat, ((0, 0), (0, LANE - HY_EMB)))
    rows = min(seq_len, 1024)
    half = seq_len // rows
    full = lambda shape: pl.BlockSpec(shape, lambda i: (0,) * len(shape))
    taps = pl.pallas_call(
        _filter_kernel,
        grid=(half,),
        in_specs=[pl.BlockSpec((rows, LANE), lambda i: (i, 0)), pl.BlockSpec((rows, LANE), lambda i: (half + i, 0)),
                  full((LANE, LANE)), full((LANE, LANE)), full((1, LANE)), full((LANE, LANE)), full((1, LANE)),
                  full((LANE, HY_WIDTH)), full((LANE, HY_WIDTH)), full((1, LANE)), full((1, HY_WIDTH))],
        out_specs=pl.BlockSpec((2, rows, HY_WIDTH), lambda i: (0, i, 0)),
        out_shape=jax.ShapeDtypeStruct((2, seq_len, HY_WIDTH), F32),
        compiler_params=_cparams("parallel"),
        name="hyena_filter",
    )(feat2, feat2, wts["f_w1a"], wts["f_w1b"], wts["f_b1"], wts["f_w2"], wts["f_b2"], wts["f_w3a"], wts["f_w3b"],
      wts["f_freq"], wts["f_delta"])
    return taps.reshape(n, HY_WIDTH)


def _cs(p, rows, cols):
    ang = 2.0 * np.pi * np.outer(np.arange(rows), np.arange(cols)) / p
    return np.cos(ang), np.sin(ang)


def _mat(blocks):
    return jnp.asarray(np.block(blocks), dtype=F32)


def _dftcols_kernel(*refs, gate):
    m_ref, x_ref = refs[0], refs[1]
    x = x_ref[0]
    k, nt, _ = x.shape
    xt = pltpu.einshape("knc->(nk)c", x)
    ys = [_dot3(m_ref, xt[n * k:(n + 1) * k]) for n in range(nt)]
    y = pltpu.einshape("(nk)c->knc", jnp.concatenate(ys, axis=0), n=nt)
    if gate:
        z_ref, bias_ref, x0_ref, o_ref = refs[2:]
        y = (y + z_ref[0] * bias_ref[...]) * x0_ref[0]
    else:
        o_ref = refs[2]
    o_ref[0] = y.astype(o_ref.dtype)


def _dftcols(m, x, gate=None, out_dtype=F32):
    g, k, n2, c = x.shape
    mo = m.shape[1]
    nt = 16 if n2 % 16 == 0 else 8
    blk = lambda rows: pl.BlockSpec((1, rows, nt, c), lambda gi, j: (gi, 0, j, 0))
    args = [m, x]
    in_specs = [pl.BlockSpec((2, mo, k), lambda gi, j: (0, 0, 0)), blk(k)]
    if gate is not None:
        z, bias, x0 = gate
        args += [z, bias, x0]
        in_specs += [blk(mo), pl.BlockSpec((1, c), lambda gi, j: (0, 0)), blk(mo)]
    return pl.pallas_call(
        functools.partial(_dftcols_kernel, gate=gate is not None),
        grid=(g, n2 // nt),
        in_specs=in_specs,
        out_specs=blk(mo),
        out_shape=jax.ShapeDtypeStruct((g, mo, n2, c), out_dtype),
        compiler_params=_cparams("parallel", "parallel"),
        name="dft_cols",
    )(*args)


def _hyena_long(z, x0, bias, filt):
    b, s, c = z.shape
    n = 2 * s
    n1 = 1 << ((n.bit_length() - 1 + 1) // 2)
    n2 = n // n1
    g = b // 2
    c1, s1 = _cs(n1, n1, n1)
    h1 = n1 // 2
    m_fwd = _split_const(_mat([[c1[:, :h1], s1[:, :h1]], [-s1[:, :h1], c1[:, :h1]]]))
    m_filt = _split_const(_mat([[c1], [-s1]]))
    m_inv = _split_const(_mat([[c1[:h1], -s1[:h1]], [s1[:h1], c1[:h1]]]) * (1.0 / n))
    c2, s2 = _cs(n2, n2, n2)
    m2 = _split_const(_mat([[c2, s2], [-s2, c2]]))
    m2c = _split_const(_mat([[c2, -s2], [s2, c2]]))
    k1 = jnp.arange(n1, dtype=jnp.int32)[:, None]
    nn2 = jnp.arange(n2, dtype=jnp.int32)[None, :]
    ang = (2.0 * math.pi / n) * (k1 * nn2).astype(F32).reshape(n)
    tw = jnp.broadcast_to(jnp.stack([jnp.cos(ang), jnp.sin(ang)])[:, :, None], (2, n, LANE))

    spec = _dftcols(m_filt, filt.reshape(1, n1, n2, c)).reshape(1, 2, n, c)
    spec = _blockfft(spec, tw, m2)[0]
    zv = z.reshape(g, n1, n2, c)
    a = _dftcols(m_fwd, zv).reshape(g, 2, n, c)
    dd = _blockfft(a, tw, m2, spectrum=spec, m2c=m2c).reshape(g, 2 * n1, n2, c)
    y = _dftcols(m_inv, dd, gate=(zv, bias, x0.reshape(g, n1, n2, c)), out_dtype=BF16)
    return y.reshape(b, s, c)


def _hyena_short(z, x0, bias, filt):
    b, s, c = z.shape
    n = 2 * s
    g = b // 2
    cf, sf = _cs(n, n, n)
    m_fwd = _mat([[cf[:, :s], sf[:, :s]], [-sf[:, :s], cf[:, :s]]])
    m_filt = _mat([[cf], [-sf]])
    m_inv = _mat([[cf[:s], -sf[:s]], [sf[:s], cf[:s]]]) * (1.0 / n)
    spec = _leftmm(m_filt, filt.reshape(1, n, c))[0]
    zv = z.reshape(g, n, c)
    zf = _leftmm(m_fwd, zv)
    y = _leftmm(m_inv, zf, spectrum=spec, gate=(zv, bias, x0.reshape(g, n, c)), out_dtype=BF16)
    return y.reshape(b, s, c)


def _attn_kernel(*refs, tk, n_chunks, heads, unroll):
    if n_chunks:
        q_ref, kc_ref, vc_ref, kl_ref, vl_ref, o_ref, s_ref = refs
    else:
        q_ref, kc_ref, vc_ref, o_ref = refs
    dn = (((1,), (1,)), ((), ()))
    qs = [q_ref[0, hh] for hh in range(heads)]

    def chunk_rows(j):
        return pl.ds(pl.multiple_of(j * tk, tk), tk)

    def scores(j, slot):
        for hh in range(heads):
            s_ref[slot, hh] = lax.dot_general(kl_ref[0, hh, chunk_rows(j), :], qs[hh], dn,
                                              preferred_element_type=F32)

    def absorb(j, slot, carry):
        out = []
        for hh in range(heads):
            m_old, acc_old = carry[hh]
            sj = s_ref[slot, hh]
            m_new = jnp.maximum(m_old, jnp.max(sj, axis=0, keepdims=True))
            pj = jnp.exp2(sj - m_new)
            acc_new = jnp.exp2(m_old - m_new) * acc_old + jnp.dot(vl_ref[0, hh, :, chunk_rows(j)], pj.astype(BF16),
                                                                  preferred_element_type=F32)
            out.append((m_new, acc_new))
        return tuple(out)

    if n_chunks:
        scores(0, 0)
    state = []
    for hh in range(heads):
        s = lax.dot_general(kc_ref[0, hh], qs[hh], dn, preferred_element_type=F32)
        m = jnp.max(s, axis=0, keepdims=True)
        p = jnp.exp2(s - m)
        state.append((m, jnp.dot(vc_ref[0, hh], p.astype(BF16), preferred_element_type=F32)))

    def body(jj, carry):
        j = unroll * jj
        for u in range(unroll):
            nxt = j + u + 1
            if u == unroll - 1:
                nxt = jnp.minimum(nxt, n_chunks - 1)
            scores(nxt, (u + 1) % 2)
            carry = absorb(j + u, u % 2, carry)
        return carry

    if n_chunks:
        state = lax.fori_loop(0, n_chunks // unroll, body, tuple(state))
    outs = [acc[:V_DIM] / acc[V_DIM:V_DIM + 1] for _, acc in state]
    o_ref[0] = jnp.concatenate(outs, axis=0).T.astype(o_ref.dtype)


def _attention(q, kc, vct, kl=None, vlt=None):
    b, h, s, dp = q.shape
    nc = kc.shape[2]
    tq = min(ROW_TILE, s)
    heads = LANE // V_DIM
    tk = 512
    n_chunks = 0 if kl is None else kl.shape[2] // tk
    unroll = 8 if n_chunks % 8 == 0 else 2
    args = [q, kc, vct]
    in_specs = [pl.BlockSpec((1, heads, tq, dp), lambda bi, hi, i: (bi, hi, i, 0)),
                pl.BlockSpec((1, heads, nc, dp), lambda bi, hi, i: (bi, hi, 0, 0)),
                pl.BlockSpec((1, heads, dp, nc), lambda bi, hi, i: (bi, hi, 0, 0))]
    scratch = []
    if n_chunks:
        assert unroll % 2 == 0 and n_chunks % unroll == 0
        sl = kl.shape[2]
        args += [kl, vlt]
        in_specs += [pl.BlockSpec((1, heads, sl, dp), lambda bi, hi, i: (bi, hi, 0, 0)),
                     pl.BlockSpec((1, heads, dp, sl), lambda bi, hi, i: (bi, hi, 0, 0))]
        scratch = [pltpu.VMEM((2, heads, tk, tq), F32)]
    return pl.pallas_call(
        functools.partial(_attn_kernel, tk=tk, n_chunks=n_chunks, heads=heads, unroll=unroll),
        grid=(b, h // heads, s // tq),
        in_specs=in_specs,
        out_specs=pl.BlockSpec((1, tq, heads * V_DIM), lambda bi, hi, i: (bi, i, hi)),
        out_shape=jax.ShapeDtypeStruct((b, s, h * V_DIM), BF16),
        scratch_shapes=scratch,
        compiler_params=_cparams("parallel", "parallel", "parallel"),
        name="attention",
    )(*args)


def _outproj_kernel(x_ref, pool_ref, hy_ref, at_ref, wout_ref, g1_ref, gn_ref, sc_ref, sh_ref, rw_ref, rb_ref,
                    *rest):
    x1_ref, h2_ref, lg_ref = rest[-3:]
    mix = jnp.concatenate([pool_ref[0], hy_ref[0], at_ref[0]], axis=-1)
    y = jnp.dot(mix, wout_ref[...], preferred_element_type=F32)
    x1 = x_ref[0] + g1_ref[0] * y
    x1_ref[0] = x1
    h2 = _rms(x1, gn_ref[...], D_MODEL) * (1.0 + sc_ref[0]) + sh_ref[0]
    h2_ref[...] = _pack_rows(h2)
    h_hi = h2.astype(BF16)
    h_lo = (h2 - h_hi.astype(F32)).astype(BF16)
    nt = functools.partial(lax.dot_general, dimension_numbers=(((1,), (1,)), ((), ())), preferred_element_type=F32)
    lg_ref[...] = nt(rw_ref[0], h_hi) + nt(rw_ref[0], h_lo) + nt(rw_ref[1], h_hi) + rb_ref[...]


def _outproj(x, pool, hy, attn, seg_of_batch, wts, mod, n_tok, tok_offset, prior=None):
    b, s, d = x.shape
    t = min(ROW_TILE, s)
    per = s // t
    off = tok_offset // t
    full = lambda shape: pl.BlockSpec(shape, lambda bi, i: (0,) * len(shape))
    seg = lambda: pl.BlockSpec((1, 1, d), lambda bi, i: (seg_of_batch(bi), 0, 0))
    rowblk = lambda w: pl.BlockSpec((1, t, w), lambda bi, i: (bi, i, 0))
    args = [x, pool, hy, attn, wts["w_out"], mod["g1"], wts["norm2_g"], mod["sc2"], mod["sh2"],
            wts["router_w"], wts["router_b"]]
    in_specs = [rowblk(d), rowblk(POOL_WIDTH), rowblk(HY_WIDTH), rowblk(MLA_WIDTH),
                full((d, d)), seg(), full((1, d)), seg(), seg(), full((2, N_EXPERTS, d)), full((N_EXPERTS, 1))]
    aliases = {}
    if prior is not None:
        aliases = {len(args): 1, len(args) + 1: 2}
        args += list(prior)
        in_specs += [pl.BlockSpec(memory_space=pl.ANY)] * 2
    return pl.pallas_call(
        _outproj_kernel,
        grid=(b, per),
        in_specs=in_specs,
        out_specs=[rowblk(d), pl.BlockSpec((t, d // 2), lambda bi, i: (off + bi * per + i, 0)),
                   pl.BlockSpec((N_EXPERTS, t), lambda bi, i: (0, off + bi * per + i))],
        out_shape=[jax.ShapeDtypeStruct((b, s, d), F32), jax.ShapeDtypeStruct((n_tok, d // 2), jnp.int32),
                   jax.ShapeDtypeStruct((N_EXPERTS, n_tok), F32)],
        input_output_aliases=aliases,
        compiler_params=_cparams("parallel", "parallel"),
        name="outproj",
    )(*args)


def _moe_kernel(be_ref, br_ref, x_ref, w1_ref, b1_ref, w2_ref, b2_ref, o_ref, w1b_ref, w2b_ref):
    i = pl.program_id(0)
    e = be_ref[i]
    e_prev = be_ref[jnp.maximum(i - 1, 0)]
    n_rows = br_ref[i]

    @pl.when((i == 0) | (e != e_prev))
    def _():
        w1b_ref[...] = w1_ref[0, 0].astype(BF16)
        w2b_ref[...] = w2_ref[0, 0].astype(BF16)

    @pl.when(n_rows > 0)
    def _():
        live = lax.broadcasted_iota(jnp.int32, (x_ref.shape[0], 1), 0) < n_rows
        x = _unpack_rows(jnp.where(live, x_ref[...], 0))
        a = jnp.dot(x, w1b_ref[...], preferred_element_type=F32) + b1_ref[0, 0]
        glu = jnp.minimum(a[:, :D_EXPERT], SWIGLU_LIMIT)
        lin = jnp.clip(a[:, D_EXPERT:], -SWIGLU_LIMIT, SWIGLU_LIMIT)
        act = glu * (1.0 / (1.0 + jnp.exp(-SWIGLU_ALPHA * glu))) * (lin + 1.0)
        y = jnp.dot(act.astype(BF16), w2b_ref[...], preferred_element_type=F32) + b2_ref[0, 0]
        o_ref[...] = _pack_rows(y)

    @pl.when(n_rows == 0)
    def _():
        o_ref[...] = jnp.zeros_like(o_ref)


def _experts(xs, block_e, block_rows, w1, b1, w2, b2, layer):
    n_buf, dh = xs.shape
    d = 2 * dh
    tm = EXPERT_TILE
    n_blocks = n_buf // tm
    de2 = w1.shape[3]
    grid_spec = pltpu.PrefetchScalarGridSpec(
        num_scalar_prefetch=2,
        grid=(n_blocks,),
        in_specs=[pl.BlockSpec((tm, dh), lambda i, be, nu: (i, 0)),
                  pl.BlockSpec((1, 1, d, de2), lambda i, be, nu: (layer, be[i], 0, 0)),
                  pl.BlockSpec((1, 1, 1, de2), lambda i, be, nu: (layer, be[i], 0, 0)),
                  pl.BlockSpec((1, 1, D_EXPERT, d), lambda i, be, nu: (layer, be[i], 0, 0)),
                  pl.BlockSpec((1, 1, 1, d), lambda i, be, nu: (layer, be[i], 0, 0))],
        out_specs=pl.BlockSpec((tm, dh), lambda i, be, nu: (i, 0)),
        scratch_shapes=[pltpu.VMEM((d, de2), BF16), pltpu.VMEM((D_EXPERT, d), BF16)],
    )
    return pl.pallas_call(
        _moe_kernel,
        grid_spec=grid_spec,
        out_shape=jax.ShapeDtypeStruct((n_buf, dh), jnp.int32),
        compiler_params=_cparams("arbitrary"),
        name="experts",
    )(block_e, block_rows, xs, w1, b1, w2, b2)


def _router_kernel(lg_ref, idx_ref, gate_ref, rank_ref, cnt_ref, base_ref):
    i = pl.program_id(0)
    n_e, t = lg_ref.shape

    @pl.when(i == 0)
    def _():
        base_ref[...] = jnp.zeros_like(base_ref)

    work = lg_ref[...]
    eid = lax.broadcasted_iota(jnp.int32, (n_e, t), 0)
    vals, idxs, sels = [], [], []
    for _ in range(TOP_K):
        m = jnp.max(work, axis=0, keepdims=True)
        idx = jnp.min(jnp.where(work == m, eid, n_e), axis=0, keepdims=True)
        sel = eid == idx
        work = jnp.where(sel, -jnp.inf, work)
        vals.append(m)
        idxs.append(idx)
        sels.append(sel)
    cnt = jnp.where(sels[0], 1.0, 0.0)
    for sel in sels[1:]:
        cnt = cnt + jnp.where(sel, 1.0, 0.0)
    earlier = (lax.broadcasted_iota(jnp.int32, (t, t), 0) < lax.broadcasted_iota(jnp.int32, (t, t), 1))
    within = jnp.dot(cnt.astype(BF16), jnp.where(earlier, 1.0, 0.0).astype(BF16), preferred_element_type=F32)
    pref = base_ref[:, 0:1] + within
    ranks = [jnp.sum(jnp.where(sel, pref, 0.0), axis=0, keepdims=True) for sel in sels]
    new_base = base_ref[...] + jnp.sum(cnt, axis=1, keepdims=True)
    base_ref[...] = new_base
    cnt_ref[...] = new_base
    ex = [jnp.exp(v - vals[0]) for v in vals]
    den = ex[0]
    for e in ex[1:]:
        den = den + e
    idx_ref[...] = jnp.concatenate(idxs, axis=0)
    gate_ref[...] = jnp.concatenate([e / den for e in ex], axis=0)
    rank_ref[...] = jnp.concatenate(ranks, axis=0).astype(jnp.int32)


def _route(logits_t):
    n_e, n_tok = logits_t.shape
    tm = EXPERT_TILE
    t = ROW_TILE
    tokblk = lambda: pl.BlockSpec((TOP_K, t), lambda i: (0, i))
    idx_t, gates_t, rank_t, counts_f = pl.pallas_call(
        _router_kernel,
        grid=(n_tok // t,),
        in_specs=[pl.BlockSpec((n_e, t), lambda i: (0, i))],
        out_specs=[tokblk(), tokblk(), tokblk(), pl.BlockSpec((n_e, LANE), lambda i: (0, 0))],
        out_shape=[jax.ShapeDtypeStruct((TOP_K, n_tok), jnp.int32), jax.ShapeDtypeStruct((TOP_K, n_tok), F32),
                   jax.ShapeDtypeStruct((TOP_K, n_tok), jnp.int32), jax.ShapeDtypeStruct((n_e, LANE), F32)],
        scratch_shapes=[pltpu.VMEM((n_e, LANE), F32)],
        compiler_params=_cparams("arbitrary"),
        name="router",
    )(logits_t)
    n_slots = n_tok * TOP_K
    counts = counts_f[:, 0].astype(jnp.int32)
    padded = (counts + tm - 1) // tm * tm
    padded_end = jnp.cumsum(padded)
    padded_start = padded_end - padded
    experts = jnp.arange(n_e, dtype=jnp.int32)
    dest_t = rank_t + jnp.sum(jnp.where(idx_t[None] == experts[:, None, None], padded_start[:, None, None], 0), axis=0)
    n_blocks = n_slots // tm + n_e
    block_start = jnp.arange(n_blocks, dtype=jnp.int32) * tm
    block_e = jnp.minimum(jnp.sum(padded_end[None, :] <= block_start[:, None], axis=1), n_e - 1).astype(jnp.int32)
    sel = (block_e[:, None] == experts[None, :])
    pick = lambda v: jnp.sum(jnp.where(sel, v[None, :], 0), axis=1)
    block_rows = jnp.clip(pick(padded_start) + pick(counts) - block_start, 0, tm).astype(jnp.int32)
    return gates_t.T, dest_t, block_e, block_rows


SC_WINDOW = 32
SC_SLOTS = 4


def _sc_gather(table, idx):
    _, d = table.shape
    b = idx.shape[0]
    info = plsc.get_sparse_core_info()
    nc, ns = info.num_cores, info.num_subcores
    nw = nc * ns
    w, nslot = SC_WINDOW, SC_SLOTS
    per_w = b // nw
    n_win = per_w // w
    assert per_w * nw == b and n_win * w == per_w and n_win % nslot == 0
    mesh = plsc.VectorSubcoreMesh(core_axis_name="c", subcore_axis_name="s")

    def body(table_hbm, idx_hbm, out_hbm, idx_v, rows_v, gsem, osem):
        wid = lax.axis_index("s") * nc + lax.axis_index("c")
        base = wid * per_w
        pltpu.sync_copy(idx_hbm.at[wid], idx_v)

        def gather(g, slot):
            return pltpu.make_async_copy(table_hbm.at[idx_v.at[g]], rows_v.at[slot], gsem.at[slot])

        def put(g, slot):
            return pltpu.make_async_copy(rows_v.at[slot], out_hbm.at[pl.ds(base + g * w, w)], osem.at[slot])

        for g in range(nslot - 1):
            gather(g, g).start()

        @pl.loop(0, n_win, step=nslot)
        def _(g0):
            for slot in range(nslot):
                g = g0 + slot
                gather(g, slot).wait()
                put(g, slot).start()
                free = (slot - 1) % nslot

                @pl.when(g + nslot - 1 < n_win)
                def _():
                    @pl.when(g >= 1)
                    def _():
                        put(g - 1, free).wait()

                    gather(g + nslot - 1, free).start()

        for g in range(n_win - nslot, n_win):
            put(g, g % nslot).wait()

    return pl.kernel(
        body,
        out_type=jax.ShapeDtypeStruct((b, d), table.dtype),
        mesh=mesh,
        scratch_types=[pltpu.VMEM((n_win, w), jnp.int32), pltpu.VMEM((nslot, w, d), table.dtype),
                       pltpu.SemaphoreType.DMA((nslot,)), pltpu.SemaphoreType.DMA((nslot,))],
        name="sc_gather",
    )(table, idx.reshape(nw, n_win, w))


SC_SCATTER_WINDOW = 16


def _sc_scatter(rows, dest_t, n_out):
    n_tok, d = rows.shape
    k = dest_t.shape[0]
    info = plsc.get_sparse_core_info()
    nc, ns = info.num_cores, info.num_subcores
    nw = nc * ns
    w = SC_SCATTER_WINDOW
    per_w = n_tok // nw
    n_win = per_w // w
    assert per_w * nw == n_tok and n_win * w == per_w and n_win % 2 == 0
    mesh = plsc.VectorSubcoreMesh(core_axis_name="c", subcore_axis_name="s")
    idx = dest_t.reshape(k, nw, n_win, w).transpose(1, 0, 2, 3)

    def body(rows_hbm, idx_hbm, out_hbm, idx_v, rows_v, lsem, ssem):
        wid = lax.axis_index("s") * nc + lax.axis_index("c")
        base = wid * per_w
        pltpu.sync_copy(idx_hbm.at[wid], idx_v)

        def load(g, slot):
            return pltpu.make_async_copy(rows_hbm.at[pl.ds(base + g * w, w)], rows_v.at[slot], lsem.at[slot])

        def scatter(g, slot, j):
            return pltpu.make_async_copy(rows_v.at[slot], out_hbm.at[idx_v.at[j, g]], ssem.at[slot])

        load(0, 0).start()

        @pl.loop(0, n_win, step=2)
        def _(g0):
            for slot in range(2):
                g = g0 + slot
                load(g, slot).wait()
                for j in range(k):
                    scatter(g, slot, j).start()

                @pl.when(g + 1 < n_win)
                def _():
                    @pl.when(g >= 1)
                    def _():
                        for j in range(k):
                            scatter(g - 1, 1 - slot, j).wait()

                    load(g + 1, 1 - slot).start()

        for g in (n_win - 2, n_win - 1):
            for j in range(k):
                scatter(g, g % 2, j).wait()

    return pl.kernel(
        body,
        out_type=jax.ShapeDtypeStruct((n_out, d), rows.dtype),
        mesh=mesh,
        scratch_types=[pltpu.VMEM((k, n_win, w), jnp.int32), pltpu.VMEM((2, w, d), rows.dtype),
                       pltpu.SemaphoreType.DMA((2,)), pltpu.SemaphoreType.DMA((2,))],
        name="sc_scatter",
    )(rows, idx)


def _combine_kernel(x_ref, y_ref, gt_ref, g2_ref, o_ref):
    gt = gt_ref[...]
    acc = gt[:, 0:1] * _unpack_rows(y_ref[0]).astype(F32)
    for j in range(1, TOP_K):
        acc = acc + gt[:, j:j + 1] * _unpack_rows(y_ref[j]).astype(F32)
    o_ref[0] = x_ref[0] + g2_ref[0] * acc


def _combine(x, ysel, gates, tok_offset, seg_of_batch, mod):
    b, s, d = x.shape
    t = min(ROW_TILE, s)
    per = s // t
    off = tok_offset // t
    return pl.pallas_call(
        _combine_kernel,
        grid=(b, per),
        in_specs=[pl.BlockSpec((1, t, d), lambda bi, i: (bi, i, 0)),
                  pl.BlockSpec((TOP_K, t, d // 2), lambda bi, i: (0, off + bi * per + i, 0)),
                  pl.BlockSpec((t, TOP_K), lambda bi, i: (off + bi * per + i, 0)),
                  pl.BlockSpec((1, 1, d), lambda bi, i: (seg_of_batch(bi), 0, 0))],
        out_specs=pl.BlockSpec((1, t, d), lambda bi, i: (bi, i, 0)),
        out_shape=jax.ShapeDtypeStruct((b, s, d), F32),
        compiler_params=_cparams("parallel", "parallel"),
        name="combine",
    )(x, ysel, gates, mod["g2"])


def _rope_tables(seq_len):
    n_rows = seq_len // GRID_W
    row = jnp.repeat(jnp.arange(n_rows, dtype=F32), GRID_W)
    col = jnp.tile(jnp.arange(GRID_W, dtype=F32), n_rows)
    pairs = ROPE // 4
    inv = ROPE_BASE ** (-jnp.arange(pairs, dtype=F32) / pairs)
    ang = jnp.concatenate([row[:, None] * inv, col[:, None] * inv], axis=-1)
    cos, sin = jnp.cos(ang), jnp.sin(ang)
    one = jnp.ones((seq_len, NOPE), F32)
    zn = jnp.zeros((seq_len, NOPE), F32)
    zp = jnp.zeros((seq_len, HEAD_PAD - QK_DIM), F32)
    return jnp.concatenate([one, cos, cos, zp], axis=-1), jnp.concatenate([zn, sin, sin, zp], axis=-1)


def _identity_rope(seq_len):
    keep = jnp.concatenate([jnp.ones((seq_len, QK_DIM), F32), jnp.zeros((seq_len, HEAD_PAD - QK_DIM), F32)], axis=-1)
    return keep, jnp.zeros((seq_len, HEAD_PAD), F32)


def _rotate_half(w):
    half = ROPE // 2
    x1, x2 = w[..., NOPE:NOPE + half], w[..., NOPE + half:QK_DIM]
    return jnp.concatenate([jnp.zeros_like(w[..., :NOPE]), -x2, x1, jnp.zeros_like(w[..., QK_DIM:])], axis=-1)


def _swap_rope_halves(g):
    half = ROPE // 2
    lo, hi = g[..., NOPE:NOPE + half], g[..., NOPE + half:QK_DIM]
    return jnp.concatenate([g[..., :NOPE], hi, lo, g[..., QK_DIM:]], axis=-1)


def _pad_cols(w, n):
    return jnp.pad(w, ((0, 0), (0, n - w.shape[1])))


def _layer_weights(l, p):
    hw = HEADS * HEAD_PAD
    wq = jnp.pad(p["mla_w_uq"][l].reshape(Q_RANK, HEADS, QK_DIM), ((0, 0), (0, 0), (0, HEAD_PAD - QK_DIM)))
    wukv = p["mla_w_ukv"][l].reshape(KV_RANK, HEADS, NOPE + V_DIM)
    wk = jnp.pad(wukv[:, :, :NOPE], ((0, 0), (0, 0), (0, HEAD_PAD - NOPE)))
    wv = jnp.pad(wukv[:, :, NOPE:], ((0, 0), (0, 0), (0, HEAD_PAD - V_DIM))).reshape(KV_RANK, hw)
    place = jnp.pad(jnp.eye(ROPE, dtype=F32), ((0, KV_RANK - ROPE), (NOPE, HEAD_PAD - QK_DIM)))
    wk = jnp.concatenate([wk, jnp.tile(place[:, None, :], (1, HEADS, 1))], axis=0)
    with_rot = lambda w: jnp.concatenate([w.reshape(-1, hw), _rotate_half(w).reshape(-1, hw)], axis=1).astype(BF16)
    wpool = jax.scipy.linalg.block_diag(*[p["pool_w"][l, g] for g in range(POOL_GROUPS)])
    vone = (jnp.arange(HEAD_PAD) == V_DIM).astype(F32)[:, None]
    deltas = jnp.abs(jnp.linspace(HY_MIN_DECAY, HY_MAX_DECAY, HY_WIDTH, dtype=F32))[None, :]
    row = lambda v, n=None: (v if n is None else jnp.pad(v, (0, n - v.shape[0])))[None, :]
    twice = lambda v: jnp.concatenate([v, v])[None, :]
    return {
        "norm1_g": row(p["norm1_g"][l]), "norm2_g": row(p["norm2_g"][l]),
        "w_in": _pad_cols(p["w_in"][l], N_IN_PAD).astype(BF16),
        "gq": row(p["mla_q_norm_g"][l]), "wq": with_rot(wq),
        "gkv": row(p["mla_kv_norm_g"][l]), "wk": with_rot(wk), "wvt": wv.T.astype(BF16),
        "gqq": row(p["qk_norm_q"][l], HEAD_PAD), "gqq_rot": _swap_rope_halves(row(p["qk_norm_q"][l], HEAD_PAD)),
        "gqk": row(p["qk_norm_k"][l], HEAD_PAD), "gqk_rot": _swap_rope_halves(row(p["qk_norm_k"][l], HEAD_PAD)),
        "vone": vone,
        "wpool": wpool.astype(BF16), "pool_scale": row(p["pool_scale"][l]),
        "conv_w": p["hy_conv_w"][l], "conv_b": row(p["hy_conv_b"][l]),
        "f_w1a": jnp.pad(p["hy_f_w1"][l], ((0, LANE - HY_EMB), (0, LANE - HY_HID))),
        "f_w1b": jnp.pad(p["hy_f_w1"][l], ((0, LANE - HY_EMB), (LANE - HY_HID, 0))),
        "f_b1": twice(p["hy_f_b1"][l]),
        "f_w2": jax.scipy.linalg.block_diag(p["hy_f_w2"][l], p["hy_f_w2"][l]),
        "f_b2": twice(p["hy_f_b2"][l]),
        "f_w3a": jnp.pad(p["hy_f_w3"][l][:, :HY_WIDTH], ((0, LANE - HY_HID), (0, 0))),
        "f_w3b": jnp.pad(p["hy_f_w3"][l][:, HY_WIDTH:], ((LANE - HY_HID, 0), (0, 0))),
        "f_freq": twice(p["hy_freq"][l]), "f_delta": deltas,
        "hy_bias": row(p["hy_bias"][l]),
        "w_out": p["w_out"][l].astype(BF16),
        "router_w": _split_const(p["router_w"][l].T), "router_b": p["router_b"][l][:, None],
    }


def _layer(l, x, xc, cc, wts, p, rope_main, rope_ctx, last):
    b, s, d = x.shape
    n_ctx = xc.shape[1]
    modm = _modulation(cc, p["w_mod"], p["b_mod"][l][None, :], l)
    names = ("sh1", "sc1", "g1", "sh2", "sc2", "g2")
    mod = {nm: modm[:, i * d:(i + 1) * d].reshape(N_SEG, 1, d) for i, nm in enumerate(names)}
    main_seg = lambda bi: bi
    ctx_seg = lambda bi: CTX_SEG

    pph, q, kl, vl = _inproj(x, main_seg, wts, mod, rope_main)
    pph_c, qc, kc, vc = _inproj(xc, ctx_seg, wts, mod, rope_ctx)
    attn = _attention(q, kc, vc, kl, vl)
    pool, x0, z = _seqmix(pph, wts)
    hy = _hyena_long(z, x0, wts["hy_bias"], _hyena_filter(s, wts))
    n_c = 0 if last else b * n_ctx
    n_tok = n_c + b * s
    init = None if last else (jnp.zeros((n_tok, d // 2), jnp.int32), jnp.zeros((N_EXPERTS, n_tok), F32))
    x1, h2, logits = _outproj(x, pool, hy, attn, main_seg, wts, mod, n_tok, n_c, prior=init)
    if not last:
        attn_c = _attention(qc, kc, vc)
        pool_c, x0c, zc = _seqmix(pph_c, wts)
        hy_c = _hyena_short(zc, x0c, wts["hy_bias"], _hyena_filter(n_ctx, wts))
        xc1, h2, logits = _outproj(xc, pool_c, hy_c, attn_c, ctx_seg, wts, mod, n_tok, 0, prior=(h2, logits))

    gates, dest_t, block_e, block_rows = _route(logits)
    de2 = p["moe_w1"].shape[-1]
    xs = _sc_scatter(h2, dest_t, block_e.shape[0] * EXPERT_TILE)
    ys = _experts(xs, block_e, block_rows, p["moe_w1"], p["moe_b1"].reshape(-1, N_EXPERTS, 1, de2),
                  p["moe_w2"], p["moe_b2"].reshape(-1, N_EXPERTS, 1, d), l)
    ysel = _sc_gather(ys, dest_t.reshape(-1)).reshape(TOP_K, -1, d // 2)
    x2 = _combine(x1, ysel, gates, n_c, main_seg, mod)
    if not last:
        xc = _combine(xc1, ysel, gates, 0, ctx_seg, mod)
    return x2, xc


def kernel(x, c, ctx, c_ctx, norm1_g, norm2_g, w_mod, b_mod, w_in, pool_w, pool_scale, hy_conv_w, hy_conv_b,
           hy_f_w1, hy_f_b1, hy_f_w2, hy_f_b2, hy_f_w3, hy_freq, hy_bias, mla_q_norm_g, mla_w_uq, mla_kv_norm_g,
           mla_w_ukv, qk_norm_q, qk_norm_k, w_out, router_w, router_b, moe_w1, moe_b1, moe_w2, moe_b2):
    p = dict(norm1_g=norm1_g, norm2_g=norm2_g, w_mod=w_mod, b_mod=b_mod, w_in=w_in, pool_w=pool_w,
             pool_scale=pool_scale,
             hy_conv_w=hy_conv_w, hy_conv_b=hy_conv_b, hy_f_w1=hy_f_w1, hy_f_b1=hy_f_b1, hy_f_w2=hy_f_w2,
             hy_f_b2=hy_f_b2, hy_f_w3=hy_f_w3, hy_freq=hy_freq, hy_bias=hy_bias, mla_q_norm_g=mla_q_norm_g,
             mla_w_uq=mla_w_uq, mla_kv_norm_g=mla_kv_norm_g, mla_w_ukv=mla_w_ukv, qk_norm_q=qk_norm_q,
             qk_norm_k=qk_norm_k, w_out=w_out, router_w=router_w, router_b=router_b, moe_w1=moe_w1,
             moe_b1=moe_b1, moe_w2=moe_w2, moe_b2=moe_b2)
    b, s, d = x.shape
    depth = w_mod.shape[0]
    assert b <= CTX_SEG
    cc = jnp.concatenate([c, jnp.zeros((CTX_SEG - b, d), F32), c_ctx[None, :],
                          jnp.zeros((N_SEG - CTX_SEG - 1, d), F32)], axis=0)
    rope_main = _rope_tables(s)
    rope_ctx = _identity_rope(ctx.shape[1])
    xc = ctx
    for l in range(depth):
        wts = _layer_weights(l, p)
        x, xc = _layer(l, x, xc, cc, wts, p, rope_main, rope_ctx, l == depth - 1)
    return x
```

```python
import functools
import math

import jax
import jax.numpy as jnp
import numpy as np
from jax import lax
from jax.experimental import pallas as pl
from jax.experimental.pallas import tpu as pltpu
from jax.experimental.pallas import tpu_sc as plsc

F32 = jnp.float32
BF16 = jnp.bfloat16
HIGHEST = lax.Precision.HIGHEST

D_MODEL = 1024
GRID_W = 64
EPS = 1e-6
POOL_WIDTH = 256
POOL_GROUPS = 4
POOL_GROUP_DIM = POOL_WIDTH // POOL_GROUPS
HY_WIDTH = 256
HY_IN = 3 * HY_WIDTH
HY_BANDS = 8
HY_EMB = 1 + 2 * HY_BANDS
HY_HID = 64
HY_MIN_DECAY = math.log(1e-2) / 1.5
HY_MAX_DECAY = math.log(1e-2) / 0.3
HEADS = 8
Q_RANK = 256
KV_RANK = 128
NOPE = 64
ROPE = 32
V_DIM = 64
QK_DIM = NOPE + ROPE
MLA_WIDTH = HEADS * V_DIM
ATTN_SCALE = QK_DIM ** -0.5
LOG2E = math.log2(math.e)
OFF_HY = POOL_WIDTH
OFF_Q = OFF_HY + HY_IN
OFF_KV = OFF_Q + Q_RANK
OFF_KR = OFF_KV + KV_RANK
N_IN = OFF_KR + ROPE
N_IN_PAD = OFF_KR + 128
ROPE_BASE = 10000.0
N_EXPERTS = 32
TOP_K = 4
D_EXPERT = 1024
SWIGLU_LIMIT = 7.0
SWIGLU_ALPHA = 1.702

LANE = 128
HEAD_PAD = LANE
ROW_TILE = 512
EXPERT_TILE = 512
VMEM_LIMIT = 56 * 1024 * 1024
N_SEG = 8
CTX_SEG = 4


def _cparams(*sem):
    return pltpu.CompilerParams(dimension_semantics=sem, vmem_limit_bytes=VMEM_LIMIT)


def _rms(x, g, n):
    ss = jnp.sum(x * x, axis=-1, keepdims=True) * (1.0 / n)
    return x * lax.rsqrt(ss + EPS) * g


def _pack_rows(x):
    half = x.shape[1] // 2
    bits = lax.bitcast_convert_type(x.astype(BF16).astype(F32), jnp.int32)
    return bits[:, :half] | lax.shift_right_logical(bits[:, half:], 16)


def _unpack_rows(w):
    hi = lax.bitcast_convert_type(w & jnp.int32(-65536), F32)
    lo = lax.bitcast_convert_type(lax.shift_left(w, 16), F32)
    return jnp.concatenate([hi, lo], axis=1).astype(BF16)


def _mod_kernel(c_ref, w_ref, b_ref, o_ref):
    c = c_ref[...]
    s = c * (1.0 / (1.0 + jnp.exp(-c)))
    o_ref[...] = jnp.dot(s, w_ref[0], precision=HIGHEST, preferred_element_type=F32) + b_ref[...]


def _modulation(cc, w, b, layer):
    _, d, n = w.shape
    tn = 1536
    return pl.pallas_call(
        _mod_kernel,
        grid=(n // tn,),
        in_specs=[pl.BlockSpec((N_SEG, d), lambda j: (0, 0)),
                  pl.BlockSpec((1, d, tn), lambda j: (layer, 0, j)),
                  pl.BlockSpec((1, tn), lambda j: (0, j))],
        out_specs=pl.BlockSpec((N_SEG, tn), lambda j: (0, j)),
        out_shape=jax.ShapeDtypeStruct((N_SEG, n), F32),
        compiler_params=_cparams("arbitrary"),
        name="modulation",
    )(cc, w, b)


def _inproj_kernel(x_ref, gn_ref, sc_ref, sh_ref, win_ref, gq_ref, wq_ref, gkv_ref, wk_ref, wvt_ref,
                   gqq_ref, gqqr_ref, gqk_ref, gqkr_ref, vone_ref, rc_ref, rs_ref,
                   pph_ref, q_ref, k_ref, vt_ref):
    x = x_ref[0]
    h = _rms(x, gn_ref[...], D_MODEL) * (1.0 + sc_ref[0]) + sh_ref[0]
    p = jnp.dot(h.astype(BF16), win_ref[...], preferred_element_type=F32)
    pph_ref[0] = p[:, :OFF_Q]
    cq = _rms(p[:, OFF_Q:OFF_KV], gq_ref[...], Q_RANK)
    qall = jnp.dot(cq.astype(BF16), wq_ref[...], preferred_element_type=F32)
    ckv_b = _rms(p[:, OFF_KV:OFF_KR], gkv_ref[...], KV_RANK).astype(BF16)
    kin = jnp.concatenate([ckv_b, p[:, OFF_KR:N_IN_PAD].astype(BF16)], axis=-1)
    kall = jnp.dot(kin, wk_ref[...], preferred_element_type=F32)
    vt_all = lax.dot_general(wvt_ref[...], ckv_b, (((1,), (1,)), ((), ())), preferred_element_type=F32)
    rc, rs = rc_ref[...], rs_ref[...]
    hw = HEADS * HEAD_PAD

    def norm_rope(x_all, hd, g_ref, gr_ref):
        xh = x_all[:, HEAD_PAD * hd:HEAD_PAD * (hd + 1)]
        xr = x_all[:, hw + HEAD_PAD * hd:hw + HEAD_PAD * (hd + 1)]
        r = lax.rsqrt(jnp.sum(xh * xh, axis=-1, keepdims=True) * (1.0 / QK_DIM) + EPS)
        return (xh * (g_ref[...] * rc) + xr * (gr_ref[...] * rs)) * r

    for hd in range(HEADS):
        q_ref[0, hd] = (norm_rope(qall, hd, gqq_ref, gqqr_ref) * (ATTN_SCALE * LOG2E)).astype(BF16)
        k_ref[0, hd] = norm_rope(kall, hd, gqk_ref, gqkr_ref).astype(BF16)
        vt_ref[0, hd] = (vt_all[HEAD_PAD * hd:HEAD_PAD * (hd + 1)] + vone_ref[...]).astype(BF16)


def _inproj(x, seg_of_batch, wts, mod, rope_tabs):
    b, s, d = x.shape
    t = min(ROW_TILE, s)
    full = lambda shape: pl.BlockSpec(shape, lambda bi, i: (0,) * len(shape))
    seg = lambda: pl.BlockSpec((1, 1, d), lambda bi, i: (seg_of_batch(bi), 0, 0))
    tab = lambda: pl.BlockSpec((t, HEAD_PAD), lambda bi, i: (i, 0))
    hw = HEADS * HEAD_PAD
    head_out = lambda: pl.BlockSpec((1, HEADS, t, HEAD_PAD), lambda bi, i: (bi, 0, i, 0))
    return pl.pallas_call(
        _inproj_kernel,
        grid=(b, s // t),
        in_specs=[pl.BlockSpec((1, t, d), lambda bi, i: (bi, i, 0)),
                  full((1, d)), seg(), seg(),
                  full((d, N_IN_PAD)), full((1, Q_RANK)), full((Q_RANK, 2 * hw)),
                  full((1, KV_RANK)), full((2 * KV_RANK, 2 * hw)), full((hw, KV_RANK)),
                  full((1, HEAD_PAD)), full((1, HEAD_PAD)), full((1, HEAD_PAD)), full((1, HEAD_PAD)),
                  full((HEAD_PAD, 1)), tab(), tab()],
        out_specs=[pl.BlockSpec((1, t, OFF_Q), lambda bi, i: (bi, i, 0)),
                   head_out(), head_out(),
                   pl.BlockSpec((1, HEADS, HEAD_PAD, t), lambda bi, i: (bi, 0, 0, i))],
        out_shape=[jax.ShapeDtypeStruct((b, s, OFF_Q), F32),
                   jax.ShapeDtypeStruct((b, HEADS, s, HEAD_PAD), BF16),
                   jax.ShapeDtypeStruct((b, HEADS, s, HEAD_PAD), BF16),
                   jax.ShapeDtypeStruct((b, HEADS, HEAD_PAD, s), BF16)],
        compiler_params=_cparams("parallel", "parallel"),
        name="inproj",
    )(x, wts["norm1_g"], mod["sc1"], mod["sh1"], wts["w_in"], wts["gq"], wts["wq"], wts["gkv"], wts["wk"],
      wts["wvt"], wts["gqq"], wts["gqq_rot"], wts["gqk"], wts["gqk_rot"], wts["vone"], *rope_tabs)


def _seqmix_kernel(prev_ref, cur_ref, next_ref, wpool_ref, pscale_ref, cw_ref, cb_ref,
                   pool_ref, x0_ref, z_ref, *, seq_len):
    i = pl.program_id(1)
    n = pl.num_programs(1)
    t_rows = cur_ref.shape[1]
    halo = prev_ref.shape[1]
    r_rows = t_rows + 2 * halo
    prev = jnp.where(i > 0, prev_ref[0], 0.0)
    nxt = jnp.where(i < n - 1, next_ref[0], 0.0)
    ext = jnp.concatenate([prev, cur_ref[0], nxt], axis=0)

    def sh(a, dlt):
        return pltpu.roll(a, (-dlt) % r_rows, axis=0)

    u = ext[:, :POOL_WIDTH]
    a2 = u + sh(u, -1)
    a4 = sh(a2, -1) + sh(a2, 1)
    a8 = sh(a4, -2) + sh(a4, 2)
    a16 = sh(a8, -4) + sh(a8, 4)
    grp = lax.broadcasted_iota(jnp.int32, (1, POOL_WIDTH), 1) // POOL_GROUP_DIM
    win = jnp.where(grp == 0, a2, jnp.where(grp == 1, a4, jnp.where(grp == 2, a8, a16)))
    half = jnp.where(grp == 0, 1, jnp.where(grp == 1, 2, jnp.where(grp == 2, 4, 8)))
    pos = i * t_rows - halo + lax.broadcasted_iota(jnp.int32, (r_rows, 1), 0)
    cnt = jnp.minimum(pos + half, seq_len) - jnp.maximum(pos - half, 0)
    cnt = jnp.maximum(cnt, 1).astype(F32)
    dev = (win / cnt - u)[halo:halo + t_rows]
    pool = jnp.dot(dev.astype(BF16), wpool_ref[...], preferred_element_type=F32) * pscale_ref[...]
    pool_ref[0] = pool.astype(BF16)

    e = ext[:, OFF_HY:OFF_Q]
    cw = cw_ref[...]
    uc = sh(e, -1) * cw[0:1] + e * cw[1:2] + sh(e, 1) * cw[2:3] + cb_ref[...]
    uc = uc[halo:halo + t_rows]
    x0_ref[0] = uc[:, :HY_WIDTH]
    z_ref[0] = uc[:, HY_WIDTH:2 * HY_WIDTH] * uc[:, 2 * HY_WIDTH:]


def _seqmix(pph, wts):
    b, s, w = pph.shape
    t = min(ROW_TILE, s)
    halo = 8
    per = t // halo
    nh = s // halo
    full = lambda shape: pl.BlockSpec(shape, lambda bi, i: (0,) * len(shape))
    out = lambda: pl.BlockSpec((1, t, HY_WIDTH), lambda bi, i: (bi, i, 0))
    return pl.pallas_call(
        functools.partial(_seqmix_kernel, seq_len=s),
        grid=(b, s // t),
        in_specs=[pl.BlockSpec((1, halo, w), lambda bi, i: (bi, jnp.maximum(i * per - 1, 0), 0)),
                  pl.BlockSpec((1, t, w), lambda bi, i: (bi, i, 0)),
                  pl.BlockSpec((1, halo, w), lambda bi, i: (bi, jnp.minimum((i + 1) * per, nh - 1), 0)),
                  full((POOL_WIDTH, POOL_WIDTH)), full((1, POOL_WIDTH)), full((3, HY_IN)), full((1, HY_IN))],
        out_specs=[out(), out(), out()],
        out_shape=[jax.ShapeDtypeStruct((b, s, POOL_WIDTH), BF16),
                   jax.ShapeDtypeStruct((b, s, HY_WIDTH), F32),
                   jax.ShapeDtypeStruct((b, s, HY_WIDTH), F32)],
        compiler_params=_cparams("parallel", "parallel"),
        name="seqmix",
    )(pph, pph, pph, wts["wpool"], wts["pool_scale"], wts["conv_w"], wts["conv_b"])


def _dot3(m_ref, x):
    x_hi = x.astype(BF16)
    x_lo = (x - x_hi.astype(F32)).astype(BF16)
    m_hi, m_lo = m_ref[0], m_ref[1]
    return (jnp.dot(m_hi, x_hi, preferred_element_type=F32) + jnp.dot(m_hi, x_lo, preferred_element_type=F32)
            + jnp.dot(m_lo, x_hi, preferred_element_type=F32))


def _split_const(m):
    hi = m.astype(BF16)
    return jnp.stack([hi, (m - hi.astype(F32)).astype(BF16)])


def _leftmm_kernel(*refs, cmul, gate):
    m_ref, x_ref = refs[0], refs[1]
    nxt = 2
    x = x_ref[0]
    if cmul:
        ks = refs[nxt][...]
        nxt += 1
        kh = x.shape[0] // 2
        xr, xi, kr, ki = x[:kh], x[kh:], ks[:kh], ks[kh:]
        x = jnp.concatenate([xr * kr - xi * ki, xr * ki + xi * kr], axis=0)
    y = jnp.dot(m_ref[...], x, precision=HIGHEST, preferred_element_type=F32)
    if gate:
        z_ref, bias_ref, x0_ref = refs[nxt:nxt + 3]
        nxt += 3
        y = (y + z_ref[0] * bias_ref[...]) * x0_ref[0]
    o_ref = refs[nxt]
    o_ref[0] = y.astype(o_ref.dtype)


def _leftmm(m, x, spectrum=None, gate=None, out_dtype=F32):
    g, k, cols = x.shape
    mo = m.shape[0]
    tc = min(cols, 2048)
    args = [m, x]
    in_specs = [pl.BlockSpec((mo, k), lambda gi, j: (0, 0)), pl.BlockSpec((1, k, tc), lambda gi, j: (gi, 0, j))]
    if spectrum is not None:
        args.append(spectrum)
        in_specs.append(pl.BlockSpec((k, tc), lambda gi, j: (0, j)))
    if gate is not None:
        z, bias, x0 = gate
        args += [z, bias, x0]
        in_specs += [pl.BlockSpec((1, mo, tc), lambda gi, j: (gi, 0, j)),
                     pl.BlockSpec((1, tc), lambda gi, j: (0, j)),
                     pl.BlockSpec((1, mo, tc), lambda gi, j: (gi, 0, j))]
    return pl.pallas_call(
        functools.partial(_leftmm_kernel, cmul=spectrum is not None, gate=gate is not None),
        grid=(g, cols // tc),
        in_specs=in_specs,
        out_specs=pl.BlockSpec((1, mo, tc), lambda gi, j: (gi, 0, j)),
        out_shape=jax.ShapeDtypeStruct((g, mo, cols), out_dtype),
        compiler_params=_cparams("parallel", "parallel"),
        name="dft_leftmm",
    )(*args)


def _blockfft_kernel(*refs, n2, blocks, conv):
    if conv:
        a_ref, tw_ref, ks_ref, m2_ref, m2c_ref, o_ref = refs
    else:
        a_ref, tw_ref, m2_ref, o_ref = refs

    def body(j, carry):
        rows = pl.ds(j * n2, n2)
        ar, ai = a_ref[0, 0, rows, :], a_ref[0, 1, rows, :]
        tcs, tsn = tw_ref[0, rows, :], tw_ref[1, rows, :]
        tcs = jnp.concatenate([tcs] * (ar.shape[1] // tcs.shape[1]), axis=1)
        tsn = jnp.concatenate([tsn] * (ar.shape[1] // tsn.shape[1]), axis=1)
        bstk = jnp.concatenate([ar * tcs + ai * tsn, ai * tcs - ar * tsn], axis=0)
        xs = _dot3(m2_ref, bstk)
        if not conv:
            o_ref[0, 0, rows, :] = xs[:n2]
            o_ref[0, 1, rows, :] = xs[n2:]
            return carry
        xr, xi = xs[:n2], xs[n2:]
        kr, ki = ks_ref[0, rows, :], ks_ref[1, rows, :]
        ystk = jnp.concatenate([xr * kr - xi * ki, xr * ki + xi * kr], axis=0)
        cs = _dot3(m2c_ref, ystk)
        cr, ci = cs[:n2], cs[n2:]
        o_ref[0, 0, rows, :] = cr * tcs - ci * tsn
        o_ref[0, 1, rows, :] = cr * tsn + ci * tcs
        return carry

    for j in range(blocks):
        body(j, 0)


def _blockfft(a, tw, m2, spectrum=None, m2c=None):
    g, _, n, c = a.shape
    n2 = m2.shape[1] // 2
    blocks = min(8, n // n2)
    rows = blocks * n2
    conv = spectrum is not None
    args = [a, tw]
    in_specs = [pl.BlockSpec((1, 2, rows, c), lambda gi, j: (gi, 0, j, 0)),
                pl.BlockSpec((2, rows, LANE), lambda gi, j: (0, j, 0))]
    if conv:
        args.append(spectrum)
        in_specs.append(pl.BlockSpec((2, rows, c), lambda gi, j: (0, j, 0)))
    args.append(m2)
    in_specs.append(pl.BlockSpec(m2.shape, lambda gi, j: (0, 0, 0)))
    if conv:
        args.append(m2c)
        in_specs.append(pl.BlockSpec(m2c.shape, lambda gi, j: (0, 0, 0)))
    return pl.pallas_call(
        functools.partial(_blockfft_kernel, n2=n2, blocks=blocks, conv=conv),
        grid=(g, n // rows),
        in_specs=in_specs,
        out_specs=pl.BlockSpec((1, 2, rows, c), lambda gi, j: (gi, 0, j, 0)),
        out_shape=jax.ShapeDtypeStruct((g, 2, n, c), F32),
        compiler_params=_cparams("parallel", "parallel"),
        name="dft_block",
    )(*args)


def _filter_kernel(za_ref, zb_ref, w1a_ref, w1b_ref, b1_ref, w2_ref, b2_ref, w3a_ref, w3b_ref, fr_ref, dl_ref,
                   o_ref):
    i = pl.program_id(0)
    za, zb = za_ref[...], zb_ref[...]
    fr = fr_ref[...]
    dot = functools.partial(jnp.dot, precision=HIGHEST, preferred_element_type=F32)
    h = jnp.sin(fr * (dot(za, w1a_ref[...]) + dot(zb, w1b_ref[...]) + b1_ref[...]))
    h = jnp.sin(fr * (dot(h, w2_ref[...]) + b2_ref[...]))
    o_ref[0] = dot(h, w3a_ref[...]) * jnp.exp(-za[:, 0:1] * dl_ref[...])
    first = (i == 0) & (lax.broadcasted_iota(jnp.int32, (zb.shape[0], 1), 0) == 0)
    o_ref[1] = jnp.where(first, 0.0, dot(h, w3b_ref[...]) * jnp.exp(-zb[:, 0:1] * dl_ref[...]))


def _hyena_filter(seq_len, wts):
    n = 2 * seq_len
    j = jnp.arange(n)
    lag = jnp.clip(jnp.where(j < seq_len, j, n - j), 0, seq_len - 1).astype(F32)[:, None]
    t = lag / (seq_len - 1)
    wpos = (2.0 * math.pi / seq_len) * lag
    f = jnp.linspace(1e-4, HY_BANDS - 1, HY_BANDS, dtype=F32)[None, :]
    feat = jnp.concatenate([t, jnp.cos(f * wpos), -jnp.sin(f * wpos)], axis=-1)
    feat2 = jnp.pad(feat, ((0, 0), (0, LANE - HY_EMB)))
    rows = min(seq_len, 1024)
    half = seq_len // rows
    full = lambda shape: pl.BlockSpec(shape, lambda i: (0,) * len(shape))
    taps = pl.pallas_call(
        _filter_kernel,
        grid=(half,),
        in_specs=[pl.BlockSpec((rows, LANE), lambda i: (i, 0)), pl.BlockSpec((rows, LANE), lambda i: (half + i, 0)),
                  full((LANE, LANE)), full((LANE, LANE)), full((1, LANE)), full((LANE, LANE)), full((1, LANE)),
                  full((LANE, HY_WIDTH)), full((LANE, HY_WIDTH)), full((1, LANE)), full((1, HY_WIDTH))],
        out_specs=pl.BlockSpec((2, rows, HY_WIDTH), lambda i: (0, i, 0)),
        out_shape=jax.ShapeDtypeStruct((2, seq_len, HY_WIDTH), F32),
        compiler_params=_cparams("parallel"),
        name="hyena_filter",
    )(feat2, feat2, wts["f_w1a"], wts["f_w1b"], wts["f_b1"], wts["f_w2"], wts["f_b2"], wts["f_w3a"], wts["f_w3b"],
      wts["f_freq"], wts["f_delta"])
    return taps.reshape(n, HY_WIDTH)


def _cs(p, rows, cols):
    ang = 2.0 * np.pi * np.outer(np.arange(rows), np.arange(cols)) / p
    return np.cos(ang), np.sin(ang)


def _mat(blocks):
    return jnp.asarray(np.block(blocks), dtype=F32)


def _dftcols_kernel(*refs, gate):
    m_ref, x_ref = refs[0], refs[1]
    x = x_ref[0]
    k, nt, _ = x.shape
    xt = pltpu.einshape("knc->(nk)c", x)
    ys = [_dot3(m_ref, xt[n * k:(n + 1) * k]) for n in range(nt)]
    y = pltpu.einshape("(nk)c->knc", jnp.concatenate(ys, axis=0), n=nt)
    if gate:
        z_ref, bias_ref, x0_ref, o_ref = refs[2:]
        y = (y + z_ref[0] * bias_ref[...]) * x0_ref[0]
    else:
        o_ref = refs[2]
    o_ref[0] = y.astype(o_ref.dtype)


def _dftcols(m, x, gate=None, out_dtype=F32):
    g, k, n2, c = x.shape
    mo = m.shape[1]
    nt = 16 if n2 % 16 == 0 else 8
    blk = lambda rows: pl.BlockSpec((1, rows, nt, c), lambda gi, j: (gi, 0, j, 0))
    args = [m, x]
    in_specs = [pl.BlockSpec((2, mo, k), lambda gi, j: (0, 0, 0)), blk(k)]
    if gate is not None:
        z, bias, x0 = gate
        args += [z, bias, x0]
        in_specs += [blk(mo), pl.BlockSpec((1, c), lambda gi, j: (0, 0)), blk(mo)]
    return pl.pallas_call(
        functools.partial(_dftcols_kernel, gate=gate is not None),
        grid=(g, n2 // nt),
        in_specs=in_specs,
        out_specs=blk(mo),
        out_shape=jax.ShapeDtypeStruct((g, mo, n2, c), out_dtype),
        compiler_params=_cparams("parallel", "parallel"),
        name="dft_cols",
    )(*args)


def _hyena_long(z, x0, bias, filt):
    b, s, c = z.shape
    n = 2 * s
    n1 = 1 << ((n.bit_length() - 1 + 1) // 2)
    n2 = n // n1
    g = b // 2
    c1, s1 = _cs(n1, n1, n1)
    h1 = n1 // 2
    m_fwd = _split_const(_mat([[c1[:, :h1], s1[:, :h1]], [-s1[:, :h1], c1[:, :h1]]]))
    m_filt = _split_const(_mat([[c1], [-s1]]))
    m_inv = _split_const(_mat([[c1[:h1], -s1[:h1]], [s1[:h1], c1[:h1]]]) * (1.0 / n))
    c2, s2 = _cs(n2, n2, n2)
    m2 = _split_const(_mat([[c2, s2], [-s2, c2]]))
    m2c = _split_const(_mat([[c2, -s2], [s2, c2]]))
    k1 = jnp.arange(n1, dtype=jnp.int32)[:, None]
    nn2 = jnp.arange(n2, dtype=jnp.int32)[None, :]
    ang = (2.0 * math.pi / n) * (k1 * nn2).astype(F32).reshape(n)
    tw = jnp.broadcast_to(jnp.stack([jnp.cos(ang), jnp.sin(ang)])[:, :, None], (2, n, LANE))

    spec = _dftcols(m_filt, filt.reshape(1, n1, n2, c)).reshape(1, 2, n, c)
    spec = _blockfft(spec, tw, m2)[0]
    zv = z.reshape(g, n1, n2, c)
    a = _dftcols(m_fwd, zv).reshape(g, 2, n, c)
    dd = _blockfft(a, tw, m2, spectrum=spec, m2c=m2c).reshape(g, 2 * n1, n2, c)
    y = _dftcols(m_inv, dd, gate=(zv, bias, x0.reshape(g, n1, n2, c)), out_dtype=BF16)
    return y.reshape(b, s, c)


def _hyena_short(z, x0, bias, filt):
    b, s, c = z.shape
    n = 2 * s
    g = b // 2
    cf, sf = _cs(n, n, n)
    m_fwd = _mat([[cf[:, :s], sf[:, :s]], [-sf[:, :s], cf[:, :s]]])
    m_filt = _mat([[cf], [-sf]])
    m_inv = _mat([[cf[:s], -sf[:s]], [sf[:s], cf[:s]]]) * (1.0 / n)
    spec = _leftmm(m_filt, filt.reshape(1, n, c))[0]
    zv = z.reshape(g, n, c)
    zf = _leftmm(m_fwd, zv)
    y = _leftmm(m_inv, zf, spectrum=spec, gate=(zv, bias, x0.reshape(g, n, c)), out_dtype=BF16)
    return y.reshape(b, s, c)


def _attn_kernel(*refs, tk, n_chunks, heads, unroll):
    if n_chunks:
        q_ref, kc_ref, vc_ref, kl_ref, vl_ref, o_ref, s_ref, sc_ref = refs
    else:
        q_ref, kc_ref, vc_ref, o_ref = refs
    dn = (((1,), (1,)), ((), ()))
    qs = [q_ref[0, hh] for hh in range(heads)]

    def chunk_rows(j):
        return pl.ds(pl.multiple_of(j * tk, tk), tk)

    def scores(j, slot):
        for hh in range(heads):
            s_ref[slot, hh] = lax.dot_general(kl_ref[0, hh, chunk_rows(j), :], qs[hh], dn,
                                              preferred_element_type=F32)

    def absorb(j, slot, carry):
        out = []
        for hh in range(heads):
            m_old, acc_old = carry[hh]
            sj = s_ref[slot, hh]
            m_new = jnp.maximum(m_old, jnp.max(sj, axis=0, keepdims=True))
            pj = jnp.exp2(sj - m_new)
            acc_new = jnp.exp2(m_old - m_new) * acc_old + jnp.dot(vl_ref[0, hh, :, chunk_rows(j)], pj.astype(BF16),
                                                                  preferred_element_type=F32)
            out.append((m_new, acc_new))
        return tuple(out)

    def ctx_scores(hh):
        return lax.dot_general(kc_ref[0, hh], qs[hh], dn, preferred_element_type=F32)

    def merge_ctx(s, hh, m_old, acc_old):
        m_new = jnp.maximum(m_old, jnp.max(s, axis=0, keepdims=True))
        p = jnp.exp2(s - m_new)
        return m_new, jnp.exp2(m_old - m_new) * acc_old + jnp.dot(vc_ref[0, hh], p.astype(BF16),
                                                                  preferred_element_type=F32)

    def body(jj, carry):
        j = unroll * jj
        for u in range(unroll):
            nxt = j + u + 1
            if u == unroll - 1:
                nxt = jnp.minimum(nxt, n_chunks - 1)
            scores(nxt, (u + 1) % 2)
            carry = absorb(j + u, u % 2, carry)
        return carry

    tq = qs[0].shape[0]
    empty = (jnp.full((1, tq), -jnp.inf, F32), jnp.zeros((vc_ref.shape[2], tq), F32))
    if n_chunks:
        scores(0, 0)
        for hh in range(heads):
            sc_ref[hh] = ctx_scores(hh)
        state = lax.fori_loop(0, n_chunks // unroll, body, (empty,) * heads)
        state = [merge_ctx(sc_ref[hh], hh, *state[hh]) for hh in range(heads)]
    else:
        state = [merge_ctx(ctx_scores(hh), hh, *empty) for hh in range(heads)]
    outs = [acc[:V_DIM] / acc[V_DIM:V_DIM + 1] for _, acc in state]
    o_ref[0] = jnp.concatenate(outs, axis=0).T.astype(o_ref.dtype)


def _attention(q, kc, vct, kl=None, vlt=None):
    b, h, s, dp = q.shape
    nc = kc.shape[2]
    tq = min(ROW_TILE, s)
    heads = LANE // V_DIM
    tk = 256
    n_chunks = 0 if kl is None else kl.shape[2] // tk
    unroll = 16 if n_chunks % 16 == 0 else 2
    args = [q, kc, vct]
    in_specs = [pl.BlockSpec((1, heads, tq, dp), lambda bi, hi, i: (bi, hi, i, 0)),
                pl.BlockSpec((1, heads, nc, dp), lambda bi, hi, i: (bi, hi, 0, 0)),
                pl.BlockSpec((1, heads, dp, nc), lambda bi, hi, i: (bi, hi, 0, 0))]
    scratch = []
    if n_chunks:
        assert unroll % 2 == 0 and n_chunks % unroll == 0
        sl = kl.shape[2]
        args += [kl, vlt]
        in_specs += [pl.BlockSpec((1, heads, sl, dp), lambda bi, hi, i: (bi, hi, 0, 0)),
                     pl.BlockSpec((1, heads, dp, sl), lambda bi, hi, i: (bi, hi, 0, 0))]
        scratch = [pltpu.VMEM((2, heads, tk, tq), F32), pltpu.VMEM((heads, nc, tq), F32)]
    return pl.pallas_call(
        functools.partial(_attn_kernel, tk=tk, n_chunks=n_chunks, heads=heads, unroll=unroll),
        grid=(b, h // heads, s // tq),
        in_specs=in_specs,
        out_specs=pl.BlockSpec((1, tq, heads * V_DIM), lambda bi, hi, i: (bi, i, hi)),
        out_shape=jax.ShapeDtypeStruct((b, s, h * V_DIM), BF16),
        scratch_shapes=scratch,
        compiler_params=_cparams("parallel", "parallel", "parallel"),
        name="attention",
    )(*args)


def _outproj_kernel(x_ref, pool_ref, hy_ref, at_ref, wout_ref, g1_ref, gn_ref, sc_ref, sh_ref, rw_ref, rb_ref,
                    *rest):
    x1_ref, h2_ref, lg_ref = rest[-3:]
    mix = jnp.concatenate([pool_ref[0], hy_ref[0], at_ref[0]], axis=-1)
    y = jnp.dot(mix, wout_ref[...], preferred_element_type=F32)
    x1 = x_ref[0] + g1_ref[0] * y
    x1_ref[0] = x1
    h2 = _rms(x1, gn_ref[...], D_MODEL) * (1.0 + sc_ref[0]) + sh_ref[0]
    h2_ref[...] = _pack_rows(h2)
    h_hi = h2.astype(BF16)
    h_lo = (h2 - h_hi.astype(F32)).astype(BF16)
    nt = functools.partial(lax.dot_general, dimension_numbers=(((1,), (1,)), ((), ())), preferred_element_type=F32)
    lg_ref[...] = nt(rw_ref[0], h_hi) + nt(rw_ref[0], h_lo) + nt(rw_ref[1], h_hi) + rb_ref[...]


def _outproj(x, pool, hy, attn, seg_of_batch, wts, mod, n_tok, tok_offset, prior=None):
    b, s, d = x.shape
    t = min(ROW_TILE, s)
    per = s // t
    off = tok_offset // t
    full = lambda shape: pl.BlockSpec(shape, lambda bi, i: (0,) * len(shape))
    seg = lambda: pl.BlockSpec((1, 1, d), lambda bi, i: (seg_of_batch(bi), 0, 0))
    rowblk = lambda w: pl.BlockSpec((1, t, w), lambda bi, i: (bi, i, 0))
    args = [x, pool, hy, attn, wts["w_out"], mod["g1"], wts["norm2_g"], mod["sc2"], mod["sh2"],
            wts["router_w"], wts["router_b"]]
    in_specs = [rowblk(d), rowblk(POOL_WIDTH), rowblk(HY_WIDTH), rowblk(MLA_WIDTH),
                full((d, d)), seg(), full((1, d)), seg(), seg(), full((2, N_EXPERTS, d)), full((N_EXPERTS, 1))]
    aliases = {}
    if prior is not None:
        aliases = {len(args): 1, len(args) + 1: 2}
        args += list(prior)
        in_specs += [pl.BlockSpec(memory_space=pl.ANY)] * 2
    return pl.pallas_call(
        _outproj_kernel,
        grid=(b, per),
        in_specs=in_specs,
        out_specs=[rowblk(d), pl.BlockSpec((t, d // 2), lambda bi, i: (off + bi * per + i, 0)),
                   pl.BlockSpec((N_EXPERTS, t), lambda bi, i: (0, off + bi * per + i))],
        out_shape=[jax.ShapeDtypeStruct((b, s, d), F32), jax.ShapeDtypeStruct((n_tok, d // 2), jnp.int32),
                   jax.ShapeDtypeStruct((N_EXPERTS, n_tok), F32)],
        input_output_aliases=aliases,
        compiler_params=_cparams("parallel", "parallel"),
        name="outproj",
    )(*args)


def _moe_kernel(be_ref, br_ref, x_ref, w1_ref, b1_ref, w2_ref, b2_ref, o_ref, w1b_ref, w2b_ref):
    i = pl.program_id(0)
    e = be_ref[i]
    e_prev = be_ref[jnp.maximum(i - 1, 0)]
    n_rows = br_ref[i]

    @pl.when((i == 0) | (e != e_prev))
    def _():
        w1b_ref[...] = w1_ref[0, 0].astype(BF16)
        w2b_ref[...] = w2_ref[0, 0].astype(BF16)

    @pl.when(n_rows > 0)
    def _():
        live = lax.broadcasted_iota(jnp.int32, (x_ref.shape[0], 1), 0) < n_rows
        x = _unpack_rows(jnp.where(live, x_ref[...], 0))
        a = jnp.dot(x, w1b_ref[...], preferred_element_type=F32) + b1_ref[0, 0]
        glu = jnp.minimum(a[:, :D_EXPERT], SWIGLU_LIMIT)
        lin = jnp.clip(a[:, D_EXPERT:], -SWIGLU_LIMIT, SWIGLU_LIMIT)
        act = glu * (1.0 / (1.0 + jnp.exp(-SWIGLU_ALPHA * glu))) * (lin + 1.0)
        y = jnp.dot(act.astype(BF16), w2b_ref[...], preferred_element_type=F32) + b2_ref[0, 0]
        o_ref[...] = _pack_rows(y)

    @pl.when(n_rows == 0)
    def _():
        o_ref[...] = jnp.zeros_like(o_ref)


def _experts(xs, block_e, block_rows, w1, b1, w2, b2, layer):
    n_buf, dh = xs.shape
    d = 2 * dh
    tm = EXPERT_TILE
    n_blocks = n_buf // tm
    de2 = w1.shape[3]
    grid_spec = pltpu.PrefetchScalarGridSpec(
        num_scalar_prefetch=2,
        grid=(n_blocks,),
        in_specs=[pl.BlockSpec((tm, dh), lambda i, be, nu: (i, 0)),
                  pl.BlockSpec((1, 1, d, de2), lambda i, be, nu: (layer, be[i], 0, 0)),
                  pl.BlockSpec((1, 1, 1, de2), lambda i, be, nu: (layer, be[i], 0, 0)),
                  pl.BlockSpec((1, 1, D_EXPERT, d), lambda i, be, nu: (layer, be[i], 0, 0)),
                  pl.BlockSpec((1, 1, 1, d), lambda i, be, nu: (layer, be[i], 0, 0))],
        out_specs=pl.BlockSpec((tm, dh), lambda i, be, nu: (i, 0)),
        scratch_shapes=[pltpu.VMEM((d, de2), BF16), pltpu.VMEM((D_EXPERT, d), BF16)],
    )
    return pl.pallas_call(
        _moe_kernel,
        grid_spec=grid_spec,
        out_shape=jax.ShapeDtypeStruct((n_buf, dh), jnp.int32),
        compiler_params=_cparams("arbitrary"),
        name="experts",
    )(block_e, block_rows, xs, w1, b1, w2, b2)


def _router_kernel(lg_ref, idx_ref, gate_ref, rank_ref, cnt_ref, base_ref):
    i = pl.program_id(0)
    n_e, t = lg_ref.shape

    @pl.when(i == 0)
    def _():
        base_ref[...] = jnp.zeros_like(base_ref)

    work = lg_ref[...]
    eid = lax.broadcasted_iota(jnp.int32, (n_e, t), 0)
    vals, idxs, sels = [], [], []
    for _ in range(TOP_K):
        m = jnp.max(work, axis=0, keepdims=True)
        idx = jnp.min(jnp.where(work == m, eid, n_e), axis=0, keepdims=True)
        sel = eid == idx
        work = jnp.where(sel, -jnp.inf, work)
        vals.append(m)
        idxs.append(idx)
        sels.append(sel)
    cnt = jnp.where(sels[0], 1.0, 0.0)
    for sel in sels[1:]:
        cnt = cnt + jnp.where(sel, 1.0, 0.0)
    earlier = (lax.broadcasted_iota(jnp.int32, (t, t), 0) < lax.broadcasted_iota(jnp.int32, (t, t), 1))
    within = jnp.dot(cnt.astype(BF16), jnp.where(earlier, 1.0, 0.0).astype(BF16), preferred_element_type=F32)
    pref = base_ref[:, 0:1] + within
    ranks = [jnp.sum(jnp.where(sel, pref, 0.0), axis=0, keepdims=True) for sel in sels]
    new_base = base_ref[...] + jnp.sum(cnt, axis=1, keepdims=True)
    base_ref[...] = new_base
    cnt_ref[...] = new_base
    ex = [jnp.exp(v - vals[0]) for v in vals]
    den = ex[0]
    for e in ex[1:]:
        den = den + e
    idx_ref[...] = jnp.concatenate(idxs, axis=0)
    gate_ref[...] = jnp.concatenate([e / den for e in ex], axis=0)
    rank_ref[...] = jnp.concatenate(ranks, axis=0).astype(jnp.int32)


def _route(logits_t):
    n_e, n_tok = logits_t.shape
    tm = EXPERT_TILE
    t = ROW_TILE
    tokblk = lambda: pl.BlockSpec((TOP_K, t), lambda i: (0, i))
    idx_t, gates_t, rank_t, counts_f = pl.pallas_call(
        _router_kernel,
        grid=(n_tok // t,),
        in_specs=[pl.BlockSpec((n_e, t), lambda i: (0, i))],
        out_specs=[tokblk(), tokblk(), tokblk(), pl.BlockSpec((n_e, LANE), lambda i: (0, 0))],
        out_shape=[jax.ShapeDtypeStruct((TOP_K, n_tok), jnp.int32), jax.ShapeDtypeStruct((TOP_K, n_tok), F32),
                   jax.ShapeDtypeStruct((TOP_K, n_tok), jnp.int32), jax.ShapeDtypeStruct((n_e, LANE), F32)],
        scratch_shapes=[pltpu.VMEM((n_e, LANE), F32)],
        compiler_params=_cparams("arbitrary"),
        name="router",
    )(logits_t)
    n_slots = n_tok * TOP_K
    counts = counts_f[:, 0].astype(jnp.int32)
    padded = (counts + tm - 1) // tm * tm
    padded_end = jnp.cumsum(padded)
    padded_start = padded_end - padded
    experts = jnp.arange(n_e, dtype=jnp.int32)
    dest_t = rank_t + jnp.sum(jnp.where(idx_t[None] == experts[:, None, None], padded_start[:, None, None], 0), axis=0)
    n_blocks = n_slots // tm + n_e
    block_start = jnp.arange(n_blocks, dtype=jnp.int32) * tm
    block_e = jnp.minimum(jnp.sum(padded_end[None, :] <= block_start[:, None], axis=1), n_e - 1).astype(jnp.int32)
    sel = (block_e[:, None] == experts[None, :])
    pick = lambda v: jnp.sum(jnp.where(sel, v[None, :], 0), axis=1)
    block_rows = jnp.clip(pick(padded_start) + pick(counts) - block_start, 0, tm).astype(jnp.int32)
    return gates_t.T, dest_t, block_e, block_rows


SC_WINDOW = 32
SC_SLOTS = 4


def _sc_gather(table, idx):
    _, d = table.shape
    b = idx.shape[0]
    info = plsc.get_sparse_core_info()
    nc, ns = info.num_cores, info.num_subcores
    nw = nc * ns
    w, nslot = SC_WINDOW, SC_SLOTS
    per_w = b // nw
    n_win = per_w // w
    assert per_w * nw == b and n_win * w == per_w and n_win % nslot == 0
    mesh = plsc.VectorSubcoreMesh(core_axis_name="c", subcore_axis_name="s")

    def body(table_hbm, idx_hbm, out_hbm, idx_v, rows_v, gsem, osem):
        wid = lax.axis_index("s") * nc + lax.axis_index("c")
        base = wid * per_w
        pltpu.sync_copy(idx_hbm.at[wid], idx_v)

        def gather(g, slot):
            return pltpu.make_async_copy(table_hbm.at[idx_v.at[g]], rows_v.at[slot], gsem.at[slot])

        def put(g, slot):
            return pltpu.make_async_copy(rows_v.at[slot], out_hbm.at[pl.ds(base + g * w, w)], osem.at[slot])

        for g in range(nslot - 1):
            gather(g, g).start()

        @pl.loop(0, n_win, step=nslot)
        def _(g0):
            for slot in range(nslot):
                g = g0 + slot
                gather(g, slot).wait()
                put(g, slot).start()
                free = (slot - 1) % nslot

                @pl.when(g + nslot - 1 < n_win)
                def _():
                    @pl.when(g >= 1)
                    def _():
                        put(g - 1, free).wait()

                    gather(g + nslot - 1, free).start()

        for g in range(n_win - nslot, n_win):
            put(g, g % nslot).wait()

    return pl.kernel(
        body,
        out_type=jax.ShapeDtypeStruct((b, d), table.dtype),
        mesh=mesh,
        scratch_types=[pltpu.VMEM((n_win, w), jnp.int32), pltpu.VMEM((nslot, w, d), table.dtype),
                       pltpu.SemaphoreType.DMA((nslot,)), pltpu.SemaphoreType.DMA((nslot,))],
        name="sc_gather",
    )(table, idx.reshape(nw, n_win, w))


SC_SCATTER_WINDOW = 16


def _sc_scatter(rows, dest_t, n_out):
    n_tok, d = rows.shape
    k = dest_t.shape[0]
    info = plsc.get_sparse_core_info()
    nc, ns = info.num_cores, info.num_subcores
    nw = nc * ns
    w = SC_SCATTER_WINDOW
    per_w = n_tok // nw
    n_win = per_w // w
    assert per_w * nw == n_tok and n_win * w == per_w and n_win % 2 == 0
    mesh = plsc.VectorSubcoreMesh(core_axis_name="c", subcore_axis_name="s")
    idx = dest_t.reshape(k, nw, n_win, w).transpose(1, 0, 2, 3)

    def body(rows_hbm, idx_hbm, out_hbm, idx_v, rows_v, lsem, ssem):
        wid = lax.axis_index("s") * nc + lax.axis_index("c")
        base = wid * per_w
        pltpu.sync_copy(idx_hbm.at[wid], idx_v)

        def load(g, slot):
            return pltpu.make_async_copy(rows_hbm.at[pl.ds(base + g * w, w)], rows_v.at[slot], lsem.at[slot])

        def scatter(g, slot, j):
            return pltpu.make_async_copy(rows_v.at[slot], out_hbm.at[idx_v.at[j, g]], ssem.at[slot])

        load(0, 0).start()

        @pl.loop(0, n_win, step=2)
        def _(g0):
            for slot in range(2):
                g = g0 + slot
                load(g, slot).wait()
                for j in range(k):
                    scatter(g, slot, j).start()

                @pl.when(g + 1 < n_win)
                def _():
                    @pl.when(g >= 1)
                    def _():
                        for j in range(k):
                            scatter(g - 1, 1 - slot, j).wait()

                    load(g + 1, 1 - slot).start()

        for g in (n_win - 2, n_win - 1):
            for j in range(k):
                scatter(g, g % 2, j).wait()

    return pl.kernel(
        body,
        out_type=jax.ShapeDtypeStruct((n_out, d), rows.dtype),
        mesh=mesh,
        scratch_types=[pltpu.VMEM((k, n_win, w), jnp.int32), pltpu.VMEM((2, w, d), rows.dtype),
                       pltpu.SemaphoreType.DMA((2,)), pltpu.SemaphoreType.DMA((2,))],
        name="sc_scatter",
    )(rows, idx)


def _combine_kernel(x_ref, y_ref, gt_ref, g2_ref, o_ref):
    gt = gt_ref[...]
    acc = gt[:, 0:1] * _unpack_rows(y_ref[0]).astype(F32)
    for j in range(1, TOP_K):
        acc = acc + gt[:, j:j + 1] * _unpack_rows(y_ref[j]).astype(F32)
    o_ref[0] = x_ref[0] + g2_ref[0] * acc


def _combine(x, ysel, gates, tok_offset, seg_of_batch, mod):
    b, s, d = x.shape
    t = min(ROW_TILE, s)
    per = s // t
    off = tok_offset // t
    return pl.pallas_call(
        _combine_kernel,
        grid=(b, per),
        in_specs=[pl.BlockSpec((1, t, d), lambda bi, i: (bi, i, 0)),
                  pl.BlockSpec((TOP_K, t, d // 2), lambda bi, i: (0, off + bi * per + i, 0)),
                  pl.BlockSpec((t, TOP_K), lambda bi, i: (off + bi * per + i, 0)),
                  pl.BlockSpec((1, 1, d), lambda bi, i: (seg_of_batch(bi), 0, 0))],
        out_specs=pl.BlockSpec((1, t, d), lambda bi, i: (bi, i, 0)),
        out_shape=jax.ShapeDtypeStruct((b, s, d), F32),
        compiler_params=_cparams("parallel", "parallel"),
        name="combine",
    )(x, ysel, gates, mod["g2"])


def _rope_tables(seq_len):
    n_rows = seq_len // GRID_W
    row = jnp.repeat(jnp.arange(n_rows, dtype=F32), GRID_W)
    col = jnp.tile(jnp.arange(GRID_W, dtype=F32), n_rows)
    pairs = ROPE // 4
    inv = ROPE_BASE ** (-jnp.arange(pairs, dtype=F32) / pairs)
    ang = jnp.concatenate([row[:, None] * inv, col[:, None] * inv], axis=-1)
    cos, sin = jnp.cos(ang), jnp.sin(ang)
    one = jnp.ones((seq_len, NOPE), F32)
    zn = jnp.zeros((seq_len, NOPE), F32)
    zp = jnp.zeros((seq_len, HEAD_PAD - QK_DIM), F32)
    return jnp.concatenate([one, cos, cos, zp], axis=-1), jnp.concatenate([zn, sin, sin, zp], axis=-1)


def _identity_rope(seq_len):
    keep = jnp.concatenate([jnp.ones((seq_len, QK_DIM), F32), jnp.zeros((seq_len, HEAD_PAD - QK_DIM), F32)], axis=-1)
    return keep, jnp.zeros((seq_len, HEAD_PAD), F32)


def _rotate_half(w):
    half = ROPE // 2
    x1, x2 = w[..., NOPE:NOPE + half], w[..., NOPE + half:QK_DIM]
    return jnp.concatenate([jnp.zeros_like(w[..., :NOPE]), -x2, x1, jnp.zeros_like(w[..., QK_DIM:])], axis=-1)


def _swap_rope_halves(g):
    half = ROPE // 2
    lo, hi = g[..., NOPE:NOPE + half], g[..., NOPE + half:QK_DIM]
    return jnp.concatenate([g[..., :NOPE], hi, lo, g[..., QK_DIM:]], axis=-1)


def _pad_cols(w, n):
    return jnp.pad(w, ((0, 0), (0, n - w.shape[1])))


def _layer_weights(l, p):
    hw = HEADS * HEAD_PAD
    wq = jnp.pad(p["mla_w_uq"][l].reshape(Q_RANK, HEADS, QK_DIM), ((0, 0), (0, 0), (0, HEAD_PAD - QK_DIM)))
    wukv = p["mla_w_ukv"][l].reshape(KV_RANK, HEADS, NOPE + V_DIM)
    wk = jnp.pad(wukv[:, :, :NOPE], ((0, 0), (0, 0), (0, HEAD_PAD - NOPE)))
    wv = jnp.pad(wukv[:, :, NOPE:], ((0, 0), (0, 0), (0, HEAD_PAD - V_DIM))).reshape(KV_RANK, hw)
    place = jnp.pad(jnp.eye(ROPE, dtype=F32), ((0, KV_RANK - ROPE), (NOPE, HEAD_PAD - QK_DIM)))
    wk = jnp.concatenate([wk, jnp.tile(place[:, None, :], (1, HEADS, 1))], axis=0)
    with_rot = lambda w: jnp.concatenate([w.reshape(-1, hw), _rotate_half(w).reshape(-1, hw)], axis=1).astype(BF16)
    wpool = jax.scipy.linalg.block_diag(*[p["pool_w"][l, g] for g in range(POOL_GROUPS)])
    vone = (jnp.arange(HEAD_PAD) == V_DIM).astype(F32)[:, None]
    deltas = jnp.abs(jnp.linspace(HY_MIN_DECAY, HY_MAX_DECAY, HY_WIDTH, dtype=F32))[None, :]
    row = lambda v, n=None: (v if n is None else jnp.pad(v, (0, n - v.shape[0])))[None, :]
    twice = lambda v: jnp.concatenate([v, v])[None, :]
    return {
        "norm1_g": row(p["norm1_g"][l]), "norm2_g": row(p["norm2_g"][l]),
        "w_in": _pad_cols(p["w_in"][l], N_IN_PAD).astype(BF16),
        "gq": row(p["mla_q_norm_g"][l]), "wq": with_rot(wq),
        "gkv": row(p["mla_kv_norm_g"][l]), "wk": with_rot(wk), "wvt": wv.T.astype(BF16),
        "gqq": row(p["qk_norm_q"][l], HEAD_PAD), "gqq_rot": _swap_rope_halves(row(p["qk_norm_q"][l], HEAD_PAD)),
        "gqk": row(p["qk_norm_k"][l], HEAD_PAD), "gqk_rot": _swap_rope_halves(row(p["qk_norm_k"][l], HEAD_PAD)),
        "vone": vone,
        "wpool": wpool.astype(BF16), "pool_scale": row(p["pool_scale"][l]),
        "conv_w": p["hy_conv_w"][l], "conv_b": row(p["hy_conv_b"][l]),
        "f_w1a": jnp.pad(p["hy_f_w1"][l], ((0, LANE - HY_EMB), (0, LANE - HY_HID))),
        "f_w1b": jnp.pad(p["hy_f_w1"][l], ((0, LANE - HY_EMB), (LANE - HY_HID, 0))),
        "f_b1": twice(p["hy_f_b1"][l]),
        "f_w2": jax.scipy.linalg.block_diag(p["hy_f_w2"][l], p["hy_f_w2"][l]),
        "f_b2": twice(p["hy_f_b2"][l]),
        "f_w3a": jnp.pad(p["hy_f_w3"][l][:, :HY_WIDTH], ((0, LANE - HY_HID), (0, 0))),
        "f_w3b": jnp.pad(p["hy_f_w3"][l][:, HY_WIDTH:], ((LANE - HY_HID, 0), (0, 0))),
        "f_freq": twice(p["hy_freq"][l]), "f_delta": deltas,
        "hy_bias": row(p["hy_bias"][l]),
        "w_out": p["w_out"][l].astype(BF16),
        "router_w": _split_const(p["router_w"][l].T), "router_b": p["router_b"][l][:, None],
    }


def _layer(l, x, xc, cc, wts, p, rope_main, rope_ctx, last):
    b, s, d = x.shape
    n_ctx = xc.shape[1]
    modm = _modulation(cc, p["w_mod"], p["b_mod"][l][None, :], l)
    names = ("sh1", "sc1", "g1", "sh2", "sc2", "g2")
    mod = {nm: modm[:, i * d:(i + 1) * d].reshape(N_SEG, 1, d) for i, nm in enumerate(names)}
    main_seg = lambda bi: bi
    ctx_seg = lambda bi: CTX_SEG

    pph, q, kl, vl = _inproj(x, main_seg, wts, mod, rope_main)
    pph_c, qc, kc, vc = _inproj(xc, ctx_seg, wts, mod, rope_ctx)
    attn = _attention(q, kc, vc, kl, vl)
    pool, x0, z = _seqmix(pph, wts)
    hy = _hyena_long(z, x0, wts["hy_bias"], _hyena_filter(s, wts))
    n_c = 0 if last else b * n_ctx
    n_tok = n_c + b * s
    init = None if last else (jnp.zeros((n_tok, d // 2), jnp.int32), jnp.zeros((N_EXPERTS, n_tok), F32))
    x1, h2, logits = _outproj(x, pool, hy, attn, main_seg, wts, mod, n_tok, n_c, prior=init)
    if not last:
        attn_c = _attention(qc, kc, vc)
        pool_c, x0c, zc = _seqmix(pph_c, wts)
        hy_c = _hyena_short(zc, x0c, wts["hy_bias"], _hyena_filter(n_ctx, wts))
        xc1, h2, logits = _outproj(xc, pool_c, hy_c, attn_c, ctx_seg, wts, mod, n_tok, 0, prior=(h2, logits))

    gates, dest_t, block_e, block_rows = _route(logits)
    de2 = p["moe_w1"].shape[-1]
    xs = _sc_scatter(h2, dest_t, block_e.shape[0] * EXPERT_TILE)
    ys = _experts(xs, block_e, block_rows, p["moe_w1"], p["moe_b1"].reshape(-1, N_EXPERTS, 1, de2),
                  p["moe_w2"], p["moe_b2"].reshape(-1, N_EXPERTS, 1, d), l)
    ysel = _sc_gather(ys, dest_t.reshape(-1)).reshape(TOP_K, -1, d // 2)
    x2 = _combine(x1, ysel, gates, n_c, main_seg, mod)
    if not last:
        xc = _combine(xc1, ysel, gates, 0, ctx_seg, mod)
    return x2, xc


def kernel(x, c, ctx, c_ctx, norm1_g, norm2_g, w_mod, b_mod, w_in, pool_w, pool_scale, hy_conv_w, hy_conv_b,
           hy_f_w1, hy_f_b1, hy_f_w2, hy_f_b2, hy_f_w3, hy_freq, hy_bias, mla_q_norm_g, mla_w_uq, mla_kv_norm_g,
           mla_w_ukv, qk_norm_q, qk_norm_k, w_out, router_w, router_b, moe_w1, moe_b1, moe_w2, moe_b2):
    p = dict(norm1_g=norm1_g, norm2_g=norm2_g, w_mod=w_mod, b_mod=b_mod, w_in=w_in, pool_w=pool_w,
             pool_scale=pool_scale,
             hy_conv_w=hy_conv_w, hy_conv_b=hy_conv_b, hy_f_w1=hy_f_w1, hy_f_b1=hy_f_b1, hy_f_w2=hy_f_w2,
             hy_f_b2=hy_f_b2, hy_f_w3=hy_f_w3, hy_freq=hy_freq, hy_bias=hy_bias, mla_q_norm_g=mla_q_norm_g,
             mla_w_uq=mla_w_uq, mla_kv_norm_g=mla_kv_norm_g, mla_w_ukv=mla_w_ukv, qk_norm_q=qk_norm_q,
             qk_norm_k=qk_norm_k, w_out=w_out, router_w=router_w, router_b=router_b, moe_w1=moe_w1,
             moe_b1=moe_b1, moe_w2=moe_w2, moe_b2=moe_b2)
    b, s, d = x.shape
    depth = w_mod.shape[0]
    assert b <= CTX_SEG
    cc = jnp.concatenate([c, jnp.zeros((CTX_SEG - b, d), F32), c_ctx[None, :],
                          jnp.zeros((N_SEG - CTX_SEG - 1, d), F32)], axis=0)
    rope_main = _rope_tables(s)
    rope_ctx = _identity_rope(ctx.shape[1])
    xc = ctx
    for l in range(depth):
        wts = _layer_weights(l, p)
        x, xc = _layer(l, x, xc, cc, wts, p, rope_main, rope_ctx, l == depth - 1)
    return x
```

```python
import functools
import math

import jax
import jax.numpy as jnp
import numpy as np
from jax import lax
from jax.experimental import pallas as pl
from jax.experimental.pallas import tpu as pltpu
from jax.experimental.pallas import tpu_sc as plsc

F32 = jnp.float32
BF16 = jnp.bfloat16
HIGHEST = lax.Precision.HIGHEST

D_MODEL = 1024
GRID_W = 64
EPS = 1e-6
POOL_WIDTH = 256
POOL_GROUPS = 4
POOL_GROUP_DIM = POOL_WIDTH // POOL_GROUPS
HY_WIDTH = 256
HY_IN = 3 * HY_WIDTH
HY_BANDS = 8
HY_EMB = 1 + 2 * HY_BANDS
HY_HID = 64
HY_MIN_DECAY = math.log(1e-2) / 1.5
HY_MAX_DECAY = math.log(1e-2) / 0.3
HEADS = 8
Q_RANK = 256
KV_RANK = 128
NOPE = 64
ROPE = 32
V_DIM = 64
QK_DIM = NOPE + ROPE
MLA_WIDTH = HEADS * V_DIM
ATTN_SCALE = QK_DIM ** -0.5
LOG2E = math.log2(math.e)
OFF_HY = POOL_WIDTH
OFF_Q = OFF_HY + HY_IN
OFF_KV = OFF_Q + Q_RANK
OFF_KR = OFF_KV + KV_RANK
N_IN = OFF_KR + ROPE
N_IN_PAD = OFF_KR + 128
ROPE_BASE = 10000.0
N_EXPERTS = 32
TOP_K = 4
D_EXPERT = 1024
SWIGLU_LIMIT = 7.0
SWIGLU_ALPHA = 1.702

LANE = 128
HEAD_PAD = LANE
ROW_TILE = 512
EXPERT_TILE = 512
ATTN_KEY_CHUNK = 256
ATTN_CHUNKS_PER_TRIP = 16
MOD_COL_TILE = 1536
SUBLANE = 8
DFT_BLOCKS_PER_STEP = 8
FILTER_ROWS = 1024
VMEM_LIMIT = 56 * 1024 * 1024
N_SEG = 8
CTX_SEG = 4


def _cparams(*sem):
    return pltpu.CompilerParams(dimension_semantics=sem, vmem_limit_bytes=VMEM_LIMIT)


def _rms(x, g, n):
    ss = jnp.sum(x * x, axis=-1, keepdims=True) * (1.0 / n)
    return x * lax.rsqrt(ss + EPS) * g


def _pack_rows(x):
    half = x.shape[1] // 2
    bits = lax.bitcast_convert_type(x.astype(BF16).astype(F32), jnp.int32)
    return bits[:, :half] | lax.shift_right_logical(bits[:, half:], 16)


def _unpack_rows(w):
    hi = lax.bitcast_convert_type(w & jnp.int32(-65536), F32)
    lo = lax.bitcast_convert_type(lax.shift_left(w, 16), F32)
    return jnp.concatenate([hi, lo], axis=1).astype(BF16)


def _mod_kernel(c_ref, w_ref, b_ref, o_ref):
    c = c_ref[...]
    s = c * (1.0 / (1.0 + jnp.exp(-c)))
    o_ref[...] = jnp.dot(s, w_ref[0], precision=HIGHEST, preferred_element_type=F32) + b_ref[...]


def _modulation(cc, w, b, layer):
    _, d, n = w.shape
    tn = MOD_COL_TILE
    return pl.pallas_call(
        _mod_kernel,
        grid=(n // tn,),
        in_specs=[pl.BlockSpec((N_SEG, d), lambda j: (0, 0)),
                  pl.BlockSpec((1, d, tn), lambda j: (layer, 0, j)),
                  pl.BlockSpec((1, tn), lambda j: (0, j))],
        out_specs=pl.BlockSpec((N_SEG, tn), lambda j: (0, j)),
        out_shape=jax.ShapeDtypeStruct((N_SEG, n), F32),
        compiler_params=_cparams("arbitrary"),
        name="modulation",
    )(cc, w, b)


def _inproj_kernel(x_ref, gn_ref, sc_ref, sh_ref, win_ref, gq_ref, wq_ref, gkv_ref, wk_ref, wvt_ref,
                   gqq_ref, gqqr_ref, gqk_ref, gqkr_ref, vone_ref, rc_ref, rs_ref,
                   pph_ref, q_ref, k_ref, vt_ref):
    x = x_ref[0]
    h = _rms(x, gn_ref[...], D_MODEL) * (1.0 + sc_ref[0]) + sh_ref[0]
    p = jnp.dot(h.astype(BF16), win_ref[...], preferred_element_type=F32)
    pph_ref[0] = p[:, :OFF_Q]
    cq = _rms(p[:, OFF_Q:OFF_KV], gq_ref[...], Q_RANK)
    qall = jnp.dot(cq.astype(BF16), wq_ref[...], preferred_element_type=F32)
    ckv_b = _rms(p[:, OFF_KV:OFF_KR], gkv_ref[...], KV_RANK).astype(BF16)
    kin = jnp.concatenate([ckv_b, p[:, OFF_KR:N_IN_PAD].astype(BF16)], axis=-1)
    kall = jnp.dot(kin, wk_ref[...], preferred_element_type=F32)
    vt_all = lax.dot_general(wvt_ref[...], ckv_b, (((1,), (1,)), ((), ())), preferred_element_type=F32)
    rc, rs = rc_ref[...], rs_ref[...]
    hw = HEADS * HEAD_PAD

    def norm_rope(x_all, hd, g_ref, gr_ref):
        xh = x_all[:, HEAD_PAD * hd:HEAD_PAD * (hd + 1)]
        xr = x_all[:, hw + HEAD_PAD * hd:hw + HEAD_PAD * (hd + 1)]
        r = lax.rsqrt(jnp.sum(xh * xh, axis=-1, keepdims=True) * (1.0 / QK_DIM) + EPS)
        return (xh * (g_ref[...] * rc) + xr * (gr_ref[...] * rs)) * r

    for hd in range(HEADS):
        q_ref[0, hd] = (norm_rope(qall, hd, gqq_ref, gqqr_ref) * (ATTN_SCALE * LOG2E)).astype(BF16)
        k_ref[0, hd] = norm_rope(kall, hd, gqk_ref, gqkr_ref).astype(BF16)
        vt_ref[0, hd] = (vt_all[HEAD_PAD * hd:HEAD_PAD * (hd + 1)] + vone_ref[...]).astype(BF16)


def _inproj(x, seg_of_batch, wts, mod, rope_tabs):
    b, s, d = x.shape
    t = min(ROW_TILE, s)
    full = lambda shape: pl.BlockSpec(shape, lambda bi, i: (0,) * len(shape))
    seg = lambda: pl.BlockSpec((1, 1, d), lambda bi, i: (seg_of_batch(bi), 0, 0))
    tab = lambda: pl.BlockSpec((t, HEAD_PAD), lambda bi, i: (i, 0))
    hw = HEADS * HEAD_PAD
    head_out = lambda: pl.BlockSpec((1, HEADS, t, HEAD_PAD), lambda bi, i: (bi, 0, i, 0))
    return pl.pallas_call(
        _inproj_kernel,
        grid=(b, s // t),
        in_specs=[pl.BlockSpec((1, t, d), lambda bi, i: (bi, i, 0)),
                  full((1, d)), seg(), seg(),
                  full((d, N_IN_PAD)), full((1, Q_RANK)), full((Q_RANK, 2 * hw)),
                  full((1, KV_RANK)), full((2 * KV_RANK, 2 * hw)), full((hw, KV_RANK)),
                  full((1, HEAD_PAD)), full((1, HEAD_PAD)), full((1, HEAD_PAD)), full((1, HEAD_PAD)),
                  full((HEAD_PAD, 1)), tab(), tab()],
        out_specs=[pl.BlockSpec((1, t, OFF_Q), lambda bi, i: (bi, i, 0)),
                   head_out(), head_out(),
                   pl.BlockSpec((1, HEADS, HEAD_PAD, t), lambda bi, i: (bi, 0, 0, i))],
        out_shape=[jax.ShapeDtypeStruct((b, s, OFF_Q), F32),
                   jax.ShapeDtypeStruct((b, HEADS, s, HEAD_PAD), BF16),
                   jax.ShapeDtypeStruct((b, HEADS, s, HEAD_PAD), BF16),
                   jax.ShapeDtypeStruct((b, HEADS, HEAD_PAD, s), BF16)],
        compiler_params=_cparams("parallel", "parallel"),
        name="inproj",
    )(x, wts["norm1_g"], mod["sc1"], mod["sh1"], wts["w_in"], wts["gq"], wts["wq"], wts["gkv"], wts["wk"],
      wts["wvt"], wts["gqq"], wts["gqq_rot"], wts["gqk"], wts["gqk_rot"], wts["vone"], *rope_tabs)


def _seqmix_kernel(prev_ref, cur_ref, next_ref, wpool_ref, pscale_ref, cw_ref, cb_ref,
                   pool_ref, x0_ref, z_ref, *, seq_len):
    i = pl.program_id(1)
    n = pl.num_programs(1)
    t_rows = cur_ref.shape[1]
    halo = prev_ref.shape[1]
    r_rows = t_rows + 2 * halo
    prev = jnp.where(i > 0, prev_ref[0], 0.0)
    nxt = jnp.where(i < n - 1, next_ref[0], 0.0)
    ext = jnp.concatenate([prev, cur_ref[0], nxt], axis=0)

    def sh(a, dlt):
        return pltpu.roll(a, (-dlt) % r_rows, axis=0)

    u = ext[:, :POOL_WIDTH]
    a2 = u + sh(u, -1)
    a4 = sh(a2, -1) + sh(a2, 1)
    a8 = sh(a4, -2) + sh(a4, 2)
    a16 = sh(a8, -4) + sh(a8, 4)
    grp = lax.broadcasted_iota(jnp.int32, (1, POOL_WIDTH), 1) // POOL_GROUP_DIM
    win = jnp.where(grp == 0, a2, jnp.where(grp == 1, a4, jnp.where(grp == 2, a8, a16)))
    half = jnp.where(grp == 0, 1, jnp.where(grp == 1, 2, jnp.where(grp == 2, 4, 8)))
    pos = i * t_rows - halo + lax.broadcasted_iota(jnp.int32, (r_rows, 1), 0)
    cnt = jnp.minimum(pos + half, seq_len) - jnp.maximum(pos - half, 0)
    cnt = jnp.maximum(cnt, 1).astype(F32)
    dev = (win / cnt - u)[halo:halo + t_rows]
    pool = jnp.dot(dev.astype(BF16), wpool_ref[...], preferred_element_type=F32) * pscale_ref[...]
    pool_ref[0] = pool.astype(BF16)

    e = ext[:, OFF_HY:OFF_Q]
    cw = cw_ref[...]
    uc = sh(e, -1) * cw[0:1] + e * cw[1:2] + sh(e, 1) * cw[2:3] + cb_ref[...]
    uc = uc[halo:halo + t_rows]
    x0_ref[0] = uc[:, :HY_WIDTH]
    z_ref[0] = uc[:, HY_WIDTH:2 * HY_WIDTH] * uc[:, 2 * HY_WIDTH:]


def _seqmix(pph, wts):
    b, s, w = pph.shape
    t = min(ROW_TILE, s)
    halo = SUBLANE
    per = t // halo
    nh = s // halo
    full = lambda shape: pl.BlockSpec(shape, lambda bi, i: (0,) * len(shape))
    out = lambda: pl.BlockSpec((1, t, HY_WIDTH), lambda bi, i: (bi, i, 0))
    return pl.pallas_call(
        functools.partial(_seqmix_kernel, seq_len=s),
        grid=(b, s // t),
        in_specs=[pl.BlockSpec((1, halo, w), lambda bi, i: (bi, jnp.maximum(i * per - 1, 0), 0)),
                  pl.BlockSpec((1, t, w), lambda bi, i: (bi, i, 0)),
                  pl.BlockSpec((1, halo, w), lambda bi, i: (bi, jnp.minimum((i + 1) * per, nh - 1), 0)),
                  full((POOL_WIDTH, POOL_WIDTH)), full((1, POOL_WIDTH)), full((3, HY_IN)), full((1, HY_IN))],
        out_specs=[out(), out(), out()],
        out_shape=[jax.ShapeDtypeStruct((b, s, POOL_WIDTH), BF16),
                   jax.ShapeDtypeStruct((b, s, HY_WIDTH), F32),
                   jax.ShapeDtypeStruct((b, s, HY_WIDTH), F32)],
        compiler_params=_cparams("parallel", "parallel"),
        name="seqmix",
    )(pph, pph, pph, wts["wpool"], wts["pool_scale"], wts["conv_w"], wts["conv_b"])


def _dot3(m_ref, x):
    x_hi = x.astype(BF16)
    x_lo = (x - x_hi.astype(F32)).astype(BF16)
    m_hi, m_lo = m_ref[0], m_ref[1]
    return (jnp.dot(m_hi, x_hi, preferred_element_type=F32) + jnp.dot(m_hi, x_lo, preferred_element_type=F32)
            + jnp.dot(m_lo, x_hi, preferred_element_type=F32))


def _split_const(m):
    hi = m.astype(BF16)
    return jnp.stack([hi, (m - hi.astype(F32)).astype(BF16)])


def _leftmm_kernel(*refs, cmul, gate):
    m_ref, x_ref = refs[0], refs[1]
    nxt = 2
    x = x_ref[0]
    if cmul:
        ks = refs[nxt][...]
        nxt += 1
        kh = x.shape[0] // 2
        xr, xi, kr, ki = x[:kh], x[kh:], ks[:kh], ks[kh:]
        x = jnp.concatenate([xr * kr - xi * ki, xr * ki + xi * kr], axis=0)
    y = jnp.dot(m_ref[...], x, precision=HIGHEST, preferred_element_type=F32)
    if gate:
        z_ref, bias_ref, x0_ref = refs[nxt:nxt + 3]
        nxt += 3
        y = (y + z_ref[0] * bias_ref[...]) * x0_ref[0]
    o_ref = refs[nxt]
    o_ref[0] = y.astype(o_ref.dtype)


def _leftmm(m, x, spectrum=None, gate=None, out_dtype=F32):
    g, k, cols = x.shape
    mo = m.shape[0]
    tc = min(cols, 2048)
    args = [m, x]
    in_specs = [pl.BlockSpec((mo, k), lambda gi, j: (0, 0)), pl.BlockSpec((1, k, tc), lambda gi, j: (gi, 0, j))]
    if spectrum is not None:
        args.append(spectrum)
        in_specs.append(pl.BlockSpec((k, tc), lambda gi, j: (0, j)))
    if gate is not None:
        z, bias, x0 = gate
        args += [z, bias, x0]
        in_specs += [pl.BlockSpec((1, mo, tc), lambda gi, j: (gi, 0, j)),
                     pl.BlockSpec((1, tc), lambda gi, j: (0, j)),
                     pl.BlockSpec((1, mo, tc), lambda gi, j: (gi, 0, j))]
    return pl.pallas_call(
        functools.partial(_leftmm_kernel, cmul=spectrum is not None, gate=gate is not None),
        grid=(g, cols // tc),
        in_specs=in_specs,
        out_specs=pl.BlockSpec((1, mo, tc), lambda gi, j: (gi, 0, j)),
        out_shape=jax.ShapeDtypeStruct((g, mo, cols), out_dtype),
        compiler_params=_cparams("parallel", "parallel"),
        name="dft_leftmm",
    )(*args)


def _blockfft_kernel(*refs, n2, blocks, conv):
    if conv:
        a_ref, tw_ref, ks_ref, m2_ref, m2c_ref, o_ref = refs
    else:
        a_ref, tw_ref, m2_ref, o_ref = refs

    def body(j, carry):
        rows = pl.ds(j * n2, n2)
        ar, ai = a_ref[0, 0, rows, :], a_ref[0, 1, rows, :]
        tcs, tsn = tw_ref[0, rows, :], tw_ref[1, rows, :]
        tcs = jnp.concatenate([tcs] * (ar.shape[1] // tcs.shape[1]), axis=1)
        tsn = jnp.concatenate([tsn] * (ar.shape[1] // tsn.shape[1]), axis=1)
        bstk = jnp.concatenate([ar * tcs + ai * tsn, ai * tcs - ar * tsn], axis=0)
        xs = _dot3(m2_ref, bstk)
        if not conv:
            o_ref[0, 0, rows, :] = xs[:n2]
            o_ref[0, 1, rows, :] = xs[n2:]
            return carry
        xr, xi = xs[:n2], xs[n2:]
        kr, ki = ks_ref[0, rows, :], ks_ref[1, rows, :]
        ystk = jnp.concatenate([xr * kr - xi * ki, xr * ki + xi * kr], axis=0)
        cs = _dot3(m2c_ref, ystk)
        cr, ci = cs[:n2], cs[n2:]
        o_ref[0, 0, rows, :] = cr * tcs - ci * tsn
        o_ref[0, 1, rows, :] = cr * tsn + ci * tcs
        return carry

    for j in range(blocks):
        body(j, 0)


def _blockfft(a, tw, m2, spectrum=None, m2c=None):
    g, _, n, c = a.shape
    n2 = m2.shape[1] // 2
    blocks = min(DFT_BLOCKS_PER_STEP, n // n2)
    rows = blocks * n2
    conv = spectrum is not None
    args = [a, tw]
    in_specs = [pl.BlockSpec((1, 2, rows, c), lambda gi, j: (gi, 0, j, 0)),
                pl.BlockSpec((2, rows, LANE), lambda gi, j: (0, j, 0))]
    if conv:
        args.append(spectrum)
        in_specs.append(pl.BlockSpec((2, rows, c), lambda gi, j: (0, j, 0)))
    args.append(m2)
    in_specs.append(pl.BlockSpec(m2.shape, lambda gi, j: (0, 0, 0)))
    if conv:
        args.append(m2c)
        in_specs.append(pl.BlockSpec(m2c.shape, lambda gi, j: (0, 0, 0)))
    return pl.pallas_call(
        functools.partial(_blockfft_kernel, n2=n2, blocks=blocks, conv=conv),
        grid=(g, n // rows),
        in_specs=in_specs,
        out_specs=pl.BlockSpec((1, 2, rows, c), lambda gi, j: (gi, 0, j, 0)),
        out_shape=jax.ShapeDtypeStruct((g, 2, n, c), F32),
        compiler_params=_cparams("parallel", "parallel"),
        name="dft_block",
    )(*args)


def _filter_kernel(za_ref, zb_ref, w1a_ref, w1b_ref, b1_ref, w2_ref, b2_ref, w3a_ref, w3b_ref, fr_ref, dl_ref,
                   o_ref):
    i = pl.program_id(0)
    za, zb = za_ref[...], zb_ref[...]
    fr = fr_ref[...]
    dot = functools.partial(jnp.dot, precision=HIGHEST, preferred_element_type=F32)
    h = jnp.sin(fr * (dot(za, w1a_ref[...]) + dot(zb, w1b_ref[...]) + b1_ref[...]))
    h = jnp.sin(fr * (dot(h, w2_ref[...]) + b2_ref[...]))
    o_ref[0] = dot(h, w3a_ref[...]) * jnp.exp(-za[:, 0:1] * dl_ref[...])
    first = (i == 0) & (lax.broadcasted_iota(jnp.int32, (zb.shape[0], 1), 0) == 0)
    o_ref[1] = jnp.where(first, 0.0, dot(h, w3b_ref[...]) * jnp.exp(-zb[:, 0:1] * dl_ref[...]))


def _hyena_filter(seq_len, wts):
    n = 2 * seq_len
    j = jnp.arange(n)
    lag = jnp.clip(jnp.where(j < seq_len, j, n - j), 0, seq_len - 1).astype(F32)[:, None]
    t = lag / (seq_len - 1)
    wpos = (2.0 * math.pi / seq_len) * lag
    f = jnp.linspace(1e-4, HY_BANDS - 1, HY_BANDS, dtype=F32)[None, :]
    feat = jnp.concatenate([t, jnp.cos(f * wpos), -jnp.sin(f * wpos)], axis=-1)
    feat2 = jnp.pad(feat, ((0, 0), (0, LANE - HY_EMB)))
    rows = min(seq_len, FILTER_ROWS)
    half = seq_len // rows
    full = lambda shape: pl.BlockSpec(shape, lambda i: (0,) * len(shape))
    taps = pl.pallas_call(
        _filter_kernel,
        grid=(half,),
        in_specs=[pl.BlockSpec((rows, LANE), lambda i: (i, 0)), pl.BlockSpec((rows, LANE), lambda i: (half + i, 0)),
                  full((LANE, LANE)), full((LANE, LANE)), full((1, LANE)), full((LANE, LANE)), full((1, LANE)),
                  full((LANE, HY_WIDTH)), full((LANE, HY_WIDTH)), full((1, LANE)), full((1, HY_WIDTH))],
        out_specs=pl.BlockSpec((2, rows, HY_WIDTH), lambda i: (0, i, 0)),
        out_shape=jax.ShapeDtypeStruct((2, seq_len, HY_WIDTH), F32),
        compiler_params=_cparams("parallel"),
        name="hyena_filter",
    )(feat2, feat2, wts["f_w1a"], wts["f_w1b"], wts["f_b1"], wts["f_w2"], wts["f_b2"], wts["f_w3a"], wts["f_w3b"],
      wts["f_freq"], wts["f_delta"])
    return taps.reshape(n, HY_WIDTH)


def _cs(p, rows, cols):
    ang = 2.0 * np.pi * np.outer(np.arange(rows), np.arange(cols)) / p
    return np.cos(ang), np.sin(ang)


def _mat(blocks):
    return jnp.asarray(np.block(blocks), dtype=F32)


def _dftcols_kernel(*refs, gate):
    m_ref, x_ref = refs[0], refs[1]
    x = x_ref[0]
    k, nt, _ = x.shape
    xt = pltpu.einshape("knc->(nk)c", x)
    ys = [_dot3(m_ref, xt[n * k:(n + 1) * k]) for n in range(nt)]
    y = pltpu.einshape("(nk)c->knc", jnp.concatenate(ys, axis=0), n=nt)
    if gate:
        z_ref, bias_ref, x0_ref, o_ref = refs[2:]
        y = (y + z_ref[0] * bias_ref[...]) * x0_ref[0]
    else:
        o_ref = refs[2]
    o_ref[0] = y.astype(o_ref.dtype)


def _dftcols(m, x, gate=None, out_dtype=F32):
    g, k, n2, c = x.shape
    mo = m.shape[1]
    nt = 16 if n2 % 16 == 0 else 8
    blk = lambda rows: pl.BlockSpec((1, rows, nt, c), lambda gi, j: (gi, 0, j, 0))
    args = [m, x]
    in_specs = [pl.BlockSpec((2, mo, k), lambda gi, j: (0, 0, 0)), blk(k)]
    if gate is not None:
        z, bias, x0 = gate
        args += [z, bias, x0]
        in_specs += [blk(mo), pl.BlockSpec((1, c), lambda gi, j: (0, 0)), blk(mo)]
    return pl.pallas_call(
        functools.partial(_dftcols_kernel, gate=gate is not None),
        grid=(g, n2 // nt),
        in_specs=in_specs,
        out_specs=blk(mo),
        out_shape=jax.ShapeDtypeStruct((g, mo, n2, c), out_dtype),
        compiler_params=_cparams("parallel", "parallel"),
        name="dft_cols",
    )(*args)


def _hyena_long(z, x0, bias, filt):
    b, s, c = z.shape
    n = 2 * s
    n1 = 1 << ((n.bit_length() - 1 + 1) // 2)
    n2 = n // n1
    g = b // 2
    c1, s1 = _cs(n1, n1, n1)
    h1 = n1 // 2
    m_fwd = _split_const(_mat([[c1[:, :h1], s1[:, :h1]], [-s1[:, :h1], c1[:, :h1]]]))
    m_filt = _split_const(_mat([[c1], [-s1]]))
    m_inv = _split_const(_mat([[c1[:h1], -s1[:h1]], [s1[:h1], c1[:h1]]]) * (1.0 / n))
    c2, s2 = _cs(n2, n2, n2)
    m2 = _split_const(_mat([[c2, s2], [-s2, c2]]))
    m2c = _split_const(_mat([[c2, -s2], [s2, c2]]))
    k1 = jnp.arange(n1, dtype=jnp.int32)[:, None]
    nn2 = jnp.arange(n2, dtype=jnp.int32)[None, :]
    ang = (2.0 * math.pi / n) * (k1 * nn2).astype(F32).reshape(n)
    tw = jnp.broadcast_to(jnp.stack([jnp.cos(ang), jnp.sin(ang)])[:, :, None], (2, n, LANE))

    spec = _dftcols(m_filt, filt.reshape(1, n1, n2, c)).reshape(1, 2, n, c)
    spec = _blockfft(spec, tw, m2)[0]
    zv = z.reshape(g, n1, n2, c)
    a = _dftcols(m_fwd, zv).reshape(g, 2, n, c)
    dd = _blockfft(a, tw, m2, spectrum=spec, m2c=m2c).reshape(g, 2 * n1, n2, c)
    y = _dftcols(m_inv, dd, gate=(zv, bias, x0.reshape(g, n1, n2, c)), out_dtype=BF16)
    return y.reshape(b, s, c)


def _hyena_short(z, x0, bias, filt):
    b, s, c = z.shape
    n = 2 * s
    g = b // 2
    cf, sf = _cs(n, n, n)
    m_fwd = _mat([[cf[:, :s], sf[:, :s]], [-sf[:, :s], cf[:, :s]]])
    m_filt = _mat([[cf], [-sf]])
    m_inv = _mat([[cf[:s], -sf[:s]], [sf[:s], cf[:s]]]) * (1.0 / n)
    spec = _leftmm(m_filt, filt.reshape(1, n, c))[0]
    zv = z.reshape(g, n, c)
    zf = _leftmm(m_fwd, zv)
    y = _leftmm(m_inv, zf, spectrum=spec, gate=(zv, bias, x0.reshape(g, n, c)), out_dtype=BF16)
    return y.reshape(b, s, c)


def _attn_kernel(*refs, tk, n_chunks, heads, unroll):
    if n_chunks:
        q_ref, kc_ref, vc_ref, kl_ref, vl_ref, o_ref, s_ref, sc_ref = refs
    else:
        q_ref, kc_ref, vc_ref, o_ref = refs
    dn = (((1,), (1,)), ((), ()))
    qs = [q_ref[0, hh] for hh in range(heads)]

    def chunk_rows(j):
        return pl.ds(pl.multiple_of(j * tk, tk), tk)

    def scores(j, slot):
        for hh in range(heads):
            s_ref[slot, hh] = lax.dot_general(kl_ref[0, hh, chunk_rows(j), :], qs[hh], dn,
                                              preferred_element_type=F32)

    def absorb(j, slot, carry):
        out = []
        for hh in range(heads):
            m_old, acc_old = carry[hh]
            sj = s_ref[slot, hh]
            m_new = jnp.maximum(m_old, jnp.max(sj, axis=0, keepdims=True))
            pj = jnp.exp2(sj - m_new)
            acc_new = jnp.exp2(m_old - m_new) * acc_old + jnp.dot(vl_ref[0, hh, :, chunk_rows(j)], pj.astype(BF16),
                                                                  preferred_element_type=F32)
            out.append((m_new, acc_new))
        return tuple(out)

    def ctx_scores(hh):
        return lax.dot_general(kc_ref[0, hh], qs[hh], dn, preferred_element_type=F32)

    def merge_ctx(s, hh, m_old, acc_old):
        m_new = jnp.maximum(m_old, jnp.max(s, axis=0, keepdims=True))
        p = jnp.exp2(s - m_new)
        return m_new, jnp.exp2(m_old - m_new) * acc_old + jnp.dot(vc_ref[0, hh], p.astype(BF16),
                                                                  preferred_element_type=F32)

    def body(jj, carry):
        j = unroll * jj
        for u in range(unroll):
            nxt = j + u + 1
            if u == unroll - 1:
                nxt = jnp.minimum(nxt, n_chunks - 1)
            scores(nxt, (u + 1) % 2)
            carry = absorb(j + u, u % 2, carry)
        return carry

    tq = qs[0].shape[0]
    empty = (jnp.full((1, tq), -jnp.inf, F32), jnp.zeros((vc_ref.shape[2], tq), F32))
    if n_chunks:
        scores(0, 0)
        for hh in range(heads):
            sc_ref[hh] = ctx_scores(hh)
        state = lax.fori_loop(0, n_chunks // unroll, body, (empty,) * heads)
        state = [merge_ctx(sc_ref[hh], hh, *state[hh]) for hh in range(heads)]
    else:
        state = [merge_ctx(ctx_scores(hh), hh, *empty) for hh in range(heads)]
    outs = [acc[:V_DIM] / acc[V_DIM:V_DIM + 1] for _, acc in state]
    o_ref[0] = jnp.concatenate(outs, axis=0).T.astype(o_ref.dtype)


def _attention(q, kc, vct, kl=None, vlt=None):
    b, h, s, dp = q.shape
    nc = kc.shape[2]
    tq = min(ROW_TILE, s)
    heads = LANE // V_DIM
    tk = ATTN_KEY_CHUNK
    n_chunks = 0 if kl is None else kl.shape[2] // tk
    unroll = ATTN_CHUNKS_PER_TRIP if n_chunks % ATTN_CHUNKS_PER_TRIP == 0 else 2
    args = [q, kc, vct]
    in_specs = [pl.BlockSpec((1, heads, tq, dp), lambda bi, hi, i: (bi, hi, i, 0)),
                pl.BlockSpec((1, heads, nc, dp), lambda bi, hi, i: (bi, hi, 0, 0)),
                pl.BlockSpec((1, heads, dp, nc), lambda bi, hi, i: (bi, hi, 0, 0))]
    scratch = []
    if n_chunks:
        assert unroll % 2 == 0 and n_chunks % unroll == 0
        sl = kl.shape[2]
        args += [kl, vlt]
        in_specs += [pl.BlockSpec((1, heads, sl, dp), lambda bi, hi, i: (bi, hi, 0, 0)),
                     pl.BlockSpec((1, heads, dp, sl), lambda bi, hi, i: (bi, hi, 0, 0))]
        scratch = [pltpu.VMEM((2, heads, tk, tq), F32), pltpu.VMEM((heads, nc, tq), F32)]
    return pl.pallas_call(
        functools.partial(_attn_kernel, tk=tk, n_chunks=n_chunks, heads=heads, unroll=unroll),
        grid=(b, h // heads, s // tq),
        in_specs=in_specs,
        out_specs=pl.BlockSpec((1, tq, heads * V_DIM), lambda bi, hi, i: (bi, i, hi)),
        out_shape=jax.ShapeDtypeStruct((b, s, h * V_DIM), BF16),
        scratch_shapes=scratch,
        compiler_params=_cparams("parallel", "parallel", "parallel"),
        name="attention",
    )(*args)


def _outproj_kernel(x_ref, pool_ref, hy_ref, at_ref, wout_ref, g1_ref, gn_ref, sc_ref, sh_ref, rw_ref, rb_ref,
                    *rest):
    x1_ref, h2_ref, lg_ref = rest[-3:]
    mix = jnp.concatenate([pool_ref[0], hy_ref[0], at_ref[0]], axis=-1)
    y = jnp.dot(mix, wout_ref[...], preferred_element_type=F32)
    x1 = x_ref[0] + g1_ref[0] * y
    x1_ref[0] = x1
    h2 = _rms(x1, gn_ref[...], D_MODEL) * (1.0 + sc_ref[0]) + sh_ref[0]
    h2_ref[...] = _pack_rows(h2)
    h_hi = h2.astype(BF16)
    h_lo = (h2 - h_hi.astype(F32)).astype(BF16)
    nt = functools.partial(lax.dot_general, dimension_numbers=(((1,), (1,)), ((), ())), preferred_element_type=F32)
    lg_ref[...] = nt(rw_ref[0], h_hi) + nt(rw_ref[0], h_lo) + nt(rw_ref[1], h_hi) + rb_ref[...]


def _outproj(x, pool, hy, attn, seg_of_batch, wts, mod, n_tok, tok_offset, prior=None):
    b, s, d = x.shape
    t = min(ROW_TILE, s)
    per = s // t
    off = tok_offset // t
    full = lambda shape: pl.BlockSpec(shape, lambda bi, i: (0,) * len(shape))
    seg = lambda: pl.BlockSpec((1, 1, d), lambda bi, i: (seg_of_batch(bi), 0, 0))
    rowblk = lambda w: pl.BlockSpec((1, t, w), lambda bi, i: (bi, i, 0))
    args = [x, pool, hy, attn, wts["w_out"], mod["g1"], wts["norm2_g"], mod["sc2"], mod["sh2"],
            wts["router_w"], wts["router_b"]]
    in_specs = [rowblk(d), rowblk(POOL_WIDTH), rowblk(HY_WIDTH), rowblk(MLA_WIDTH),
                full((d, d)), seg(), full((1, d)), seg(), seg(), full((2, N_EXPERTS, d)), full((N_EXPERTS, 1))]
    aliases = {}
    if prior is not None:
        aliases = {len(args): 1, len(args) + 1: 2}
        args += list(prior)
        in_specs += [pl.BlockSpec(memory_space=pl.ANY)] * 2
    return pl.pallas_call(
        _outproj_kernel,
        grid=(b, per),
        in_specs=in_specs,
        out_specs=[rowblk(d), pl.BlockSpec((t, d // 2), lambda bi, i: (off + bi * per + i, 0)),
                   pl.BlockSpec((N_EXPERTS, t), lambda bi, i: (0, off + bi * per + i))],
        out_shape=[jax.ShapeDtypeStruct((b, s, d), F32), jax.ShapeDtypeStruct((n_tok, d // 2), jnp.int32),
                   jax.ShapeDtypeStruct((N_EXPERTS, n_tok), F32)],
        input_output_aliases=aliases,
        compiler_params=_cparams("parallel", "parallel"),
        name="outproj",
    )(*args)


def _moe_kernel(be_ref, br_ref, x_ref, w1_ref, b1_ref, w2_ref, b2_ref, o_ref, w1b_ref, w2b_ref):
    i = pl.program_id(0)
    e = be_ref[i]
    e_prev = be_ref[jnp.maximum(i - 1, 0)]
    n_rows = br_ref[i]

    @pl.when((i == 0) | (e != e_prev))
    def _():
        w1b_ref[...] = w1_ref[0, 0].astype(BF16)
        w2b_ref[...] = w2_ref[0, 0].astype(BF16)

    @pl.when(n_rows > 0)
    def _():
        live = lax.broadcasted_iota(jnp.int32, (x_ref.shape[0], 1), 0) < n_rows
        x = _unpack_rows(jnp.where(live, x_ref[...], 0))
        a = jnp.dot(x, w1b_ref[...], preferred_element_type=F32) + b1_ref[0, 0]
        glu = jnp.minimum(a[:, :D_EXPERT], SWIGLU_LIMIT)
        lin = jnp.clip(a[:, D_EXPERT:], -SWIGLU_LIMIT, SWIGLU_LIMIT)
        act = glu * (1.0 / (1.0 + jnp.exp(-SWIGLU_ALPHA * glu))) * (lin + 1.0)
        y = jnp.dot(act.astype(BF16), w2b_ref[...], preferred_element_type=F32) + b2_ref[0, 0]
        o_ref[...] = _pack_rows(y)

    @pl.when(n_rows == 0)
    def _():
        o_ref[...] = jnp.zeros_like(o_ref)


def _experts(xs, block_e, block_rows, w1, b1, w2, b2, layer):
    n_buf, dh = xs.shape
    d = 2 * dh
    tm = EXPERT_TILE
    n_blocks = n_buf // tm
    de2 = w1.shape[3]
    grid_spec = pltpu.PrefetchScalarGridSpec(
        num_scalar_prefetch=2,
        grid=(n_blocks,),
        in_specs=[pl.BlockSpec((tm, dh), lambda i, be, nu: (i, 0)),
                  pl.BlockSpec((1, 1, d, de2), lambda i, be, nu: (layer, be[i], 0, 0)),
                  pl.BlockSpec((1, 1, 1, de2), lambda i, be, nu: (layer, be[i], 0, 0)),
                  pl.BlockSpec((1, 1, D_EXPERT, d), lambda i, be, nu: (layer, be[i], 0, 0)),
                  pl.BlockSpec((1, 1, 1, d), lambda i, be, nu: (layer, be[i], 0, 0))],
        out_specs=pl.BlockSpec((tm, dh), lambda i, be, nu: (i, 0)),
        scratch_shapes=[pltpu.VMEM((d, de2), BF16), pltpu.VMEM((D_EXPERT, d), BF16)],
    )
    return pl.pallas_call(
        _moe_kernel,
        grid_spec=grid_spec,
        out_shape=jax.ShapeDtypeStruct((n_buf, dh), jnp.int32),
        compiler_params=_cparams("arbitrary"),
        name="experts",
    )(block_e, block_rows, xs, w1, b1, w2, b2)


def _router_kernel(lg_ref, idx_ref, gate_ref, rank_ref, cnt_ref, base_ref):
    i = pl.program_id(0)
    n_e, t = lg_ref.shape

    @pl.when(i == 0)
    def _():
        base_ref[...] = jnp.zeros_like(base_ref)

    work = lg_ref[...]
    eid = lax.broadcasted_iota(jnp.int32, (n_e, t), 0)
    vals, idxs, sels = [], [], []
    for _ in range(TOP_K):
        m = jnp.max(work, axis=0, keepdims=True)
        idx = jnp.min(jnp.where(work == m, eid, n_e), axis=0, keepdims=True)
        sel = eid == idx
        work = jnp.where(sel, -jnp.inf, work)
        vals.append(m)
        idxs.append(idx)
        sels.append(sel)
    cnt = jnp.where(sels[0], 1.0, 0.0)
    for sel in sels[1:]:
        cnt = cnt + jnp.where(sel, 1.0, 0.0)
    earlier = (lax.broadcasted_iota(jnp.int32, (t, t), 0) < lax.broadcasted_iota(jnp.int32, (t, t), 1))
    within = jnp.dot(cnt.astype(BF16), jnp.where(earlier, 1.0, 0.0).astype(BF16), preferred_element_type=F32)
    pref = base_ref[:, 0:1] + within
    ranks = [jnp.sum(jnp.where(sel, pref, 0.0), axis=0, keepdims=True) for sel in sels]
    new_base = base_ref[...] + jnp.sum(cnt, axis=1, keepdims=True)
    base_ref[...] = new_base
    cnt_ref[...] = new_base
    ex = [jnp.exp(v - vals[0]) for v in vals]
    den = ex[0]
    for e in ex[1:]:
        den = den + e
    idx_ref[...] = jnp.concatenate(idxs, axis=0)
    gate_ref[...] = jnp.concatenate([e / den for e in ex], axis=0)
    rank_ref[...] = jnp.concatenate(ranks, axis=0).astype(jnp.int32)


def _route(logits_t):
    n_e, n_tok = logits_t.shape
    tm = EXPERT_TILE
    t = ROW_TILE
    tokblk = lambda: pl.BlockSpec((TOP_K, t), lambda i: (0, i))
    idx_t, gates_t, rank_t, counts_f = pl.pallas_call(
        _router_kernel,
        grid=(n_tok // t,),
        in_specs=[pl.BlockSpec((n_e, t), lambda i: (0, i))],
        out_specs=[tokblk(), tokblk(), tokblk(), pl.BlockSpec((n_e, LANE), lambda i: (0, 0))],
        out_shape=[jax.ShapeDtypeStruct((TOP_K, n_tok), jnp.int32), jax.ShapeDtypeStruct((TOP_K, n_tok), F32),
                   jax.ShapeDtypeStruct((TOP_K, n_tok), jnp.int32), jax.ShapeDtypeStruct((n_e, LANE), F32)],
        scratch_shapes=[pltpu.VMEM((n_e, LANE), F32)],
        compiler_params=_cparams("arbitrary"),
        name="router",
    )(logits_t)
    n_slots = n_tok * TOP_K
    counts = counts_f[:, 0].astype(jnp.int32)
    padded = (counts + tm - 1) // tm * tm
    padded_end = jnp.cumsum(padded)
    padded_start = padded_end - padded
    experts = jnp.arange(n_e, dtype=jnp.int32)
    dest_t = rank_t + jnp.sum(jnp.where(idx_t[None] == experts[:, None, None], padded_start[:, None, None], 0), axis=0)
    n_blocks = n_slots // tm + n_e
    block_start = jnp.arange(n_blocks, dtype=jnp.int32) * tm
    block_e = jnp.minimum(jnp.sum(padded_end[None, :] <= block_start[:, None], axis=1), n_e - 1).astype(jnp.int32)
    sel = (block_e[:, None] == experts[None, :])
    pick = lambda v: jnp.sum(jnp.where(sel, v[None, :], 0), axis=1)
    block_rows = jnp.clip(pick(padded_start) + pick(counts) - block_start, 0, tm).astype(jnp.int32)
    return gates_t.T, dest_t, block_e, block_rows


SC_WINDOW = 32
SC_SLOTS = 4


def _sc_gather(table, idx):
    _, d = table.shape
    b = idx.shape[0]
    info = plsc.get_sparse_core_info()
    nc, ns = info.num_cores, info.num_subcores
    nw = nc * ns
    w, nslot = SC_WINDOW, SC_SLOTS
    per_w = b // nw
    n_win = per_w // w
    assert per_w * nw == b and n_win * w == per_w and n_win % nslot == 0
    mesh = plsc.VectorSubcoreMesh(core_axis_name="c", subcore_axis_name="s")

    def body(table_hbm, idx_hbm, out_hbm, idx_v, rows_v, gsem, osem):
        wid = lax.axis_index("s") * nc + lax.axis_index("c")
        base = wid * per_w
        pltpu.sync_copy(idx_hbm.at[wid], idx_v)

        def gather(g, slot):
            return pltpu.make_async_copy(table_hbm.at[idx_v.at[g]], rows_v.at[slot], gsem.at[slot])

        def put(g, slot):
            return pltpu.make_async_copy(rows_v.at[slot], out_hbm.at[pl.ds(base + g * w, w)], osem.at[slot])

        for g in range(nslot - 1):
            gather(g, g).start()

        @pl.loop(0, n_win, step=nslot)
        def _(g0):
            for slot in range(nslot):
                g = g0 + slot
                gather(g, slot).wait()
                put(g, slot).start()
                free = (slot - 1) % nslot

                @pl.when(g + nslot - 1 < n_win)
                def _():
                    @pl.when(g >= 1)
                    def _():
                        put(g - 1, free).wait()

                    gather(g + nslot - 1, free).start()

        for g in range(n_win - nslot, n_win):
            put(g, g % nslot).wait()

    return pl.kernel(
        body,
        out_type=jax.ShapeDtypeStruct((b, d), table.dtype),
        mesh=mesh,
        scratch_types=[pltpu.VMEM((n_win, w), jnp.int32), pltpu.VMEM((nslot, w, d), table.dtype),
                       pltpu.SemaphoreType.DMA((nslot,)), pltpu.SemaphoreType.DMA((nslot,))],
        name="sc_gather",
    )(table, idx.reshape(nw, n_win, w))


SC_SCATTER_WINDOW = 16


def _sc_scatter(rows, dest_t, n_out):
    n_tok, d = rows.shape
    k = dest_t.shape[0]
    info = plsc.get_sparse_core_info()
    nc, ns = info.num_cores, info.num_subcores
    nw = nc * ns
    w = SC_SCATTER_WINDOW
    per_w = n_tok // nw
    n_win = per_w // w
    assert per_w * nw == n_tok and n_win * w == per_w and n_win % 2 == 0
    mesh = plsc.VectorSubcoreMesh(core_axis_name="c", subcore_axis_name="s")
    idx = dest_t.reshape(k, nw, n_win, w).transpose(1, 0, 2, 3)

    def body(rows_hbm, idx_hbm, out_hbm, idx_v, rows_v, lsem, ssem):
        wid = lax.axis_index("s") * nc + lax.axis_index("c")
        base = wid * per_w
        pltpu.sync_copy(idx_hbm.at[wid], idx_v)

        def load(g, slot):
            return pltpu.make_async_copy(rows_hbm.at[pl.ds(base + g * w, w)], rows_v.at[slot], lsem.at[slot])

        def scatter(g, slot, j):
            return pltpu.make_async_copy(rows_v.at[slot], out_hbm.at[idx_v.at[j, g]], ssem.at[slot])

        load(0, 0).start()

        @pl.loop(0, n_win, step=2)
        def _(g0):
            for slot in range(2):
                g = g0 + slot
                load(g, slot).wait()
                for j in range(k):
                    scatter(g, slot, j).start()

                @pl.when(g + 1 < n_win)
                def _():
                    @pl.when(g >= 1)
                    def _():
                        for j in range(k):
                            scatter(g - 1, 1 - slot, j).wait()

                    load(g + 1, 1 - slot).start()

        for g in (n_win - 2, n_win - 1):
            for j in range(k):
                scatter(g, g % 2, j).wait()

    return pl.kernel(
        body,
        out_type=jax.ShapeDtypeStruct((n_out, d), rows.dtype),
        mesh=mesh,
        scratch_types=[pltpu.VMEM((k, n_win, w), jnp.int32), pltpu.VMEM((2, w, d), rows.dtype),
                       pltpu.SemaphoreType.DMA((2,)), pltpu.SemaphoreType.DMA((2,))],
        name="sc_scatter",
    )(rows, idx)


def _combine_kernel(x_ref, y_ref, gt_ref, g2_ref, o_ref):
    gt = gt_ref[...]
    acc = gt[:, 0:1] * _unpack_rows(y_ref[0]).astype(F32)
    for j in range(1, TOP_K):
        acc = acc + gt[:, j:j + 1] * _unpack_rows(y_ref[j]).astype(F32)
    o_ref[0] = x_ref[0] + g2_ref[0] * acc


def _combine(x, ysel, gates, tok_offset, seg_of_batch, mod):
    b, s, d = x.shape
    t = min(ROW_TILE, s)
    per = s // t
    off = tok_offset // t
    return pl.pallas_call(
        _combine_kernel,
        grid=(b, per),
        in_specs=[pl.BlockSpec((1, t, d), lambda bi, i: (bi, i, 0)),
                  pl.BlockSpec((TOP_K, t, d // 2), lambda bi, i: (0, off + bi * per + i, 0)),
                  pl.BlockSpec((t, TOP_K), lambda bi, i: (off + bi * per + i, 0)),
                  pl.BlockSpec((1, 1, d), lambda bi, i: (seg_of_batch(bi), 0, 0))],
        out_specs=pl.BlockSpec((1, t, d), lambda bi, i: (bi, i, 0)),
        out_shape=jax.ShapeDtypeStruct((b, s, d), F32),
        compiler_params=_cparams("parallel", "parallel"),
        name="combine",
    )(x, ysel, gates, mod["g2"])


def _rope_tables(seq_len):
    n_rows = seq_len // GRID_W
    row = jnp.repeat(jnp.arange(n_rows, dtype=F32), GRID_W)
    col = jnp.tile(jnp.arange(GRID_W, dtype=F32), n_rows)
    pairs = ROPE // 4
    inv = ROPE_BASE ** (-jnp.arange(pairs, dtype=F32) / pairs)
    ang = jnp.concatenate([row[:, None] * inv, col[:, None] * inv], axis=-1)
    cos, sin = jnp.cos(ang), jnp.sin(ang)
    one = jnp.ones((seq_len, NOPE), F32)
    zn = jnp.zeros((seq_len, NOPE), F32)
    zp = jnp.zeros((seq_len, HEAD_PAD - QK_DIM), F32)
    return jnp.concatenate([one, cos, cos, zp], axis=-1), jnp.concatenate([zn, sin, sin, zp], axis=-1)


def _identity_rope(seq_len):
    keep = jnp.concatenate([jnp.ones((seq_len, QK_DIM), F32), jnp.zeros((seq_len, HEAD_PAD - QK_DIM), F32)], axis=-1)
    return keep, jnp.zeros((seq_len, HEAD_PAD), F32)


def _rotate_half(w):
    half = ROPE // 2
    x1, x2 = w[..., NOPE:NOPE + half], w[..., NOPE + half:QK_DIM]
    return jnp.concatenate([jnp.zeros_like(w[..., :NOPE]), -x2, x1, jnp.zeros_like(w[..., QK_DIM:])], axis=-1)


def _swap_rope_halves(g):
    half = ROPE // 2
    lo, hi = g[..., NOPE:NOPE + half], g[..., NOPE + half:QK_DIM]
    return jnp.concatenate([g[..., :NOPE], hi, lo, g[..., QK_DIM:]], axis=-1)


def _pad_cols(w, n):
    return jnp.pad(w, ((0, 0), (0, n - w.shape[1])))


def _layer_weights(l, p):
    hw = HEADS * HEAD_PAD
    wq = jnp.pad(p["mla_w_uq"][l].reshape(Q_RANK, HEADS, QK_DIM), ((0, 0), (0, 0), (0, HEAD_PAD - QK_DIM)))
    wukv = p["mla_w_ukv"][l].reshape(KV_RANK, HEADS, NOPE + V_DIM)
    wk = jnp.pad(wukv[:, :, :NOPE], ((0, 0), (0, 0), (0, HEAD_PAD - NOPE)))
    wv = jnp.pad(wukv[:, :, NOPE:], ((0, 0), (0, 0), (0, HEAD_PAD - V_DIM))).reshape(KV_RANK, hw)
    place = jnp.pad(jnp.eye(ROPE, dtype=F32), ((0, KV_RANK - ROPE), (NOPE, HEAD_PAD - QK_DIM)))
    wk = jnp.concatenate([wk, jnp.tile(place[:, None, :], (1, HEADS, 1))], axis=0)
    with_rot = lambda w: jnp.concatenate([w.reshape(-1, hw), _rotate_half(w).reshape(-1, hw)], axis=1).astype(BF16)
    wpool = jax.scipy.linalg.block_diag(*[p["pool_w"][l, g] for g in range(POOL_GROUPS)])
    vone = (jnp.arange(HEAD_PAD) == V_DIM).astype(F32)[:, None]
    deltas = jnp.abs(jnp.linspace(HY_MIN_DECAY, HY_MAX_DECAY, HY_WIDTH, dtype=F32))[None, :]
    row = lambda v, n=None: (v if n is None else jnp.pad(v, (0, n - v.shape[0])))[None, :]
    twice = lambda v: jnp.concatenate([v, v])[None, :]
    return {
        "norm1_g": row(p["norm1_g"][l]), "norm2_g": row(p["norm2_g"][l]),
        "w_in": _pad_cols(p["w_in"][l], N_IN_PAD).astype(BF16),
        "gq": row(p["mla_q_norm_g"][l]), "wq": with_rot(wq),
        "gkv": row(p["mla_kv_norm_g"][l]), "wk": with_rot(wk), "wvt": wv.T.astype(BF16),
        "gqq": row(p["qk_norm_q"][l], HEAD_PAD), "gqq_rot": _swap_rope_halves(row(p["qk_norm_q"][l], HEAD_PAD)),
        "gqk": row(p["qk_norm_k"][l], HEAD_PAD), "gqk_rot": _swap_rope_halves(row(p["qk_norm_k"][l], HEAD_PAD)),
        "vone": vone,
        "wpool": wpool.astype(BF16), "pool_scale": row(p["pool_scale"][l]),
        "conv_w": p["hy_conv_w"][l], "conv_b": row(p["hy_conv_b"][l]),
        "f_w1a": jnp.pad(p["hy_f_w1"][l], ((0, LANE - HY_EMB), (0, LANE - HY_HID))),
        "f_w1b": jnp.pad(p["hy_f_w1"][l], ((0, LANE - HY_EMB), (LANE - HY_HID, 0))),
        "f_b1": twice(p["hy_f_b1"][l]),
        "f_w2": jax.scipy.linalg.block_diag(p["hy_f_w2"][l], p["hy_f_w2"][l]),
        "f_b2": twice(p["hy_f_b2"][l]),
        "f_w3a": jnp.pad(p["hy_f_w3"][l][:, :HY_WIDTH], ((0, LANE - HY_HID), (0, 0))),
        "f_w3b": jnp.pad(p["hy_f_w3"][l][:, HY_WIDTH:], ((LANE - HY_HID, 0), (0, 0))),
        "f_freq": twice(p["hy_freq"][l]), "f_delta": deltas,
        "hy_bias": row(p["hy_bias"][l]),
        "w_out": p["w_out"][l].astype(BF16),
        "router_w": _split_const(p["router_w"][l].T), "router_b": p["router_b"][l][:, None],
    }


def _layer(l, x, xc, cc, wts, p, rope_main, rope_ctx, last):
    b, s, d = x.shape
    n_ctx = xc.shape[1]
    modm = _modulation(cc, p["w_mod"], p["b_mod"][l][None, :], l)
    names = ("sh1", "sc1", "g1", "sh2", "sc2", "g2")
    mod = {nm: modm[:, i * d:(i + 1) * d].reshape(N_SEG, 1, d) for i, nm in enumerate(names)}
    main_seg = lambda bi: bi
    ctx_seg = lambda bi: CTX_SEG

    pph, q, kl, vl = _inproj(x, main_seg, wts, mod, rope_main)
    pph_c, qc, kc, vc = _inproj(xc, ctx_seg, wts, mod, rope_ctx)
    attn = _attention(q, kc, vc, kl, vl)
    pool, x0, z = _seqmix(pph, wts)
    hy = _hyena_long(z, x0, wts["hy_bias"], _hyena_filter(s, wts))
    n_c = 0 if last else b * n_ctx
    n_tok = n_c + b * s
    init = None if last else (jnp.zeros((n_tok, d // 2), jnp.int32), jnp.zeros((N_EXPERTS, n_tok), F32))
    x1, h2, logits = _outproj(x, pool, hy, attn, main_seg, wts, mod, n_tok, n_c, prior=init)
    if not last:
        attn_c = _attention(qc, kc, vc)
        pool_c, x0c, zc = _seqmix(pph_c, wts)
        hy_c = _hyena_short(zc, x0c, wts["hy_bias"], _hyena_filter(n_ctx, wts))
        xc1, h2, logits = _outproj(xc, pool_c, hy_c, attn_c, ctx_seg, wts, mod, n_tok, 0, prior=(h2, logits))

    gates, dest_t, block_e, block_rows = _route(logits)
    de2 = p["moe_w1"].shape[-1]
    xs = _sc_scatter(h2, dest_t, block_e.shape[0] * EXPERT_TILE)
    ys = _experts(xs, block_e, block_rows, p["moe_w1"], p["moe_b1"].reshape(-1, N_EXPERTS, 1, de2),
                  p["moe_w2"], p["moe_b2"].reshape(-1, N_EXPERTS, 1, d), l)
    ysel = _sc_gather(ys, dest_t.reshape(-1)).reshape(TOP_K, -1, d // 2)
    x2 = _combine(x1, ysel, gates, n_c, main_seg, mod)
    if not last:
        xc = _combine(xc1, ysel, gates, 0, ctx_seg, mod)
    return x2, xc


def kernel(x, c, ctx, c_ctx, norm1_g, norm2_g, w_mod, b_mod, w_in, pool_w, pool_scale, hy_conv_w, hy_conv_b,
           hy_f_w1, hy_f_b1, hy_f_w2, hy_f_b2, hy_f_w3, hy_freq, hy_bias, mla_q_norm_g, mla_w_uq, mla_kv_norm_g,
           mla_w_ukv, qk_norm_q, qk_norm_k, w_out, router_w, router_b, moe_w1, moe_b1, moe_w2, moe_b2):
    p = dict(norm1_g=norm1_g, norm2_g=norm2_g, w_mod=w_mod, b_mod=b_mod, w_in=w_in, pool_w=pool_w,
             pool_scale=pool_scale,
             hy_conv_w=hy_conv_w, hy_conv_b=hy_conv_b, hy_f_w1=hy_f_w1, hy_f_b1=hy_f_b1, hy_f_w2=hy_f_w2,
             hy_f_b2=hy_f_b2, hy_f_w3=hy_f_w3, hy_freq=hy_freq, hy_bias=hy_bias, mla_q_norm_g=mla_q_norm_g,
             mla_w_uq=mla_w_uq, mla_kv_norm_g=mla_kv_norm_g, mla_w_ukv=mla_w_ukv, qk_norm_q=qk_norm_q,
             qk_norm_k=qk_norm_k, w_out=w_out, router_w=router_w, router_b=router_b, moe_w1=moe_w1,
             moe_b1=moe_b1, moe_w2=moe_w2, moe_b2=moe_b2)
    b, s, d = x.shape
    depth = w_mod.shape[0]
    assert b <= CTX_SEG
    cc = jnp.concatenate([c, jnp.zeros((CTX_SEG - b, d), F32), c_ctx[None, :],
                          jnp.zeros((N_SEG - CTX_SEG - 1, d), F32)], axis=0)
    rope_main = _rope_tables(s)
    rope_ctx = _identity_rope(ctx.shape[1])
    xc = ctx
    for l in range(depth):
        wts = _layer_weights(l, p)
        x, xc = _layer(l, x, xc, cc, wts, p, rope_main, rope_ctx, l == depth - 1)
    return x
```

```python
import functools
import math

import jax
import jax.numpy as jnp
import numpy as np
from jax import lax
from jax.experimental import pallas as pl
from jax.experimental.pallas import tpu as pltpu
from jax.experimental.pallas import tpu_sc as plsc

F32 = jnp.float32
BF16 = jnp.bfloat16
HIGHEST = lax.Precision.HIGHEST

D_MODEL = 1024
GRID_W = 64
EPS = 1e-6
POOL_WIDTH = 256
POOL_GROUPS = 4
POOL_GROUP_DIM = POOL_WIDTH // POOL_GROUPS
HY_WIDTH = 256
HY_IN = 3 * HY_WIDTH
HY_BANDS = 8
HY_EMB = 1 + 2 * HY_BANDS
HY_HID = 64
HY_MIN_DECAY = math.log(1e-2) / 1.5
HY_MAX_DECAY = math.log(1e-2) / 0.3
HEADS = 8
Q_RANK = 256
KV_RANK = 128
NOPE = 64
ROPE = 32
V_DIM = 64
QK_DIM = NOPE + ROPE
MLA_WIDTH = HEADS * V_DIM
ATTN_SCALE = QK_DIM ** -0.5
LOG2E = math.log2(math.e)
OFF_HY = POOL_WIDTH
OFF_Q = OFF_HY + HY_IN
OFF_KV = OFF_Q + Q_RANK
OFF_KR = OFF_KV + KV_RANK
N_IN = OFF_KR + ROPE
N_IN_PAD = OFF_KR + 128
ROPE_BASE = 10000.0
N_EXPERTS = 32
TOP_K = 4
D_EXPERT = 1024
SWIGLU_LIMIT = 7.0
SWIGLU_ALPHA = 1.702

LANE = 128
HEAD_PAD = LANE
ROW_TILE = 512
EXPERT_TILE = 512
ATTN_KEY_CHUNK = 256
ATTN_CHUNKS_PER_TRIP = 16
MOD_COL_TILE = 1536
SUBLANE = 8
DFT_BLOCKS_PER_STEP = 16
FILTER_ROWS = 1024
VMEM_LIMIT = 56 * 1024 * 1024
N_SEG = 8
CTX_SEG = 4


def _cparams(*sem):
    return pltpu.CompilerParams(dimension_semantics=sem, vmem_limit_bytes=VMEM_LIMIT)


def _rms(x, g, n):
    ss = jnp.sum(x * x, axis=-1, keepdims=True) * (1.0 / n)
    return x * lax.rsqrt(ss + EPS) * g


def _pack_rows(x):
    half = x.shape[1] // 2
    bits = lax.bitcast_convert_type(x.astype(BF16).astype(F32), jnp.int32)
    return bits[:, :half] | lax.shift_right_logical(bits[:, half:], 16)


def _unpack_rows(w):
    hi = lax.bitcast_convert_type(w & jnp.int32(-65536), F32)
    lo = lax.bitcast_convert_type(lax.shift_left(w, 16), F32)
    return jnp.concatenate([hi, lo], axis=1).astype(BF16)


def _mod_kernel(c_ref, w_ref, b_ref, o_ref):
    c = c_ref[...]
    s = c * (1.0 / (1.0 + jnp.exp(-c)))
    o_ref[...] = jnp.dot(s, w_ref[0], precision=HIGHEST, preferred_element_type=F32) + b_ref[...]


def _modulation(cc, w, b, layer):
    _, d, n = w.shape
    tn = MOD_COL_TILE
    return pl.pallas_call(
        _mod_kernel,
        grid=(n // tn,),
        in_specs=[pl.BlockSpec((N_SEG, d), lambda j: (0, 0)),
                  pl.BlockSpec((1, d, tn), lambda j: (layer, 0, j)),
                  pl.BlockSpec((1, tn), lambda j: (0, j))],
        out_specs=pl.BlockSpec((N_SEG, tn), lambda j: (0, j)),
        out_shape=jax.ShapeDtypeStruct((N_SEG, n), F32),
        compiler_params=_cparams("arbitrary"),
        name="modulation",
    )(cc, w, b)


def _inproj_kernel(x_ref, gn_ref, sc_ref, sh_ref, win_ref, gq_ref, wq_ref, gkv_ref, wk_ref, wvt_ref,
                   gqq_ref, gqqr_ref, gqk_ref, gqkr_ref, vone_ref, rc_ref, rs_ref,
                   pph_ref, q_ref, k_ref, vt_ref):
    x = x_ref[0]
    h = _rms(x, gn_ref[...], D_MODEL) * (1.0 + sc_ref[0]) + sh_ref[0]
    p = jnp.dot(h.astype(BF16), win_ref[...], preferred_element_type=F32)
    pph_ref[0] = p[:, :OFF_Q]
    cq = _rms(p[:, OFF_Q:OFF_KV], gq_ref[...], Q_RANK)
    qall = jnp.dot(cq.astype(BF16), wq_ref[...], preferred_element_type=F32)
    ckv_b = _rms(p[:, OFF_KV:OFF_KR], gkv_ref[...], KV_RANK).astype(BF16)
    kin = jnp.concatenate([ckv_b, p[:, OFF_KR:N_IN_PAD].astype(BF16)], axis=-1)
    kall = jnp.dot(kin, wk_ref[...], preferred_element_type=F32)
    vt_all = lax.dot_general(wvt_ref[...], ckv_b, (((1,), (1,)), ((), ())), preferred_element_type=F32)
    rc, rs = rc_ref[...], rs_ref[...]
    hw = HEADS * HEAD_PAD

    def norm_rope(x_all, hd, g_ref, gr_ref):
        xh = x_all[:, HEAD_PAD * hd:HEAD_PAD * (hd + 1)]
        xr = x_all[:, hw + HEAD_PAD * hd:hw + HEAD_PAD * (hd + 1)]
        r = lax.rsqrt(jnp.sum(xh * xh, axis=-1, keepdims=True) * (1.0 / QK_DIM) + EPS)
        return (xh * (g_ref[...] * rc) + xr * (gr_ref[...] * rs)) * r

    for hd in range(HEADS):
        q_ref[0, hd] = (norm_rope(qall, hd, gqq_ref, gqqr_ref) * (ATTN_SCALE * LOG2E)).astype(BF16)
        k_ref[0, hd] = norm_rope(kall, hd, gqk_ref, gqkr_ref).astype(BF16)
        vt_ref[0, hd] = (vt_all[HEAD_PAD * hd:HEAD_PAD * (hd + 1)] + vone_ref[...]).astype(BF16)


def _inproj(x, seg_of_batch, wts, mod, rope_tabs):
    b, s, d = x.shape
    t = min(ROW_TILE, s)
    full = lambda shape: pl.BlockSpec(shape, lambda bi, i: (0,) * len(shape))
    seg = lambda: pl.BlockSpec((1, 1, d), lambda bi, i: (seg_of_batch(bi), 0, 0))
    tab = lambda: pl.BlockSpec((t, HEAD_PAD), lambda bi, i: (i, 0))
    hw = HEADS * HEAD_PAD
    head_out = lambda: pl.BlockSpec((1, HEADS, t, HEAD_PAD), lambda bi, i: (bi, 0, i, 0))
    return pl.pallas_call(
        _inproj_kernel,
        grid=(b, s // t),
        in_specs=[pl.BlockSpec((1, t, d), lambda bi, i: (bi, i, 0)),
                  full((1, d)), seg(), seg(),
                  full((d, N_IN_PAD)), full((1, Q_RANK)), full((Q_RANK, 2 * hw)),
                  full((1, KV_RANK)), full((2 * KV_RANK, 2 * hw)), full((hw, KV_RANK)),
                  full((1, HEAD_PAD)), full((1, HEAD_PAD)), full((1, HEAD_PAD)), full((1, HEAD_PAD)),
                  full((HEAD_PAD, 1)), tab(), tab()],
        out_specs=[pl.BlockSpec((1, t, OFF_Q), lambda bi, i: (bi, i, 0)),
                   head_out(), head_out(),
                   pl.BlockSpec((1, HEADS, HEAD_PAD, t), lambda bi, i: (bi, 0, 0, i))],
        out_shape=[jax.ShapeDtypeStruct((b, s, OFF_Q), F32),
                   jax.ShapeDtypeStruct((b, HEADS, s, HEAD_PAD), BF16),
                   jax.ShapeDtypeStruct((b, HEADS, s, HEAD_PAD), BF16),
                   jax.ShapeDtypeStruct((b, HEADS, HEAD_PAD, s), BF16)],
        compiler_params=_cparams("parallel", "parallel"),
        name="inproj",
    )(x, wts["norm1_g"], mod["sc1"], mod["sh1"], wts["w_in"], wts["gq"], wts["wq"], wts["gkv"], wts["wk"],
      wts["wvt"], wts["gqq"], wts["gqq_rot"], wts["gqk"], wts["gqk_rot"], wts["vone"], *rope_tabs)


def _seqmix_kernel(prev_ref, cur_ref, next_ref, wpool_ref, pscale_ref, cw_ref, cb_ref,
                   pool_ref, x0_ref, z_ref, *, seq_len):
    i = pl.program_id(1)
    n = pl.num_programs(1)
    t_rows = cur_ref.shape[1]
    halo = prev_ref.shape[1]
    r_rows = t_rows + 2 * halo
    prev = jnp.where(i > 0, prev_ref[0], 0.0)
    nxt = jnp.where(i < n - 1, next_ref[0], 0.0)
    ext = jnp.concatenate([prev, cur_ref[0], nxt], axis=0)

    def sh(a, dlt):
        return pltpu.roll(a, (-dlt) % r_rows, axis=0)

    u = ext[:, :POOL_WIDTH]
    a2 = u + sh(u, -1)
    a4 = sh(a2, -1) + sh(a2, 1)
    a8 = sh(a4, -2) + sh(a4, 2)
    a16 = sh(a8, -4) + sh(a8, 4)
    grp = lax.broadcasted_iota(jnp.int32, (1, POOL_WIDTH), 1) // POOL_GROUP_DIM
    win = jnp.where(grp == 0, a2, jnp.where(grp == 1, a4, jnp.where(grp == 2, a8, a16)))
    half = jnp.where(grp == 0, 1, jnp.where(grp == 1, 2, jnp.where(grp == 2, 4, 8)))
    pos = i * t_rows - halo + lax.broadcasted_iota(jnp.int32, (r_rows, 1), 0)
    cnt = jnp.minimum(pos + half, seq_len) - jnp.maximum(pos - half, 0)
    cnt = jnp.maximum(cnt, 1).astype(F32)
    dev = (win / cnt - u)[halo:halo + t_rows]
    pool = jnp.dot(dev.astype(BF16), wpool_ref[...], preferred_element_type=F32) * pscale_ref[...]
    pool_ref[0] = pool.astype(BF16)

    e = ext[:, OFF_HY:OFF_Q]
    cw = cw_ref[...]
    uc = sh(e, -1) * cw[0:1] + e * cw[1:2] + sh(e, 1) * cw[2:3] + cb_ref[...]
    uc = uc[halo:halo + t_rows]
    x0_ref[0] = uc[:, :HY_WIDTH]
    z_ref[0] = uc[:, HY_WIDTH:2 * HY_WIDTH] * uc[:, 2 * HY_WIDTH:]


def _seqmix(pph, wts):
    b, s, w = pph.shape
    t = min(ROW_TILE, s)
    halo = SUBLANE
    per = t // halo
    nh = s // halo
    full = lambda shape: pl.BlockSpec(shape, lambda bi, i: (0,) * len(shape))
    out = lambda: pl.BlockSpec((1, t, HY_WIDTH), lambda bi, i: (bi, i, 0))
    return pl.pallas_call(
        functools.partial(_seqmix_kernel, seq_len=s),
        grid=(b, s // t),
        in_specs=[pl.BlockSpec((1, halo, w), lambda bi, i: (bi, jnp.maximum(i * per - 1, 0), 0)),
                  pl.BlockSpec((1, t, w), lambda bi, i: (bi, i, 0)),
                  pl.BlockSpec((1, halo, w), lambda bi, i: (bi, jnp.minimum((i + 1) * per, nh - 1), 0)),
                  full((POOL_WIDTH, POOL_WIDTH)), full((1, POOL_WIDTH)), full((3, HY_IN)), full((1, HY_IN))],
        out_specs=[out(), out(), out()],
        out_shape=[jax.ShapeDtypeStruct((b, s, POOL_WIDTH), BF16),
                   jax.ShapeDtypeStruct((b, s, HY_WIDTH), F32),
                   jax.ShapeDtypeStruct((b, s, HY_WIDTH), F32)],
        compiler_params=_cparams("parallel", "parallel"),
        name="seqmix",
    )(pph, pph, pph, wts["wpool"], wts["pool_scale"], wts["conv_w"], wts["conv_b"])


def _dot3(m_ref, x):
    x_hi = x.astype(BF16)
    x_lo = (x - x_hi.astype(F32)).astype(BF16)
    m_hi, m_lo = m_ref[0], m_ref[1]
    return (jnp.dot(m_hi, x_hi, preferred_element_type=F32) + jnp.dot(m_hi, x_lo, preferred_element_type=F32)
            + jnp.dot(m_lo, x_hi, preferred_element_type=F32))


def _split_const(m):
    hi = m.astype(BF16)
    return jnp.stack([hi, (m - hi.astype(F32)).astype(BF16)])


def _leftmm_kernel(*refs, cmul, gate):
    m_ref, x_ref = refs[0], refs[1]
    nxt = 2
    x = x_ref[0]
    if cmul:
        ks = refs[nxt][...]
        nxt += 1
        kh = x.shape[0] // 2
        xr, xi, kr, ki = x[:kh], x[kh:], ks[:kh], ks[kh:]
        x = jnp.concatenate([xr * kr - xi * ki, xr * ki + xi * kr], axis=0)
    y = jnp.dot(m_ref[...], x, precision=HIGHEST, preferred_element_type=F32)
    if gate:
        z_ref, bias_ref, x0_ref = refs[nxt:nxt + 3]
        nxt += 3
        y = (y + z_ref[0] * bias_ref[...]) * x0_ref[0]
    o_ref = refs[nxt]
    o_ref[0] = y.astype(o_ref.dtype)


def _leftmm(m, x, spectrum=None, gate=None, out_dtype=F32):
    g, k, cols = x.shape
    mo = m.shape[0]
    tc = min(cols, 2048)
    args = [m, x]
    in_specs = [pl.BlockSpec((mo, k), lambda gi, j: (0, 0)), pl.BlockSpec((1, k, tc), lambda gi, j: (gi, 0, j))]
    if spectrum is not None:
        args.append(spectrum)
        in_specs.append(pl.BlockSpec((k, tc), lambda gi, j: (0, j)))
    if gate is not None:
        z, bias, x0 = gate
        args += [z, bias, x0]
        in_specs += [pl.BlockSpec((1, mo, tc), lambda gi, j: (gi, 0, j)),
                     pl.BlockSpec((1, tc), lambda gi, j: (0, j)),
                     pl.BlockSpec((1, mo, tc), lambda gi, j: (gi, 0, j))]
    return pl.pallas_call(
        functools.partial(_leftmm_kernel, cmul=spectrum is not None, gate=gate is not None),
        grid=(g, cols // tc),
        in_specs=in_specs,
        out_specs=pl.BlockSpec((1, mo, tc), lambda gi, j: (gi, 0, j)),
        out_shape=jax.ShapeDtypeStruct((g, mo, cols), out_dtype),
        compiler_params=_cparams("parallel", "parallel"),
        name="dft_leftmm",
    )(*args)


def _blockfft_kernel(*refs, n2, blocks, conv):
    if conv:
        a_ref, tw_ref, ks_ref, m2_ref, m2c_ref, o_ref = refs
    else:
        a_ref, tw_ref, m2_ref, o_ref = refs

    def body(j, carry):
        rows = pl.ds(j * n2, n2)
        ar, ai = a_ref[0, 0, rows, :], a_ref[0, 1, rows, :]
        tcs, tsn = tw_ref[0, rows, :], tw_ref[1, rows, :]
        tcs = jnp.concatenate([tcs] * (ar.shape[1] // tcs.shape[1]), axis=1)
        tsn = jnp.concatenate([tsn] * (ar.shape[1] // tsn.shape[1]), axis=1)
        bstk = jnp.concatenate([ar * tcs + ai * tsn, ai * tcs - ar * tsn], axis=0)
        xs = _dot3(m2_ref, bstk)
        if not conv:
            o_ref[0, 0, rows, :] = xs[:n2]
            o_ref[0, 1, rows, :] = xs[n2:]
            return carry
        xr, xi = xs[:n2], xs[n2:]
        kr, ki = ks_ref[0, rows, :], ks_ref[1, rows, :]
        ystk = jnp.concatenate([xr * kr - xi * ki, xr * ki + xi * kr], axis=0)
        cs = _dot3(m2c_ref, ystk)
        cr, ci = cs[:n2], cs[n2:]
        o_ref[0, 0, rows, :] = cr * tcs - ci * tsn
        o_ref[0, 1, rows, :] = cr * tsn + ci * tcs
        return carry

    for j in range(blocks):
        body(j, 0)


def _blockfft(a, tw, m2, spectrum=None, m2c=None):
    g, _, n, c = a.shape
    n2 = m2.shape[1] // 2
    blocks = min(DFT_BLOCKS_PER_STEP, n // n2)
    rows = blocks * n2
    conv = spectrum is not None
    args = [a, tw]
    in_specs = [pl.BlockSpec((1, 2, rows, c), lambda gi, j: (gi, 0, j, 0)),
                pl.BlockSpec((2, rows, LANE), lambda gi, j: (0, j, 0))]
    if conv:
        args.append(spectrum)
        in_specs.append(pl.BlockSpec((2, rows, c), lambda gi, j: (0, j, 0)))
    args.append(m2)
    in_specs.append(pl.BlockSpec(m2.shape, lambda gi, j: (0, 0, 0)))
    if conv:
        args.append(m2c)
        in_specs.append(pl.BlockSpec(m2c.shape, lambda gi, j: (0, 0, 0)))
    return pl.pallas_call(
        functools.partial(_blockfft_kernel, n2=n2, blocks=blocks, conv=conv),
        grid=(g, n // rows),
        in_specs=in_specs,
        out_specs=pl.BlockSpec((1, 2, rows, c), lambda gi, j: (gi, 0, j, 0)),
        out_shape=jax.ShapeDtypeStruct((g, 2, n, c), F32),
        compiler_params=_cparams("parallel", "parallel"),
        name="dft_block",
    )(*args)


def _filter_kernel(za_ref, zb_ref, w1a_ref, w1b_ref, b1_ref, w2_ref, b2_ref, w3a_ref, w3b_ref, fr_ref, dl_ref,
                   o_ref):
    i = pl.program_id(0)
    za, zb = za_ref[...], zb_ref[...]
    fr = fr_ref[...]
    dot = functools.partial(jnp.dot, precision=HIGHEST, preferred_element_type=F32)
    h = jnp.sin(fr * (dot(za, w1a_ref[...]) + dot(zb, w1b_ref[...]) + b1_ref[...]))
    h = jnp.sin(fr * (dot(h, w2_ref[...]) + b2_ref[...]))
    o_ref[0] = dot(h, w3a_ref[...]) * jnp.exp(-za[:, 0:1] * dl_ref[...])
    first = (i == 0) & (lax.broadcasted_iota(jnp.int32, (zb.shape[0], 1), 0) == 0)
    o_ref[1] = jnp.where(first, 0.0, dot(h, w3b_ref[...]) * jnp.exp(-zb[:, 0:1] * dl_ref[...]))


def _hyena_filter(seq_len, wts):
    n = 2 * seq_len
    j = jnp.arange(n)
    lag = jnp.clip(jnp.where(j < seq_len, j, n - j), 0, seq_len - 1).astype(F32)[:, None]
    t = lag / (seq_len - 1)
    wpos = (2.0 * math.pi / seq_len) * lag
    f = jnp.linspace(1e-4, HY_BANDS - 1, HY_BANDS, dtype=F32)[None, :]
    feat = jnp.concatenate([t, jnp.cos(f * wpos), -jnp.sin(f * wpos)], axis=-1)
    feat2 = jnp.pad(feat, ((0, 0), (0, LANE - HY_EMB)))
    rows = min(seq_len, FILTER_ROWS)
    half = seq_len // rows
    full = lambda shape: pl.BlockSpec(shape, lambda i: (0,) * len(shape))
    taps = pl.pallas_call(
        _filter_kernel,
        grid=(half,),
        in_specs=[pl.BlockSpec((rows, LANE), lambda i: (i, 0)), pl.BlockSpec((rows, LANE), lambda i: (half + i, 0)),
                  full((LANE, LANE)), full((LANE, LANE)), full((1, LANE)), full((LANE, LANE)), full((1, LANE)),
                  full((LANE, HY_WIDTH)), full((LANE, HY_WIDTH)), full((1, LANE)), full((1, HY_WIDTH))],
        out_specs=pl.BlockSpec((2, rows, HY_WIDTH), lambda i: (0, i, 0)),
        out_shape=jax.ShapeDtypeStruct((2, seq_len, HY_WIDTH), F32),
        compiler_params=_cparams("parallel"),
        name="hyena_filter",
    )(feat2, feat2, wts["f_w1a"], wts["f_w1b"], wts["f_b1"], wts["f_w2"], wts["f_b2"], wts["f_w3a"], wts["f_w3b"],
      wts["f_freq"], wts["f_delta"])
    return taps.reshape(n, HY_WIDTH)


def _cs(p, rows, cols):
    ang = 2.0 * np.pi * np.outer(np.arange(rows), np.arange(cols)) / p
    return np.cos(ang), np.sin(ang)


def _mat(blocks):
    return jnp.asarray(np.block(blocks), dtype=F32)


def _dftcols_kernel(*refs, gate):
    m_ref, x_ref = refs[0], refs[1]
    x = x_ref[0]
    k, nt, _ = x.shape
    xt = pltpu.einshape("knc->(nk)c", x)
    ys = [_dot3(m_ref, xt[n * k:(n + 1) * k]) for n in range(nt)]
    y = pltpu.einshape("(nk)c->knc", jnp.concatenate(ys, axis=0), n=nt)
    if gate:
        z_ref, bias_ref, x0_ref, o_ref = refs[2:]
        y = (y + z_ref[0] * bias_ref[...]) * x0_ref[0]
    else:
        o_ref = refs[2]
    o_ref[0] = y.astype(o_ref.dtype)


def _dftcols(m, x, gate=None, out_dtype=F32):
    g, k, n2, c = x.shape
    mo = m.shape[1]
    nt = 16 if n2 % 16 == 0 else 8
    blk = lambda rows: pl.BlockSpec((1, rows, nt, c), lambda gi, j: (gi, 0, j, 0))
    args = [m, x]
    in_specs = [pl.BlockSpec((2, mo, k), lambda gi, j: (0, 0, 0)), blk(k)]
    if gate is not None:
        z, bias, x0 = gate
        args += [z, bias, x0]
        in_specs += [blk(mo), pl.BlockSpec((1, c), lambda gi, j: (0, 0)), blk(mo)]
    return pl.pallas_call(
        functools.partial(_dftcols_kernel, gate=gate is not None),
        grid=(g, n2 // nt),
        in_specs=in_specs,
        out_specs=blk(mo),
        out_shape=jax.ShapeDtypeStruct((g, mo, n2, c), out_dtype),
        compiler_params=_cparams("parallel", "parallel"),
        name="dft_cols",
    )(*args)


def _hyena_long(z, x0, bias, filt):
    b, s, c = z.shape
    n = 2 * s
    n1 = 1 << ((n.bit_length() - 1 + 1) // 2)
    n2 = n // n1
    g = b // 2
    c1, s1 = _cs(n1, n1, n1)
    h1 = n1 // 2
    m_fwd = _split_const(_mat([[c1[:, :h1], s1[:, :h1]], [-s1[:, :h1], c1[:, :h1]]]))
    m_filt = _split_const(_mat([[c1], [-s1]]))
    m_inv = _split_const(_mat([[c1[:h1], -s1[:h1]], [s1[:h1], c1[:h1]]]) * (1.0 / n))
    c2, s2 = _cs(n2, n2, n2)
    m2 = _split_const(_mat([[c2, s2], [-s2, c2]]))
    m2c = _split_const(_mat([[c2, -s2], [s2, c2]]))
    k1 = jnp.arange(n1, dtype=jnp.int32)[:, None]
    nn2 = jnp.arange(n2, dtype=jnp.int32)[None, :]
    ang = (2.0 * math.pi / n) * (k1 * nn2).astype(F32).reshape(n)
    tw = jnp.broadcast_to(jnp.stack([jnp.cos(ang), jnp.sin(ang)])[:, :, None], (2, n, LANE))

    spec = _dftcols(m_filt, filt.reshape(1, n1, n2, c)).reshape(1, 2, n, c)
    spec = _blockfft(spec, tw, m2)[0]
    zv = z.reshape(g, n1, n2, c)
    a = _dftcols(m_fwd, zv).reshape(g, 2, n, c)
    dd = _blockfft(a, tw, m2, spectrum=spec, m2c=m2c).reshape(g, 2 * n1, n2, c)
    y = _dftcols(m_inv, dd, gate=(zv, bias, x0.reshape(g, n1, n2, c)), out_dtype=BF16)
    return y.reshape(b, s, c)


def _hyena_short(z, x0, bias, filt):
    b, s, c = z.shape
    n = 2 * s
    g = b // 2
    cf, sf = _cs(n, n, n)
    m_fwd = _mat([[cf[:, :s], sf[:, :s]], [-sf[:, :s], cf[:, :s]]])
    m_filt = _mat([[cf], [-sf]])
    m_inv = _mat([[cf[:s], -sf[:s]], [sf[:s], cf[:s]]]) * (1.0 / n)
    spec = _leftmm(m_filt, filt.reshape(1, n, c))[0]
    zv = z.reshape(g, n, c)
    zf = _leftmm(m_fwd, zv)
    y = _leftmm(m_inv, zf, spectrum=spec, gate=(zv, bias, x0.reshape(g, n, c)), out_dtype=BF16)
    return y.reshape(b, s, c)


def _attn_kernel(*refs, tk, n_chunks, heads, unroll):
    if n_chunks:
        q_ref, kc_ref, vc_ref, kl_ref, vl_ref, o_ref, s_ref, sc_ref = refs
    else:
        q_ref, kc_ref, vc_ref, o_ref = refs
    dn = (((1,), (1,)), ((), ()))
    qs = [q_ref[0, hh] for hh in range(heads)]

    def chunk_rows(j):
        return pl.ds(pl.multiple_of(j * tk, tk), tk)

    def scores(j, slot):
        for hh in range(heads):
            s_ref[slot, hh] = lax.dot_general(kl_ref[0, hh, chunk_rows(j), :], qs[hh], dn,
                                              preferred_element_type=F32)

    def absorb(j, slot, carry):
        out = []
        for hh in range(heads):
            m_old, acc_old = carry[hh]
            sj = s_ref[slot, hh]
            m_new = jnp.maximum(m_old, jnp.max(sj, axis=0, keepdims=True))
            pj = jnp.exp2(sj - m_new)
            acc_new = jnp.exp2(m_old - m_new) * acc_old + jnp.dot(vl_ref[0, hh, :, chunk_rows(j)], pj.astype(BF16),
                                                                  preferred_element_type=F32)
            out.append((m_new, acc_new))
        return tuple(out)

    def ctx_scores(hh):
        return lax.dot_general(kc_ref[0, hh], qs[hh], dn, preferred_element_type=F32)

    def merge_ctx(s, hh, m_old, acc_old):
        m_new = jnp.maximum(m_old, jnp.max(s, axis=0, keepdims=True))
        p = jnp.exp2(s - m_new)
        return m_new, jnp.exp2(m_old - m_new) * acc_old + jnp.dot(vc_ref[0, hh], p.astype(BF16),
                                                                  preferred_element_type=F32)

    def body(jj, carry):
        j = unroll * jj
        for u in range(unroll):
            nxt = j + u + 1
            if u == unroll - 1:
                nxt = jnp.minimum(nxt, n_chunks - 1)
            scores(nxt, (u + 1) % 2)
            carry = absorb(j + u, u % 2, carry)
        return carry

    tq = qs[0].shape[0]
    empty = (jnp.full((1, tq), -jnp.inf, F32), jnp.zeros((vc_ref.shape[2], tq), F32))
    if n_chunks:
        scores(0, 0)
        for hh in range(heads):
            sc_ref[hh] = ctx_scores(hh)
        state = lax.fori_loop(0, n_chunks // unroll, body, (empty,) * heads)
        state = [merge_ctx(sc_ref[hh], hh, *state[hh]) for hh in range(heads)]
    else:
        state = [merge_ctx(ctx_scores(hh), hh, *empty) for hh in range(heads)]
    outs = [acc[:V_DIM] / acc[V_DIM:V_DIM + 1] for _, acc in state]
    o_ref[0] = jnp.concatenate(outs, axis=0).T.astype(o_ref.dtype)


def _attention(q, kc, vct, kl=None, vlt=None):
    b, h, s, dp = q.shape
    nc = kc.shape[2]
    tq = min(ROW_TILE, s)
    heads = LANE // V_DIM
    tk = ATTN_KEY_CHUNK
    n_chunks = 0 if kl is None else kl.shape[2] // tk
    unroll = ATTN_CHUNKS_PER_TRIP if n_chunks % ATTN_CHUNKS_PER_TRIP == 0 else 2
    args = [q, kc, vct]
    in_specs = [pl.BlockSpec((1, heads, tq, dp), lambda bi, hi, i: (bi, hi, i, 0)),
                pl.BlockSpec((1, heads, nc, dp), lambda bi, hi, i: (bi, hi, 0, 0)),
                pl.BlockSpec((1, heads, dp, nc), lambda bi, hi, i: (bi, hi, 0, 0))]
    scratch = []
    if n_chunks:
        assert unroll % 2 == 0 and n_chunks % unroll == 0
        sl = kl.shape[2]
        args += [kl, vlt]
        in_specs += [pl.BlockSpec((1, heads, sl, dp), lambda bi, hi, i: (bi, hi, 0, 0)),
                     pl.BlockSpec((1, heads, dp, sl), lambda bi, hi, i: (bi, hi, 0, 0))]
        scratch = [pltpu.VMEM((2, heads, tk, tq), F32), pltpu.VMEM((heads, nc, tq), F32)]
    return pl.pallas_call(
        functools.partial(_attn_kernel, tk=tk, n_chunks=n_chunks, heads=heads, unroll=unroll),
        grid=(b, h // heads, s // tq),
        in_specs=in_specs,
        out_specs=pl.BlockSpec((1, tq, heads * V_DIM), lambda bi, hi, i: (bi, i, hi)),
        out_shape=jax.ShapeDtypeStruct((b, s, h * V_DIM), BF16),
        scratch_shapes=scratch,
        compiler_params=_cparams("parallel", "parallel", "parallel"),
        name="attention",
    )(*args)


def _outproj_kernel(x_ref, pool_ref, hy_ref, at_ref, wout_ref, g1_ref, gn_ref, sc_ref, sh_ref, rw_ref, rb_ref,
                    *rest):
    x1_ref, h2_ref, lg_ref = rest[-3:]
    mix = jnp.concatenate([pool_ref[0], hy_ref[0], at_ref[0]], axis=-1)
    y = jnp.dot(mix, wout_ref[...], preferred_element_type=F32)
    x1 = x_ref[0] + g1_ref[0] * y
    x1_ref[0] = x1
    h2 = _rms(x1, gn_ref[...], D_MODEL) * (1.0 + sc_ref[0]) + sh_ref[0]
    h2_ref[...] = _pack_rows(h2)
    h_hi = h2.astype(BF16)
    h_lo = (h2 - h_hi.astype(F32)).astype(BF16)
    nt = functools.partial(lax.dot_general, dimension_numbers=(((1,), (1,)), ((), ())), preferred_element_type=F32)
    lg_ref[...] = nt(rw_ref[0], h_hi) + nt(rw_ref[0], h_lo) + nt(rw_ref[1], h_hi) + rb_ref[...]


def _outproj(x, pool, hy, attn, seg_of_batch, wts, mod, n_tok, tok_offset, prior=None):
    b, s, d = x.shape
    t = min(2 * ROW_TILE, s)
    per = s // t
    off = tok_offset // t
    full = lambda shape: pl.BlockSpec(shape, lambda bi, i: (0,) * len(shape))
    seg = lambda: pl.BlockSpec((1, 1, d), lambda bi, i: (seg_of_batch(bi), 0, 0))
    rowblk = lambda w: pl.BlockSpec((1, t, w), lambda bi, i: (bi, i, 0))
    args = [x, pool, hy, attn, wts["w_out"], mod["g1"], wts["norm2_g"], mod["sc2"], mod["sh2"],
            wts["router_w"], wts["router_b"]]
    in_specs = [rowblk(d), rowblk(POOL_WIDTH), rowblk(HY_WIDTH), rowblk(MLA_WIDTH),
                full((d, d)), seg(), full((1, d)), seg(), seg(), full((2, N_EXPERTS, d)), full((N_EXPERTS, 1))]
    aliases = {}
    if prior is not None:
        aliases = {len(args): 1, len(args) + 1: 2}
        args += list(prior)
        in_specs += [pl.BlockSpec(memory_space=pl.ANY)] * 2
    return pl.pallas_call(
        _outproj_kernel,
        grid=(b, per),
        in_specs=in_specs,
        out_specs=[rowblk(d), pl.BlockSpec((t, d // 2), lambda bi, i: (off + bi * per + i, 0)),
                   pl.BlockSpec((N_EXPERTS, t), lambda bi, i: (0, off + bi * per + i))],
        out_shape=[jax.ShapeDtypeStruct((b, s, d), F32), jax.ShapeDtypeStruct((n_tok, d // 2), jnp.int32),
                   jax.ShapeDtypeStruct((N_EXPERTS, n_tok), F32)],
        input_output_aliases=aliases,
        compiler_params=_cparams("parallel", "parallel"),
        name="outproj",
    )(*args)


def _moe_kernel(be_ref, br_ref, x_ref, w1_ref, b1_ref, w2_ref, b2_ref, o_ref, w1b_ref, w2b_ref):
    i = pl.program_id(0)
    e = be_ref[i]
    e_prev = be_ref[jnp.maximum(i - 1, 0)]
    n_rows = br_ref[i]

    @pl.when((i == 0) | (e != e_prev))
    def _():
        w1b_ref[...] = w1_ref[0, 0].astype(BF16)
        w2b_ref[...] = w2_ref[0, 0].astype(BF16)

    @pl.when(n_rows > 0)
    def _():
        live = lax.broadcasted_iota(jnp.int32, (x_ref.shape[0], 1), 0) < n_rows
        x = _unpack_rows(jnp.where(live, x_ref[...], 0))
        a = jnp.dot(x, w1b_ref[...], preferred_element_type=F32) + b1_ref[0, 0]
        glu = jnp.minimum(a[:, :D_EXPERT], SWIGLU_LIMIT)
        lin = jnp.clip(a[:, D_EXPERT:], -SWIGLU_LIMIT, SWIGLU_LIMIT)
        act = glu * (1.0 / (1.0 + jnp.exp(-SWIGLU_ALPHA * glu))) * (lin + 1.0)
        y = jnp.dot(act.astype(BF16), w2b_ref[...], preferred_element_type=F32) + b2_ref[0, 0]
        o_ref[...] = _pack_rows(y)

    @pl.when(n_rows == 0)
    def _():
        o_ref[...] = jnp.zeros_like(o_ref)


def _experts(xs, block_e, block_rows, w1, b1, w2, b2, layer):
    n_buf, dh = xs.shape
    d = 2 * dh
    tm = EXPERT_TILE
    n_blocks = n_buf // tm
    de2 = w1.shape[3]
    grid_spec = pltpu.PrefetchScalarGridSpec(
        num_scalar_prefetch=2,
        grid=(n_blocks,),
        in_specs=[pl.BlockSpec((tm, dh), lambda i, be, nu: (i, 0)),
                  pl.BlockSpec((1, 1, d, de2), lambda i, be, nu: (layer, be[i], 0, 0)),
                  pl.BlockSpec((1, 1, 1, de2), lambda i, be, nu: (layer, be[i], 0, 0)),
                  pl.BlockSpec((1, 1, D_EXPERT, d), lambda i, be, nu: (layer, be[i], 0, 0)),
                  pl.BlockSpec((1, 1, 1, d), lambda i, be, nu: (layer, be[i], 0, 0))],
        out_specs=pl.BlockSpec((tm, dh), lambda i, be, nu: (i, 0)),
        scratch_shapes=[pltpu.VMEM((d, de2), BF16), pltpu.VMEM((D_EXPERT, d), BF16)],
    )
    return pl.pallas_call(
        _moe_kernel,
        grid_spec=grid_spec,
        out_shape=jax.ShapeDtypeStruct((n_buf, dh), jnp.int32),
        compiler_params=_cparams("arbitrary"),
        name="experts",
    )(block_e, block_rows, xs, w1, b1, w2, b2)


def _router_kernel(lg_ref, idx_ref, gate_ref, rank_ref, cnt_ref, base_ref):
    i = pl.program_id(0)
    n_e, t = lg_ref.shape

    @pl.when(i == 0)
    def _():
        base_ref[...] = jnp.zeros_like(base_ref)

    work = lg_ref[...]
    eid = lax.broadcasted_iota(jnp.int32, (n_e, t), 0)
    vals, idxs, sels = [], [], []
    for _ in range(TOP_K):
        m = jnp.max(work, axis=0, keepdims=True)
        idx = jnp.min(jnp.where(work == m, eid, n_e), axis=0, keepdims=True)
        sel = eid == idx
        work = jnp.where(sel, -jnp.inf, work)
        vals.append(m)
        idxs.append(idx)
        sels.append(sel)
    cnt = jnp.where(sels[0], 1.0, 0.0)
    for sel in sels[1:]:
        cnt = cnt + jnp.where(sel, 1.0, 0.0)
    earlier = (lax.broadcasted_iota(jnp.int32, (t, t), 0) < lax.broadcasted_iota(jnp.int32, (t, t), 1))
    within = jnp.dot(cnt.astype(BF16), jnp.where(earlier, 1.0, 0.0).astype(BF16), preferred_element_type=F32)
    pref = base_ref[:, 0:1] + within
    ranks = [jnp.sum(jnp.where(sel, pref, 0.0), axis=0, keepdims=True) for sel in sels]
    new_base = base_ref[...] + jnp.sum(cnt, axis=1, keepdims=True)
    base_ref[...] = new_base
    cnt_ref[...] = new_base
    ex = [jnp.exp(v - vals[0]) for v in vals]
    den = ex[0]
    for e in ex[1:]:
        den = den + e
    idx_ref[...] = jnp.concatenate(idxs, axis=0)
    gate_ref[...] = jnp.concatenate([e / den for e in ex], axis=0)
    rank_ref[...] = jnp.concatenate(ranks, axis=0).astype(jnp.int32)


def _route(logits_t):
    n_e, n_tok = logits_t.shape
    tm = EXPERT_TILE
    t = ROW_TILE
    tokblk = lambda: pl.BlockSpec((TOP_K, t), lambda i: (0, i))
    idx_t, gates_t, rank_t, counts_f = pl.pallas_call(
        _router_kernel,
        grid=(n_tok // t,),
        in_specs=[pl.BlockSpec((n_e, t), lambda i: (0, i))],
        out_specs=[tokblk(), tokblk(), tokblk(), pl.BlockSpec((n_e, LANE), lambda i: (0, 0))],
        out_shape=[jax.ShapeDtypeStruct((TOP_K, n_tok), jnp.int32), jax.ShapeDtypeStruct((TOP_K, n_tok), F32),
                   jax.ShapeDtypeStruct((TOP_K, n_tok), jnp.int32), jax.ShapeDtypeStruct((n_e, LANE), F32)],
        scratch_shapes=[pltpu.VMEM((n_e, LANE), F32)],
        compiler_params=_cparams("arbitrary"),
        name="router",
    )(logits_t)
    n_slots = n_tok * TOP_K
    counts = counts_f[:, 0].astype(jnp.int32)
    padded = (counts + tm - 1) // tm * tm
    padded_end = jnp.cumsum(padded)
    padded_start = padded_end - padded
    experts = jnp.arange(n_e, dtype=jnp.int32)
    dest_t = rank_t + jnp.sum(jnp.where(idx_t[None] == experts[:, None, None], padded_start[:, None, None], 0), axis=0)
    n_blocks = n_slots // tm + n_e
    block_start = jnp.arange(n_blocks, dtype=jnp.int32) * tm
    block_e = jnp.minimum(jnp.sum(padded_end[None, :] <= block_start[:, None], axis=1), n_e - 1).astype(jnp.int32)
    sel = (block_e[:, None] == experts[None, :])
    pick = lambda v: jnp.sum(jnp.where(sel, v[None, :], 0), axis=1)
    block_rows = jnp.clip(pick(padded_start) + pick(counts) - block_start, 0, tm).astype(jnp.int32)
    return gates_t.T, dest_t, block_e, block_rows


SC_WINDOW = 32
SC_SLOTS = 4


def _sc_gather(table, idx):
    _, d = table.shape
    b = idx.shape[0]
    info = plsc.get_sparse_core_info()
    nc, ns = info.num_cores, info.num_subcores
    nw = nc * ns
    w, nslot = SC_WINDOW, SC_SLOTS
    per_w = b // nw
    n_win = per_w // w
    assert per_w * nw == b and n_win * w == per_w and n_win % nslot == 0
    mesh = plsc.VectorSubcoreMesh(core_axis_name="c", subcore_axis_name="s")

    def body(table_hbm, idx_hbm, out_hbm, idx_v, rows_v, gsem, osem):
        wid = lax.axis_index("s") * nc + lax.axis_index("c")
        base = wid * per_w
        pltpu.sync_copy(idx_hbm.at[wid], idx_v)

        def gather(g, slot):
            return pltpu.make_async_copy(table_hbm.at[idx_v.at[g]], rows_v.at[slot], gsem.at[slot])

        def put(g, slot):
            return pltpu.make_async_copy(rows_v.at[slot], out_hbm.at[pl.ds(base + g * w, w)], osem.at[slot])

        for g in range(nslot - 1):
            gather(g, g).start()

        @pl.loop(0, n_win, step=nslot)
        def _(g0):
            for slot in range(nslot):
                g = g0 + slot
                gather(g, slot).wait()
                put(g, slot).start()
                free = (slot - 1) % nslot

                @pl.when(g + nslot - 1 < n_win)
                def _():
                    @pl.when(g >= 1)
                    def _():
                        put(g - 1, free).wait()

                    gather(g + nslot - 1, free).start()

        for g in range(n_win - nslot, n_win):
            put(g, g % nslot).wait()

    return pl.kernel(
        body,
        out_type=jax.ShapeDtypeStruct((b, d), table.dtype),
        mesh=mesh,
        scratch_types=[pltpu.VMEM((n_win, w), jnp.int32), pltpu.VMEM((nslot, w, d), table.dtype),
                       pltpu.SemaphoreType.DMA((nslot,)), pltpu.SemaphoreType.DMA((nslot,))],
        name="sc_gather",
    )(table, idx.reshape(nw, n_win, w))


SC_SCATTER_WINDOW = 16


def _sc_scatter(rows, dest_t, n_out):
    n_tok, d = rows.shape
    k = dest_t.shape[0]
    info = plsc.get_sparse_core_info()
    nc, ns = info.num_cores, info.num_subcores
    nw = nc * ns
    w = SC_SCATTER_WINDOW
    per_w = n_tok // nw
    n_win = per_w // w
    assert per_w * nw == n_tok and n_win * w == per_w and n_win % 2 == 0
    mesh = plsc.VectorSubcoreMesh(core_axis_name="c", subcore_axis_name="s")
    idx = dest_t.reshape(k, nw, n_win, w).transpose(1, 0, 2, 3)

    def body(rows_hbm, idx_hbm, out_hbm, idx_v, rows_v, lsem, ssem):
        wid = lax.axis_index("s") * nc + lax.axis_index("c")
        base = wid * per_w
        pltpu.sync_copy(idx_hbm.at[wid], idx_v)

        def load(g, slot):
            return pltpu.make_async_copy(rows_hbm.at[pl.ds(base + g * w, w)], rows_v.at[slot], lsem.at[slot])

        def scatter(g, slot, j):
            return pltpu.make_async_copy(rows_v.at[slot], out_hbm.at[idx_v.at[j, g]], ssem.at[slot])

        load(0, 0).start()

        @pl.loop(0, n_win, step=2)
        def _(g0):
            for slot in range(2):
                g = g0 + slot
                load(g, slot).wait()
                for j in range(k):
                    scatter(g, slot, j).start()

                @pl.when(g + 1 < n_win)
                def _():
                    @pl.when(g >= 1)
                    def _():
                        for j in range(k):
                            scatter(g - 1, 1 - slot, j).wait()

                    load(g + 1, 1 - slot).start()

        for g in (n_win - 2, n_win - 1):
            for j in range(k):
                scatter(g, g % 2, j).wait()

    return pl.kernel(
        body,
        out_type=jax.ShapeDtypeStruct((n_out, d), rows.dtype),
        mesh=mesh,
        scratch_types=[pltpu.VMEM((k, n_win, w), jnp.int32), pltpu.VMEM((2, w, d), rows.dtype),
                       pltpu.SemaphoreType.DMA((2,)), pltpu.SemaphoreType.DMA((2,))],
        name="sc_scatter",
    )(rows, idx)


def _combine_kernel(x_ref, y_ref, gt_ref, g2_ref, o_ref):
    gt = gt_ref[...]
    acc = gt[:, 0:1] * _unpack_rows(y_ref[0]).astype(F32)
    for j in range(1, TOP_K):
        acc = acc + gt[:, j:j + 1] * _unpack_rows(y_ref[j]).astype(F32)
    o_ref[0] = x_ref[0] + g2_ref[0] * acc


def _combine(x, ysel, gates, tok_offset, seg_of_batch, mod):
    b, s, d = x.shape
    t = min(ROW_TILE, s)
    per = s // t
    off = tok_offset // t
    return pl.pallas_call(
        _combine_kernel,
        grid=(b, per),
        in_specs=[pl.BlockSpec((1, t, d), lambda bi, i: (bi, i, 0)),
                  pl.BlockSpec((TOP_K, t, d // 2), lambda bi, i: (0, off + bi * per + i, 0)),
                  pl.BlockSpec((t, TOP_K), lambda bi, i: (off + bi * per + i, 0)),
                  pl.BlockSpec((1, 1, d), lambda bi, i: (seg_of_batch(bi), 0, 0))],
        out_specs=pl.BlockSpec((1, t, d), lambda bi, i: (bi, i, 0)),
        out_shape=jax.ShapeDtypeStruct((b, s, d), F32),
        compiler_params=_cparams("parallel", "parallel"),
        name="combine",
    )(x, ysel, gates, mod["g2"])


def _rope_tables(seq_len):
    n_rows = seq_len // GRID_W
    row = jnp.repeat(jnp.arange(n_rows, dtype=F32), GRID_W)
    col = jnp.tile(jnp.arange(GRID_W, dtype=F32), n_rows)
    pairs = ROPE // 4
    inv = ROPE_BASE ** (-jnp.arange(pairs, dtype=F32) / pairs)
    ang = jnp.concatenate([row[:, None] * inv, col[:, None] * inv], axis=-1)
    cos, sin = jnp.cos(ang), jnp.sin(ang)
    one = jnp.ones((seq_len, NOPE), F32)
    zn = jnp.zeros((seq_len, NOPE), F32)
    zp = jnp.zeros((seq_len, HEAD_PAD - QK_DIM), F32)
    return jnp.concatenate([one, cos, cos, zp], axis=-1), jnp.concatenate([zn, sin, sin, zp], axis=-1)


def _identity_rope(seq_len):
    keep = jnp.concatenate([jnp.ones((seq_len, QK_DIM), F32), jnp.zeros((seq_len, HEAD_PAD - QK_DIM), F32)], axis=-1)
    return keep, jnp.zeros((seq_len, HEAD_PAD), F32)


def _rotate_half(w):
    half = ROPE // 2
    x1, x2 = w[..., NOPE:NOPE + half], w[..., NOPE + half:QK_DIM]
    return jnp.concatenate([jnp.zeros_like(w[..., :NOPE]), -x2, x1, jnp.zeros_like(w[..., QK_DIM:])], axis=-1)


def _swap_rope_halves(g):
    half = ROPE // 2
    lo, hi = g[..., NOPE:NOPE + half], g[..., NOPE + half:QK_DIM]
    return jnp.concatenate([g[..., :NOPE], hi, lo, g[..., QK_DIM:]], axis=-1)


def _pad_cols(w, n):
    return jnp.pad(w, ((0, 0), (0, n - w.shape[1])))


def _layer_weights(l, p):
    hw = HEADS * HEAD_PAD
    wq = jnp.pad(p["mla_w_uq"][l].reshape(Q_RANK, HEADS, QK_DIM), ((0, 0), (0, 0), (0, HEAD_PAD - QK_DIM)))
    wukv = p["mla_w_ukv"][l].reshape(KV_RANK, HEADS, NOPE + V_DIM)
    wk = jnp.pad(wukv[:, :, :NOPE], ((0, 0), (0, 0), (0, HEAD_PAD - NOPE)))
    wv = jnp.pad(wukv[:, :, NOPE:], ((0, 0), (0, 0), (0, HEAD_PAD - V_DIM))).reshape(KV_RANK, hw)
    place = jnp.pad(jnp.eye(ROPE, dtype=F32), ((0, KV_RANK - ROPE), (NOPE, HEAD_PAD - QK_DIM)))
    wk = jnp.concatenate([wk, jnp.tile(place[:, None, :], (1, HEADS, 1))], axis=0)
    with_rot = lambda w: jnp.concatenate([w.reshape(-1, hw), _rotate_half(w).reshape(-1, hw)], axis=1).astype(BF16)
    wpool = jax.scipy.linalg.block_diag(*[p["pool_w"][l, g] for g in range(POOL_GROUPS)])
    vone = (jnp.arange(HEAD_PAD) == V_DIM).astype(F32)[:, None]
    deltas = jnp.abs(jnp.linspace(HY_MIN_DECAY, HY_MAX_DECAY, HY_WIDTH, dtype=F32))[None, :]
    row = lambda v, n=None: (v if n is None else jnp.pad(v, (0, n - v.shape[0])))[None, :]
    twice = lambda v: jnp.concatenate([v, v])[None, :]
    return {
        "norm1_g": row(p["norm1_g"][l]), "norm2_g": row(p["norm2_g"][l]),
        "w_in": _pad_cols(p["w_in"][l], N_IN_PAD).astype(BF16),
        "gq": row(p["mla_q_norm_g"][l]), "wq": with_rot(wq),
        "gkv": row(p["mla_kv_norm_g"][l]), "wk": with_rot(wk), "wvt": wv.T.astype(BF16),
        "gqq": row(p["qk_norm_q"][l], HEAD_PAD), "gqq_rot": _swap_rope_halves(row(p["qk_norm_q"][l], HEAD_PAD)),
        "gqk": row(p["qk_norm_k"][l], HEAD_PAD), "gqk_rot": _swap_rope_halves(row(p["qk_norm_k"][l], HEAD_PAD)),
        "vone": vone,
        "wpool": wpool.astype(BF16), "pool_scale": row(p["pool_scale"][l]),
        "conv_w": p["hy_conv_w"][l], "conv_b": row(p["hy_conv_b"][l]),
        "f_w1a": jnp.pad(p["hy_f_w1"][l], ((0, LANE - HY_EMB), (0, LANE - HY_HID))),
        "f_w1b": jnp.pad(p["hy_f_w1"][l], ((0, LANE - HY_EMB), (LANE - HY_HID, 0))),
        "f_b1": twice(p["hy_f_b1"][l]),
        "f_w2": jax.scipy.linalg.block_diag(p["hy_f_w2"][l], p["hy_f_w2"][l]),
        "f_b2": twice(p["hy_f_b2"][l]),
        "f_w3a": jnp.pad(p["hy_f_w3"][l][:, :HY_WIDTH], ((0, LANE - HY_HID), (0, 0))),
        "f_w3b": jnp.pad(p["hy_f_w3"][l][:, HY_WIDTH:], ((LANE - HY_HID, 0), (0, 0))),
        "f_freq": twice(p["hy_freq"][l]), "f_delta": deltas,
        "hy_bias": row(p["hy_bias"][l]),
        "w_out": p["w_out"][l].astype(BF16),
        "router_w": _split_const(p["router_w"][l].T), "router_b": p["router_b"][l][:, None],
    }


def _layer(l, x, xc, cc, wts, p, rope_main, rope_ctx, last):
    b, s, d = x.shape
    n_ctx = xc.shape[1]
    modm = _modulation(cc, p["w_mod"], p["b_mod"][l][None, :], l)
    names = ("sh1", "sc1", "g1", "sh2", "sc2", "g2")
    mod = {nm: modm[:, i * d:(i + 1) * d].reshape(N_SEG, 1, d) for i, nm in enumerate(names)}
    main_seg = lambda bi: bi
    ctx_seg = lambda bi: CTX_SEG

    pph, q, kl, vl = _inproj(x, main_seg, wts, mod, rope_main)
    pph_c, qc, kc, vc = _inproj(xc, ctx_seg, wts, mod, rope_ctx)
    attn = _attention(q, kc, vc, kl, vl)
    pool, x0, z = _seqmix(pph, wts)
    hy = _hyena_long(z, x0, wts["hy_bias"], _hyena_filter(s, wts))
    n_c = 0 if last else b * n_ctx
    n_tok = n_c + b * s
    init = None if last else (jnp.zeros((n_tok, d // 2), jnp.int32), jnp.zeros((N_EXPERTS, n_tok), F32))
    x1, h2, logits = _outproj(x, pool, hy, attn, main_seg, wts, mod, n_tok, n_c, prior=init)
    if not last:
        attn_c = _attention(qc, kc, vc)
        pool_c, x0c, zc = _seqmix(pph_c, wts)
        hy_c = _hyena_short(zc, x0c, wts["hy_bias"], _hyena_filter(n_ctx, wts))
        xc1, h2, logits = _outproj(xc, pool_c, hy_c, attn_c, ctx_seg, wts, mod, n_tok, 0, prior=(h2, logits))

    gates, dest_t, block_e, block_rows = _route(logits)
    de2 = p["moe_w1"].shape[-1]
    xs = _sc_scatter(h2, dest_t, block_e.shape[0] * EXPERT_TILE)
    ys = _experts(xs, block_e, block_rows, p["moe_w1"], p["moe_b1"].reshape(-1, N_EXPERTS, 1, de2),
                  p["moe_w2"], p["moe_b2"].reshape(-1, N_EXPERTS, 1, d), l)
    ysel = _sc_gather(ys, dest_t.reshape(-1)).reshape(TOP_K, -1, d // 2)
    x2 = _combine(x1, ysel, gates, n_c, main_seg, mod)
    if not last:
        xc = _combine(xc1, ysel, gates, 0, ctx_seg, mod)
    return x2, xc


def kernel(x, c, ctx, c_ctx, norm1_g, norm2_g, w_mod, b_mod, w_in, pool_w, pool_scale, hy_conv_w, hy_conv_b,
           hy_f_w1, hy_f_b1, hy_f_w2, hy_f_b2, hy_f_w3, hy_freq, hy_bias, mla_q_norm_g, mla_w_uq, mla_kv_norm_g,
           mla_w_ukv, qk_norm_q, qk_norm_k, w_out, router_w, router_b, moe_w1, moe_b1, moe_w2, moe_b2):
    p = dict(norm1_g=norm1_g, norm2_g=norm2_g, w_mod=w_mod, b_mod=b_mod, w_in=w_in, pool_w=pool_w,
             pool_scale=pool_scale,
             hy_conv_w=hy_conv_w, hy_conv_b=hy_conv_b, hy_f_w1=hy_f_w1, hy_f_b1=hy_f_b1, hy_f_w2=hy_f_w2,
             hy_f_b2=hy_f_b2, hy_f_w3=hy_f_w3, hy_freq=hy_freq, hy_bias=hy_bias, mla_q_norm_g=mla_q_norm_g,
             mla_w_uq=mla_w_uq, mla_kv_norm_g=mla_kv_norm_g, mla_w_ukv=mla_w_ukv, qk_norm_q=qk_norm_q,
             qk_norm_k=qk_norm_k, w_out=w_out, router_w=router_w, router_b=router_b, moe_w1=moe_w1,
             moe_b1=moe_b1, moe_w2=moe_w2, moe_b2=moe_b2)
    b, s, d = x.shape
    depth = w_mod.shape[0]
    assert b <= CTX_SEG
    cc = jnp.concatenate([c, jnp.zeros((CTX_SEG - b, d), F32), c_ctx[None, :],
                          jnp.zeros((N_SEG - CTX_SEG - 1, d), F32)], axis=0)
    rope_main = _rope_tables(s)
    rope_ctx = _identity_rope(ctx.shape[1])
    xc = ctx
    for l in range(depth):
        wts = _layer_weights(l, p)
        x, xc = _layer(l, x, xc, cc, wts, p, rope_main, rope_ctx, l == depth - 1)
    return x
```
